```python
import math
import jax, jax.numpy as jnp
from jax import lax
import numpy as np

D_MODEL = 1024
BATCH = 16
SEQ = 2048
DEPTH = 2

CHUNK = 64
Q_BLOCK = 128
N_MIXERS = 2
ATTN_HEADS = 8
ATTN_HEAD_DIM = D_MODEL // (2 * ATTN_HEADS)
ALIBI_MAX_BIAS = 8.0
MLSTM_INNER = 2 * D_MODEL
MLSTM_HEADS = 4
MLSTM_HEAD_DIM = MLSTM_INNER // MLSTM_HEADS
MLSTM_CONV = 4
QKV_BLOCK = 4
FFN_DENSE = ((8 * D_MODEL // 3 + 127) // 128) * 128
N_EXPERTS = 8
TOP_K = 2
FFN_EXPERT = 7 * D_MODEL // 2
N_ATTN_LAYERS = (DEPTH + 1) // 2
N_MLSTM_LAYERS = DEPTH // 2
NORM_EPS = 1e-6

kernel_name = 'hybrid_diffattn_mlstm_moe_trunk'


def rmsnorm(x, g):
    xf = x.astype(jnp.float32)
    y = xf * lax.rsqrt(jnp.mean(xf * xf, axis=-1, keepdims=True) + NORM_EPS)
    return (y * g.astype(jnp.float32)).astype(x.dtype)


def head_layernorm(x, g):
    xf = x.astype(jnp.float32)
    mu = jnp.mean(xf, axis=-1, keepdims=True)
    var = jnp.mean(jnp.square(xf - mu), axis=-1, keepdims=True)
    return (xf - mu) * lax.rsqrt(var + NORM_EPS) * g.astype(jnp.float32)


def modulate(x, shift, scale):
    return x * (1.0 + scale[:, None, :]) + shift[:, None, :]


def swiglu(h, w_gate, w_up, w_down):
    return (jax.nn.silu(h @ w_gate) * (h @ w_up)) @ w_down


def alibi_slopes(n_heads):
    return jnp.exp2(-ALIBI_MAX_BIAS * jnp.arange(1, n_heads + 1, dtype=jnp.float32) / n_heads)


def diff_attention(h, w_in, w_out, lam, subln_g, lambda_init):
    B, S, D = h.shape
    H, DA = ATTN_HEADS, ATTN_HEAD_DIM
    q, k, v = jnp.split(h @ w_in, 3, axis=-1)
    q = q.reshape(B, S, H, 2, DA) * (DA ** -0.5)
    k = k.reshape(B, S, H, 2, DA)
    v = v.reshape(B, S, H, 2 * DA)
    lam_f = lam.astype(jnp.float32)
    lam_full = (jnp.exp(jnp.sum(lam_f[0] * lam_f[1])) - jnp.exp(jnp.sum(lam_f[2] * lam_f[3]))
                + lambda_init)
    slopes = alibi_slopes(H)
    key_pos = jnp.arange(S)
    n_blk = S // Q_BLOCK
    q_blocks = jnp.moveaxis(q.reshape(B, n_blk, Q_BLOCK, H, 2, DA), 1, 0)

    def one_block(args):
        q_blk, blk = args
        qpos = blk * Q_BLOCK + jnp.arange(Q_BLOCK)
        dist = jnp.abs(qpos[:, None] - key_pos[None, :]).astype(jnp.float32)
        allowed = (key_pos[None, :] // CHUNK) <= (qpos[:, None] // CHUNK)
        bias = jnp.where(allowed[None], -slopes[:, None, None] * dist[None], -jnp.inf)
        s = jnp.einsum('bqhcd,bkhcd->bhcqk', q_blk, k, preferred_element_type=jnp.float32)
        p = jax.nn.softmax(s + bias[None, :, None], axis=-1)
        a = p[:, :, 0] - lam_full * p[:, :, 1]
        return jnp.einsum('bhqk,bkhe->bqhe', a.astype(v.dtype), v)

    o = lax.map(one_block, (q_blocks, jnp.arange(n_blk)))
    o = jnp.moveaxis(o, 0, 1).reshape(B, S, H, 2 * DA)
    o = rmsnorm(o, subln_g) * (1.0 - lambda_init)
    return o.reshape(B, S, D) @ w_out


def mlstm_cell(q, k, v, i_pre, f_pre):
    B, S, NH, DH = q.shape
    L = CHUNK
    NC = S // L

    def to_chunks(t):
        t = t.astype(jnp.float32).reshape((B, NC, L) + t.shape[2:])
        return jnp.moveaxis(jnp.moveaxis(t, 1, 0), 3, 2)

    qc = to_chunks(q)
    kc = to_chunks(k) * (DH ** -0.5)
    vc = to_chunks(v)
    ic = to_chunks(i_pre)
    bc = jnp.cumsum(jax.nn.log_sigmoid(to_chunks(f_pre)), axis=-1)
    causal = jnp.tril(jnp.ones((L, L), dtype=bool))

    def step(carry, inp):
        C, n, m = carry
        q_, k_, v_, i_, b_ = inp
        dlog = jnp.where(causal, b_[..., :, None] - b_[..., None, :] + i_[..., None, :], -jnp.inf)
        g = b_ + m[..., None]
        m_t = jnp.maximum(g, jnp.max(dlog, axis=-1))
        w = jnp.exp(dlog - m_t[..., None]) * jnp.einsum('bhtd,bhsd->bhts', q_, k_)
        inter = jnp.exp(g - m_t)
        num = (jnp.einsum('bhts,bhsd->bhtd', w, v_)
               + inter[..., None] * jnp.einsum('bhvd,bhtd->bhtv', C, q_))
        den = jnp.sum(w, axis=-1) + inter * jnp.einsum('bhd,bhtd->bht', n, q_)
        h = num / jnp.maximum(jnp.abs(den), jnp.exp(-m_t))[..., None]
        b_last = b_[..., -1]
        w_state = b_last[..., None] - b_ + i_
        m_new = jnp.maximum(b_last + m, jnp.max(w_state, axis=-1))
        decay = jnp.exp(b_last + m - m_new)
        ws = jnp.exp(w_state - m_new[..., None])
        C_new = decay[..., None, None] * C + jnp.einsum('bhsv,bhsd->bhvd', v_ * ws[..., None], k_)
        n_new = decay[..., None] * n + jnp.einsum('bhs,bhsd->bhd', ws, k_)
        return (C_new, n_new, m_new), h

    init = (jnp.zeros((B, NH, DH, DH), jnp.float32),
            jnp.zeros((B, NH, DH), jnp.float32),
            jnp.zeros((B, NH), jnp.float32))
    _, hs = lax.scan(step, init, (qc, kc, vc, ic, bc))
    hs = jnp.moveaxis(jnp.moveaxis(hs, 2, 3), 0, 1)
    return hs.reshape(B, S, NH, DH)


def mlstm_layer(h, w_in, conv_w, conv_b, w_q, w_k, w_v, w_gate, b_gate, skip, norm_g, w_out):
    B, S, _ = h.shape
    NH, DH = MLSTM_HEADS, MLSTM_HEAD_DIM
    x_m, z = jnp.split(h @ w_in, 2, axis=-1)
    xp = jnp.pad(x_m, ((0, 0), (MLSTM_CONV - 1, 0), (0, 0)))
    conv = sum(xp[:, j:j + S] * conv_w[j] for j in range(MLSTM_CONV)) + conv_b
    x_c = jax.nn.silu(conv)

    def headwise(t, w):
        return jnp.einsum('bsgi,gio->bsgo', t.reshape(B, S, -1, QKV_BLOCK), w).reshape(B, S, MLSTM_INNER)

    q = headwise(x_c, w_q)
    k = headwise(x_c, w_k)
    v = headwise(x_m, w_v)
    gates = (jnp.concatenate([q, k, v], axis=-1) @ w_gate + b_gate).astype(jnp.float32)
    i_pre, f_pre = gates[..., :NH], gates[..., NH:]
    hh = mlstm_cell(q.reshape(B, S, NH, DH), k.reshape(B, S, NH, DH), v.reshape(B, S, NH, DH), i_pre, f_pre)
    hh = head_layernorm(hh, norm_g.reshape(NH, DH)).reshape(B, S, MLSTM_INNER).astype(x_c.dtype)
    out = (hh + skip * x_c) * jax.nn.silu(z)
    return out @ w_out


def moe_swiglu(h, w_router, b_router, w_gate, w_up, w_down):
    B, S, D = h.shape
    t = h.reshape(B * S, D)
    logits = (t @ w_router).astype(jnp.float32) + b_router.astype(jnp.float32)
    probs = jax.nn.softmax(logits, axis=-1)
    top_p, top_i = lax.top_k(probs, TOP_K)
    top_p = top_p / jnp.sum(top_p, axis=-1, keepdims=True)
    combine = jnp.sum(jax.nn.one_hot(top_i, N_EXPERTS, dtype=jnp.float32) * top_p[..., None], axis=1)
    y = jnp.zeros_like(t)
    for e in range(N_EXPERTS):
        y = y + combine[:, e:e + 1].astype(t.dtype) * swiglu(t, w_gate[e], w_up[e], w_down[e])
    return y.reshape(B, S, D)


def setup_inputs(seed: int = 0) -> dict:
    key = jax.random.key(seed)
    ks = jax.random.split(key, 32)
    f32 = jnp.float32
    D, NA, NM = D_MODEL, N_ATTN_LAYERS, N_MLSTM_LAYERS
    DI, NH, DA = MLSTM_INNER, MLSTM_HEADS, ATTN_HEAD_DIM
    G = DI // QKV_BLOCK

    def nrm(k, shape, scale):
        return jax.random.normal(k, shape, f32) * scale

    b_gate = jnp.concatenate([nrm(ks[19], (NM, NH), 0.1),
                              jnp.linspace(3.0, 6.0, NH, dtype=f32)[None, :] + nrm(ks[20], (NM, NH), 0.1)], axis=-1)
    return {
        'x': nrm(ks[0], (BATCH, SEQ, D), 1.0),
        'c': nrm(ks[1], (BATCH, D), 1.0),
        'mod_w': nrm(ks[2], (DEPTH, D, 6 * D), D ** -0.5),
        'mod_b': nrm(ks[3], (DEPTH, 6 * D), 0.02),
        'norm_g': 1.0 + nrm(ks[4], (DEPTH, 4, D), 0.05),
        'attn_w_in': nrm(ks[5], (NA, D, 3 * D), D ** -0.5),
        'attn_w_out': nrm(ks[6], (NA, D, D), D ** -0.5),
        'attn_lambda': nrm(ks[7], (NA, 4, DA), 0.1),
        'attn_subln': 1.0 + nrm(ks[8], (NA, 2 * DA), 0.05),
        'ffn_w_gate': nrm(ks[9], (NA, D, FFN_DENSE), D ** -0.5),
        'ffn_w_up': nrm(ks[10], (NA, D, FFN_DENSE), D ** -0.5),
        'ffn_w_down': nrm(ks[11], (NA, FFN_DENSE, D), FFN_DENSE ** -0.5),
        'mlstm_w_in': nrm(ks[12], (NM, D, 2 * DI), D ** -0.5),
        'mlstm_conv_w': nrm(ks[13], (NM, MLSTM_CONV, DI), MLSTM_CONV ** -0.5),
        'mlstm_conv_b': nrm(ks[14], (NM, DI), 0.02),
        'mlstm_w_q': nrm(ks[15], (NM, G, QKV_BLOCK, QKV_BLOCK), QKV_BLOCK ** -0.5),
        'mlstm_w_k': nrm(ks[16], (NM, G, QKV_BLOCK, QKV_BLOCK), QKV_BLOCK ** -0.5),
        'mlstm_w_v': nrm(ks[17], (NM, G, QKV_BLOCK, QKV_BLOCK), QKV_BLOCK ** -0.5),
        'mlstm_w_gate': nrm(ks[18], (NM, 3 * DI, 2 * NH), (3 * DI) ** -0.5),
        'mlstm_b_gate': b_gate,
        'mlstm_skip': 1.0 + nrm(ks[21], (NM, DI), 0.05),
        'mlstm_norm': 1.0 + nrm(ks[22], (NM, DI), 0.05),
        'mlstm_w_out': nrm(ks[23], (NM, DI, D), DI ** -0.5),
        'moe_w_router': nrm(ks[24], (NM, D, N_EXPERTS), D ** -0.5),
        'moe_b_router': nrm(ks[25], (NM, N_EXPERTS), 0.01),
        'moe_w_gate': nrm(ks[26], (NM, N_EXPERTS, D, FFN_EXPERT), D ** -0.5),
        'moe_w_up': nrm(ks[27], (NM, N_EXPERTS, D, FFN_EXPERT), D ** -0.5),
        'moe_w_down': nrm(ks[28], (NM, N_EXPERTS, FFN_EXPERT, D), FFN_EXPERT ** -0.5),
    }


def reference(x, c, mod_w, mod_b, norm_g, attn_w_in, attn_w_out, attn_lambda, attn_subln,
              ffn_w_gate, ffn_w_up, ffn_w_down, mlstm_w_in, mlstm_conv_w, mlstm_conv_b,
              mlstm_w_q, mlstm_w_k, mlstm_w_v, mlstm_w_gate, mlstm_b_gate, mlstm_skip, mlstm_norm,
              mlstm_w_out, moe_w_router, moe_b_router, moe_w_gate, moe_w_up, moe_w_down):
    B = x.shape[0]
    cond = jax.nn.silu(c)
    for i in range(DEPTH):
        mod = (cond @ mod_w[i] + mod_b[i]).reshape(B, 6, D_MODEL)
        g = norm_g[i]
        j = i // N_MIXERS
        h = modulate(rmsnorm(x, g[0]), mod[:, 0], mod[:, 1])
        if i % N_MIXERS == 0:
            lambda_init = 0.8 - 0.6 * math.exp(-0.3 * i)
            y = diff_attention(h, attn_w_in[j], attn_w_out[j], attn_lambda[j], attn_subln[j], lambda_init)
        else:
            y = mlstm_layer(h, mlstm_w_in[j], mlstm_conv_w[j], mlstm_conv_b[j], mlstm_w_q[j], mlstm_w_k[j],
                            mlstm_w_v[j], mlstm_w_gate[j], mlstm_b_gate[j], mlstm_skip[j], mlstm_norm[j],
                            mlstm_w_out[j])
        x = x + mod[:, 2][:, None, :] * rmsnorm(y, g[1])
        h = modulate(rmsnorm(x, g[2]), mod[:, 3], mod[:, 4])
        if i % 2 == 0:
            y = swiglu(h, ffn_w_gate[j], ffn_w_up[j], ffn_w_down[j])
        else:
            y = moe_swiglu(h, moe_w_router[j], moe_b_router[j], moe_w_gate[j], moe_w_up[j], moe_w_down[j])
        x = x + mod[:, 5][:, None, :] * rmsnorm(y, g[3])
    return x
```

```python
import functools
import math

import jax
import jax.numpy as jnp
from jax import lax
from jax.experimental import pallas as pl
from jax.experimental.pallas import tpu as pltpu

F32 = jnp.float32
BF16 = jnp.bfloat16
HIGHEST = lax.Precision.HIGHEST

CHUNK = 64
ATTN_HEADS = 8
ALIBI_MAX_BIAS = 8.0
MLSTM_HEADS = 4
MLSTM_CONV = 4
QKV_BLOCK = 4
TOP_K = 2
NORM_EPS = 1e-6
N_MIXERS = 2

LANES = 128
SUBLANES = 8
VMEM_LIMIT_BYTES = 56 * 1024 * 1024
NEG_BIG = -1e30


def _cparams(*sem):
    return pltpu.CompilerParams(dimension_semantics=sem, vmem_limit_bytes=VMEM_LIMIT_BYTES)


def _rms(x, g):
    return x * lax.rsqrt(jnp.mean(x * x, axis=-1, keepdims=True) + NORM_EPS) * g


def _silu(x):
    return x * jax.nn.sigmoid(x)


def _token_tile(s):
    return min(512, s)


def _mod_kernel(c_ref, w_ref, b_ref, o_ref):
    cond = _silu(c_ref[...])
    o_ref[0] = jnp.dot(cond, w_ref[0], preferred_element_type=F32, precision=HIGHEST) + b_ref[0]


def _modulation(c, mod_w, mod_b):
    depth, d, n = mod_w.shape
    b = c.shape[0]
    tn = 1536 if n % 1536 == 0 else n
    out = pl.pallas_call(
        _mod_kernel,
        grid=(depth, n // tn),
        in_specs=[pl.BlockSpec((b, d), lambda i, j: (0, 0)),
                  pl.BlockSpec((1, d, tn), lambda i, j: (i, 0, j)),
                  pl.BlockSpec((1, 1, tn), lambda i, j: (i, 0, j))],
        out_specs=pl.BlockSpec((1, b, tn), lambda i, j: (i, 0, j)),
        out_shape=jax.ShapeDtypeStruct((depth, b, n), F32),
        compiler_params=_cparams("arbitrary", "arbitrary"),
        name="modulation",
    )(c, mod_w, mod_b.reshape(depth, 1, n))
    return out.reshape(depth, b, 6, d)


def _in_proj_kernel(x_ref, mod_ref, g_ref, w_ref, *o_refs, scales):
    x = x_ref[...]
    h = _rms(x, g_ref[0:1, :]) * (1.0 + mod_ref[0, 1:2, :]) + mod_ref[0, 0:1, :]
    hb = h.astype(BF16)
    n = o_refs[0].shape[1]
    for idx, o_ref in enumerate(o_refs):
        r = jnp.dot(hb, w_ref[:, idx * n:(idx + 1) * n], preferred_element_type=F32)
        if scales[idx] != 1.0:
            r = r * scales[idx]
        o_ref[...] = r.astype(BF16)


def _in_proj(x2d, mod, g, w_bf16, n_out, scales, s, name):
    t, d = x2d.shape
    tm = _token_tile(s)
    tpb = s // tm
    n = w_bf16.shape[1] // n_out
    return pl.pallas_call(
        functools.partial(_in_proj_kernel, scales=scales),
        grid=(t // tm,),
        in_specs=[pl.BlockSpec((tm, d), lambda i: (i, 0)),
                  pl.BlockSpec((1, 6, d), lambda i: (i // tpb, 0, 0)),
                  pl.BlockSpec((4, d), lambda i: (0, 0)),
                  pl.BlockSpec(w_bf16.shape, lambda i: (0, 0))],
        out_specs=[pl.BlockSpec((tm, n), lambda i: (i, 0))] * n_out,
        out_shape=[jax.ShapeDtypeStruct((t, n), BF16)] * n_out,
        compiler_params=_cparams("arbitrary"),
        name=name,
    )(x2d, mod, g, w_bf16)


def _attn_kernel(slopes_ref, lam_ref, subln_ref, q_ref, k_ref, v_ref, o_ref, m_sc, l_sc, acc_sc,
                 *, tq, tk, lambda_init):
    h = pl.program_id(1)
    qi = pl.program_id(2)
    slope = slopes_ref[h]
    half = q_ref.shape[2] // 2
    q = q_ref[0]
    lane = lax.broadcasted_iota(jnp.int32, q.shape, 1)
    zero = jnp.zeros_like(q)
    q2 = jnp.concatenate([jnp.where(lane < half, q, zero), jnp.where(lane >= half, q, zero)], axis=0)

    m_sc[...] = jnp.full(m_sc.shape, NEG_BIG, F32)
    l_sc[...] = jnp.zeros(l_sc.shape, F32)
    acc_sc[...] = jnp.zeros(acc_sc.shape, F32)

    row = lax.broadcasted_iota(jnp.int32, (tq, tk), 0)
    col = lax.broadcasted_iota(jnp.int32, (tq, tk), 1)
    q_base = qi * tq

    def update(j, masked):
        k_base = pl.multiple_of(j * tk, tk)
        kj = k_ref[0, pl.ds(k_base, tk), :]
        vj = v_ref[0, pl.ds(k_base, tk), :]
        s = lax.dot_general(q2, kj, (((1,), (1,)), ((), ())), preferred_element_type=F32)
        qpos = row + q_base
        kpos = col + k_base
        dist = (qpos - kpos).astype(F32)
        if masked:
            bias = jnp.where(kpos // CHUNK <= qpos // CHUNK, -slope * jnp.abs(dist), NEG_BIG)
        else:
            bias = -slope * dist
        s = s + jnp.concatenate([bias, bias], axis=0)
        m_prev = m_sc[...]
        m_new = jnp.maximum(m_prev, jnp.max(s, axis=1, keepdims=True))
        alpha = jnp.exp(m_prev - m_new)
        p = jnp.exp(s - m_new)
        l_sc[...] = alpha * l_sc[...] + jnp.sum(p, axis=1, keepdims=True)
        acc_sc[...] = alpha * acc_sc[...] + jnp.dot(p.astype(BF16), vj, preferred_element_type=F32)
        m_sc[...] = m_new

    n_full = q_base // tk

    def body(j, carry):
        update(j, False)
        return carry

    lax.fori_loop(0, n_full, body, 0)
    update(n_full, True)

    lam = lam_ref[...]
    lam_full = (jnp.exp(jnp.sum(lam[0:1, :] * lam[1:2, :], axis=1, keepdims=True))
                - jnp.exp(jnp.sum(lam[2:3, :] * lam[3:4, :], axis=1, keepdims=True)) + lambda_init)
    o_all = acc_sc[...] / l_sc[...]
    o = o_all[:tq] - lam_full * o_all[tq:]
    o = _rms(o, subln_ref[...]) * (1.0 - lambda_init)
    o_ref[0] = o.astype(BF16)


def _diff_attention(q, k, v, lam, subln, lambda_init, b, s):
    d = q.shape[1]
    hd = d // ATTN_HEADS
    tq = min(256, s)
    tk = min(512, s)
    slopes = jnp.exp2(-ALIBI_MAX_BIAS * jnp.arange(1, ATTN_HEADS + 1, dtype=F32) / ATTN_HEADS)
    q3, k3, v3 = (a.reshape(b, s, d) for a in (q, k, v))
    grid_spec = pltpu.PrefetchScalarGridSpec(
        num_scalar_prefetch=1,
        grid=(b, ATTN_HEADS, s // tq),
        in_specs=[pl.BlockSpec(lam.shape, lambda bi, h, qi, sl: (0, 0)),
                  pl.BlockSpec((1, hd), lambda bi, h, qi, sl: (0, 0)),
                  pl.BlockSpec((1, tq, hd), lambda bi, h, qi, sl: (bi, qi, h)),
                  pl.BlockSpec((1, s, hd), lambda bi, h, qi, sl: (bi, 0, h)),
                  pl.BlockSpec((1, s, hd), lambda bi, h, qi, sl: (bi, 0, h))],
        out_specs=pl.BlockSpec((1, tq, hd), lambda bi, h, qi, sl: (bi, qi, h)),
        scratch_shapes=[pltpu.VMEM((2 * tq, 1), F32), pltpu.VMEM((2 * tq, 1), F32),
                        pltpu.VMEM((2 * tq, hd), F32)],
    )
    o = pl.pallas_call(
        functools.partial(_attn_kernel, tq=tq, tk=tk, lambda_init=lambda_init),
        grid_spec=grid_spec,
        out_shape=jax.ShapeDtypeStruct((b, s, d), BF16),
        compiler_params=_cparams("arbitrary", "arbitrary", "arbitrary"),
        name="diff_attention",
    )(slopes, lam, subln.reshape(1, hd), q3, k3, v3)
    return o.reshape(b * s, d)


def _mixer_out_core(inp_bf16, w_ref, x_ref, mod_ref, g_ref, x_out_ref):
    y = jnp.dot(inp_bf16, w_ref[...], preferred_element_type=F32)
    x1 = x_ref[...] + mod_ref[0, 2:3, :] * _rms(y, g_ref[1:2, :])
    x_out_ref[...] = x1
    return _rms(x1, g_ref[2:3, :]) * (1.0 + mod_ref[0, 4:5, :]) + mod_ref[0, 3:4, :]


def _attn_out_kernel(o_ref, w_ref, x_ref, mod_ref, g_ref, x_out_ref, h_out_ref):
    h2 = _mixer_out_core(o_ref[...], w_ref, x_ref, mod_ref, g_ref, x_out_ref)
    h_out_ref[...] = h2.astype(BF16)


def _attn_out(o, w_bf16, x2d, mod, g, s):
    t, d = x2d.shape
    tm = _token_tile(s)
    tpb = s // tm
    return pl.pallas_call(
        _attn_out_kernel,
        grid=(t // tm,),
        in_specs=[pl.BlockSpec((tm, d), lambda i: (i, 0)),
                  pl.BlockSpec(w_bf16.shape, lambda i: (0, 0)),
                  pl.BlockSpec((tm, d), lambda i: (i, 0)),
                  pl.BlockSpec((1, 6, d), lambda i: (i // tpb, 0, 0)),
                  pl.BlockSpec((4, d), lambda i: (0, 0))],
        out_specs=[pl.BlockSpec((tm, d), lambda i: (i, 0)), pl.BlockSpec((tm, d), lambda i: (i, 0))],
        out_shape=[jax.ShapeDtypeStruct((t, d), F32), jax.ShapeDtypeStruct((t, d), BF16)],
        compiler_params=_cparams("arbitrary"),
        name="attn_out_proj",
    )(o, w_bf16, x2d, mod, g)


def _ffn_kernel(h_ref, wg_ref, wu_ref, wd_ref, x_ref, mod_ref, g_ref, o_ref, acc_ref):
    j = pl.program_id(1)

    @pl.when(j == 0)
    def _():
        acc_ref[...] = jnp.zeros(acc_ref.shape, F32)

    h = h_ref[...]
    gate = jnp.dot(h, wg_ref[...], preferred_element_type=F32)
    up = jnp.dot(h, wu_ref[...], preferred_element_type=F32)
    act = (_silu(gate) * up).astype(BF16)
    acc_ref[...] += jnp.dot(act, wd_ref[...], preferred_element_type=F32)

    @pl.when(j == pl.num_programs(1) - 1)
    def _():
        o_ref[...] = x_ref[...] + mod_ref[0, 5:6, :] * _rms(acc_ref[...], g_ref[3:4, :])


def _ffn_chunk(f):
    for tf in (1408, 1024, 896, 512, 256, 128):
        if f % tf == 0:
            return tf
    return f


def _dense_ffn(h2, wg, wu, wd, x2d, mod, g, s):
    t, d = x2d.shape
    f = wg.shape[1]
    tm = _token_tile(s)
    tpb = s // tm
    tf = _ffn_chunk(f)
    return pl.pallas_call(
        _ffn_kernel,
        grid=(t // tm, f // tf),
        in_specs=[pl.BlockSpec((tm, d), lambda i, j: (i, 0)),
                  pl.BlockSpec((d, tf), lambda i, j: (0, j)),
                  pl.BlockSpec((d, tf), lambda i, j: (0, j)),
                  pl.BlockSpec((tf, d), lambda i, j: (j, 0)),
                  pl.BlockSpec((tm, d), lambda i, j: (i, 0)),
                  pl.BlockSpec((1, 6, d), lambda i, j: (i // tpb, 0, 0)),
                  pl.BlockSpec((4, d), lambda i, j: (0, 0))],
        out_specs=pl.BlockSpec((tm, d), lambda i, j: (i, 0)),
        out_shape=jax.ShapeDtypeStruct((t, d), F32),
        scratch_shapes=[pltpu.VMEM((tm, d), F32)],
        compiler_params=_cparams("arbitrary", "arbitrary"),
        name="dense_ffn",
    )(h2, wg, wu, wd, x2d, mod, g)


def _mlstm_qkv_kernel(xm_ref, cw_ref, cb_ref, wqk_ref, wv_ref, wgq_ref, wgk_ref, wgv_ref, bg_ref,
                      xc_ref, q_ref, k_ref, v_ref, gates_ref, pad_sc, *, ts, k_scale):
    c = pl.program_id(1)
    s, cw = xm_ref.shape[1], xm_ref.shape[2]
    front = SUBLANES
    pad_sc[0:front, :] = jnp.zeros((front, cw), F32)
    pad_sc[front:front + s, :] = xm_ref[0].astype(F32)

    @pl.when(c == 0)
    def _():
        gates_ref[0] = jnp.broadcast_to(bg_ref[...], gates_ref.shape[1:])

    for r in range(s // ts):
        r0 = r * ts
        conv = cb_ref[0]
        for j in range(MLSTM_CONV):
            start = r0 + front - (MLSTM_CONV - 1) + j
            conv = conv + pad_sc[start:start + ts, :] * cw_ref[0, j:j + 1, :]
        xc = _silu(conv)
        xcb = xc.astype(BF16)
        qk = jnp.dot(xcb, wqk_ref[0], preferred_element_type=F32)
        qb = qk[:, :cw].astype(BF16)
        kb = qk[:, cw:].astype(BF16)
        vb = jnp.dot(xm_ref[0, r0:r0 + ts, :], wv_ref[0], preferred_element_type=F32).astype(BF16)
        gates_ref[0, r0:r0 + ts, :] += (jnp.dot(qb, wgq_ref[...], preferred_element_type=F32)
                                        + jnp.dot(kb, wgk_ref[...], preferred_element_type=F32)
                                        + jnp.dot(vb, wgv_ref[...], preferred_element_type=F32))
        xc_ref[0, r0:r0 + ts, :] = xcb
        q_ref[0, r0:r0 + ts, :] = qb
        k_ref[0, r0:r0 + ts, :] = (qk[:, cw:] * k_scale).astype(BF16)
        v_ref[0, r0:r0 + ts, :] = vb


def _block_diag(w, cw):
    g, qb, _ = w.shape
    per = cw // qb
    wr = w.reshape(g // per, per, qb, qb)
    eye = jnp.eye(per, dtype=w.dtype)
    return jnp.einsum("cgio,gh->cgiho", wr, eye).reshape(g // per, cw, cw)


def _mlstm_qkv(xm, conv_w, conv_b, w_q, w_k, w_v, w_gate, b_gate, b, s):
    di = xm.shape[1]
    cw = 256
    nchunk = di // cw
    nh = MLSTM_HEADS
    dh = di // nh
    ts = min(256, s)
    wqk = jnp.concatenate([_block_diag(w_q, cw), _block_diag(w_k, cw)], axis=2).astype(BF16)
    wv = _block_diag(w_v, cw).astype(BF16)
    wg = jnp.zeros((3 * di, LANES), F32).at[:, :2 * nh].set(w_gate).astype(BF16)
    bg = jnp.zeros((1, LANES), F32).at[0, :2 * nh].set(b_gate)
    xm3 = xm.reshape(b, s, di)
    blk = pl.BlockSpec((1, s, cw), lambda bi, c: (bi, 0, c))
    outs = pl.pallas_call(
        functools.partial(_mlstm_qkv_kernel, ts=ts, k_scale=dh ** -0.5),
        grid=(b, nchunk),
        in_specs=[blk,
                  pl.BlockSpec((1, MLSTM_CONV, cw), lambda bi, c: (c, 0, 0)),
                  pl.BlockSpec((1, 1, cw), lambda bi, c: (c, 0, 0)),
                  pl.BlockSpec((1, cw, 2 * cw), lambda bi, c: (c, 0, 0)),
                  pl.BlockSpec((1, cw, cw), lambda bi, c: (c, 0, 0)),
                  pl.BlockSpec((cw, LANES), lambda bi, c: (c, 0)),
                  pl.BlockSpec((cw, LANES), lambda bi, c: (nchunk + c, 0)),
                  pl.BlockSpec((cw, LANES), lambda bi, c: (2 * nchunk + c, 0)),
                  pl.BlockSpec((1, LANES), lambda bi, c: (0, 0))],
        out_specs=[blk, blk, blk, blk, pl.BlockSpec((1, s, LANES), lambda bi, c: (bi, 0, 0))],
        out_shape=[jax.ShapeDtypeStruct((b, s, di), BF16)] * 4 + [jax.ShapeDtypeStruct((b, s, LANES), F32)],
        scratch_shapes=[pltpu.VMEM((s + SUBLANES, cw), F32)],
        compiler_params=_cparams("arbitrary", "arbitrary"),
        name="mlstm_qkv",
    )(xm3, conv_w.reshape(MLSTM_CONV, nchunk, cw).transpose(1, 0, 2), conv_b.reshape(nchunk, 1, cw),
      wqk, wv, wg, wg, wg, bg)
    return outs


def _log_sigmoid(x):
    return jnp.minimum(x, 0.0) - jnp.log1p(jnp.exp(-jnp.abs(x)))


def _mlstm_cell_kernel(q_ref, k_ref, v_ref, gc_ref, gr_ref, ng_ref, o_ref, ct_sc, n_sc, m_sc, *, nh):
    h = pl.program_id(1)
    c = pl.program_id(2)
    L = q_ref.shape[1]

    @pl.when(c == 0)
    def _():
        ct_sc[...] = jnp.zeros(ct_sc.shape, F32)
        n_sc[...] = jnp.zeros(n_sc.shape, F32)
        m_sc[...] = jnp.zeros(m_sc.shape, F32)

    gc = gc_ref[0]
    lane = lax.broadcasted_iota(jnp.int32, gc.shape, 1)
    i_col = jnp.sum(jnp.where(lane == h, gc, 0.0), axis=1, keepdims=True)
    f_col = jnp.sum(jnp.where(lane == h + nh, gc, 0.0), axis=1, keepdims=True)
    gr = gr_ref[0]
    sub = lax.broadcasted_iota(jnp.int32, gr.shape, 0)
    i_row = jnp.sum(jnp.where(sub == h, gr, 0.0), axis=0, keepdims=True)
    f_row = jnp.sum(jnp.where(sub == h + nh, gr, 0.0), axis=0, keepdims=True)

    t_idx = lax.broadcasted_iota(jnp.int32, (L, L), 0)
    s_idx = lax.broadcasted_iota(jnp.int32, (L, L), 1)
    causal = s_idx <= t_idx
    tri = causal.astype(F32)
    tri_t = (t_idx <= s_idx).astype(F32)
    b_col = jnp.dot(tri, jnp.broadcast_to(_log_sigmoid(f_col), (L, LANES)),
                    preferred_element_type=F32, precision=HIGHEST)[:, 0:1]
    b_row = jnp.dot(jnp.broadcast_to(_log_sigmoid(f_row), (SUBLANES, L)), tri_t,
                    preferred_element_type=F32, precision=HIGHEST)[0:1, :]

    m_prev = m_sc[0:1, 0:1]
    dlog = jnp.where(causal, b_col - b_row + i_row, NEG_BIG)
    g = b_col + m_prev
    m_t = jnp.maximum(g, jnp.max(dlog, axis=1, keepdims=True))
    qb = q_ref[0]
    kb = k_ref[0]
    vb = v_ref[0]
    qk = lax.dot_general(qb, kb, (((1,), (1,)), ((), ())), preferred_element_type=F32)
    w = jnp.exp(dlog - m_t) * qk
    inter = jnp.exp(g - m_t)
    ct = ct_sc[...]
    num = (jnp.dot(w.astype(BF16), vb, preferred_element_type=F32)
           + inter * jnp.dot(qb, ct.astype(BF16), preferred_element_type=F32))
    den = (jnp.sum(w, axis=1, keepdims=True)
           + inter * jnp.sum(qb.astype(F32) * n_sc[...], axis=1, keepdims=True))
    hh = num / jnp.maximum(jnp.abs(den), jnp.exp(-m_t))

    mu = jnp.mean(hh, axis=1, keepdims=True)
    cen = hh - mu
    var = jnp.mean(cen * cen, axis=1, keepdims=True)
    o_ref[0] = (cen * lax.rsqrt(var + NORM_EPS) * ng_ref[0]).astype(BF16)

    b_last = b_row[:, L - 1:L]
    w_state_row = b_last - b_row + i_row
    m_new = jnp.maximum(b_last + m_prev, jnp.max(w_state_row, axis=1, keepdims=True))
    decay = jnp.exp(b_last + m_prev - m_new)
    ws_row = jnp.exp(w_state_row - m_new)
    ws_col = jnp.exp(b_last - b_col + i_col - m_new)
    vs = (vb.astype(F32) * ws_col).astype(BF16)
    ct_sc[...] = decay * ct + lax.dot_general(kb, vs, (((0,), (0,)), ((), ())), preferred_element_type=F32)
    n_upd = jnp.dot(jnp.broadcast_to(ws_row, (SUBLANES, L)).astype(BF16), kb, preferred_element_type=F32)
    n_sc[...] = decay * n_sc[...] + n_upd[0:1, :]
    m_sc[...] = jnp.broadcast_to(m_new, m_sc.shape)


def _mlstm_cell(q, k, v, gates, norm_g, b, s):
    di = q.shape[2]
    nh = MLSTM_HEADS
    dh = di // nh
    L = min(256, s)
    gates_t = jnp.transpose(gates[:, :, :SUBLANES], (0, 2, 1))
    blk = pl.BlockSpec((1, L, dh), lambda bi, h, c: (bi, c, h))
    return pl.pallas_call(
        functools.partial(_mlstm_cell_kernel, nh=nh),
        grid=(b, nh, s // L),
        in_specs=[blk, blk, blk,
                  pl.BlockSpec((1, L, LANES), lambda bi, h, c: (bi, c, 0)),
                  pl.BlockSpec((1, SUBLANES, L), lambda bi, h, c: (bi, 0, c)),
                  pl.BlockSpec((1, 1, dh), lambda bi, h, c: (h, 0, 0))],
        out_specs=blk,
        out_shape=jax.ShapeDtypeStruct((b, s, di), BF16),
        scratch_shapes=[pltpu.VMEM((dh, dh), F32), pltpu.VMEM((1, dh), F32), pltpu.VMEM((1, LANES), F32)],
        compiler_params=_cparams("arbitrary", "arbitrary", "arbitrary"),
        name="mlstm_cell",
    )(q, k, v, gates, gates_t, norm_g.reshape(nh, 1, dh))


def _mlstm_out_kernel(hn_ref, xc_ref, z_ref, skip_ref, w_ref, x_ref, mod_ref, g_ref, wr_ref, br_ref,
                      x_out_ref, h_out_ref, route_ref, counts_ref, carry_sc, *, n_experts):
    i = pl.program_id(0)

    @pl.when(i == 0)
    def _():
        carry_sc[...] = jnp.zeros(carry_sc.shape, F32)

    inner = ((hn_ref[...].astype(F32) + skip_ref[...] * xc_ref[...].astype(F32))
             * _silu(z_ref[...].astype(F32)))
    h4 = _mixer_out_core(inner.astype(BF16), w_ref, x_ref, mod_ref, g_ref, x_out_ref)
    h_out_ref[...] = h4

    tm = h4.shape[0]
    lane = lax.broadcasted_iota(jnp.int32, (tm, LANES), 1)
    logits = jnp.dot(h4, wr_ref[...], preferred_element_type=F32, precision=HIGHEST) + br_ref[...]
    logits = jnp.where(lane < n_experts, logits, NEG_BIG)
    ex = jnp.exp(logits - jnp.max(logits, axis=1, keepdims=True))
    probs = ex / jnp.sum(ex, axis=1, keepdims=True)
    probs = jnp.where(lane < n_experts, probs, -1.0)
    lane_f = lane.astype(F32)
    p0 = jnp.max(probs, axis=1, keepdims=True)
    e0 = jnp.min(jnp.where(probs == p0, lane_f, float(LANES)), axis=1, keepdims=True)
    rest = jnp.where(lane_f == e0, -1.0, probs)
    p1 = jnp.max(rest, axis=1, keepdims=True)
    e1 = jnp.min(jnp.where(rest == p1, lane_f, float(LANES)), axis=1, keepdims=True)
    tot = p0 + p1
    sel0 = lane_f == e0
    sel1 = lane_f == e1
    sel = jnp.where(sel0 | sel1, 1.0, 0.0)
    r_idx = lax.broadcasted_iota(jnp.int32, (tm, tm), 0)
    c_idx = lax.broadcasted_iota(jnp.int32, (tm, tm), 1)
    before = (c_idx < r_idx).astype(BF16)
    cum = jnp.dot(before, sel.astype(BF16), preferred_element_type=F32) + carry_sc[0:1, :]
    rank0 = jnp.sum(jnp.where(sel0, cum, 0.0), axis=1, keepdims=True)
    rank1 = jnp.sum(jnp.where(sel1, cum, 0.0), axis=1, keepdims=True)
    carry = carry_sc[0:1, :] + jnp.sum(sel, axis=0, keepdims=True)
    carry_sc[...] = jnp.broadcast_to(carry, carry_sc.shape)
    counts_ref[...] = jnp.broadcast_to(carry, counts_ref.shape)
    vals = (p0 / tot, p1 / tot, e0, e1, rank0, rank1)
    route = jnp.zeros((tm, LANES), F32)
    for idx, val in enumerate(vals):
        route = jnp.where(lane == idx, val, route)
    route_ref[...] = route


def _mlstm_out(hn, xc, z, skip, w_bf16, x2d, mod, g, w_router, b_router, s):
    t, d = x2d.shape
    di = hn.shape[1]
    e = w_router.shape[1]
    tm = _token_tile(s)
    tpb = s // tm
    wr = jnp.zeros((d, LANES), F32).at[:, :e].set(w_router)
    br = jnp.zeros((1, LANES), F32).at[0, :e].set(b_router)
    tok = lambda n: pl.BlockSpec((tm, n), lambda i: (i, 0))
    return pl.pallas_call(
        functools.partial(_mlstm_out_kernel, n_experts=e),
        grid=(t // tm,),
        in_specs=[tok(di), tok(di), tok(di),
                  pl.BlockSpec((1, di), lambda i: (0, 0)),
                  pl.BlockSpec(w_bf16.shape, lambda i: (0, 0)),
                  tok(d),
                  pl.BlockSpec((1, 6, d), lambda i: (i // tpb, 0, 0)),
                  pl.BlockSpec((4, d), lambda i: (0, 0)),
                  pl.BlockSpec((d, LANES), lambda i: (0, 0)),
                  pl.BlockSpec((1, LANES), lambda i: (0, 0))],
        out_specs=[tok(d), tok(d), tok(LANES), pl.BlockSpec((SUBLANES, LANES), lambda i: (0, 0))],
        out_shape=[jax.ShapeDtypeStruct((t, d), F32), jax.ShapeDtypeStruct((t, d), F32),
                   jax.ShapeDtypeStruct((t, LANES), F32), jax.ShapeDtypeStruct((SUBLANES, LANES), F32)],
        scratch_shapes=[pltpu.VMEM((SUBLANES, LANES), F32)],
        compiler_params=_cparams("arbitrary"),
        name="mlstm_out_router",
    )(hn, xc, z, skip.reshape(1, di), w_bf16, x2d, mod, g, wr, br)


def _dispatch_kernel(pos_ref, h_ref, xs_in_ref, xs_ref, sem):
    del xs_in_ref
    tm = h_ref.shape[0]

    def row_copy(r, slot):
        dst = pos_ref[0, slot, r]
        return pltpu.make_async_copy(h_ref.at[pl.ds(r, 1)], xs_ref.at[pl.ds(dst, 1)], sem)

    def start(r, carry):
        row_copy(r, 0).start()
        row_copy(r, 1).start()
        return carry

    def wait(r, carry):
        row_copy(r, 0).wait()
        row_copy(r, 1).wait()
        return carry

    lax.fori_loop(0, tm, start, 0)
    lax.fori_loop(0, tm, wait, 0)


def _dispatch(h4, pos, n_rows, s):
    t, d = h4.shape
    tm = _token_tile(s)
    xs0 = jnp.zeros((n_rows, d), F32)
    return pl.pallas_call(
        _dispatch_kernel,
        grid=(t // tm,),
        in_specs=[pl.BlockSpec((1, TOP_K, tm), lambda i: (i, 0, 0), memory_space=pltpu.SMEM),
                  pl.BlockSpec((tm, d), lambda i: (i, 0)),
                  pl.BlockSpec(memory_space=pl.ANY)],
        out_specs=pl.BlockSpec(memory_space=pl.ANY),
        out_shape=jax.ShapeDtypeStruct((n_rows, d), F32),
        scratch_shapes=[pltpu.SemaphoreType.DMA(())],
        input_output_aliases={2: 0},
        compiler_params=_cparams("arbitrary"),
        name="moe_dispatch",
    )(pos, h4, xs0)


def _expert_ffn_kernel(te_ref, nreal_ref, xs_ref, wg_ref, wu_ref, wd_ref, ys_ref, xb_sc, acc_sc):
    i = pl.program_id(0)
    j = pl.program_id(1)
    last = pl.num_programs(1) - 1
    real = i < nreal_ref[0]

    @pl.when(real & (j == 0))
    def _():
        xb_sc[...] = xs_ref[...].astype(BF16)
        acc_sc[...] = jnp.zeros(acc_sc.shape, F32)

    @pl.when(real)
    def _():
        xb = xb_sc[...]
        gate = jnp.dot(xb, wg_ref[0], preferred_element_type=F32)
        up = jnp.dot(xb, wu_ref[0], preferred_element_type=F32)
        act = (_silu(gate) * up).astype(BF16)
        acc_sc[...] += jnp.dot(act, wd_ref[0], preferred_element_type=F32)

    @pl.when(real & (j == last))
    def _():
        ys_ref[...] = acc_sc[...]

    @pl.when(jnp.logical_not(real) & (j == last))
    def _():
        ys_ref[...] = jnp.zeros(ys_ref.shape, F32)


def _expert_ffn(xs, tile_expert, n_real, wg, wu, wd, tm):
    p, d = xs.shape
    f = wg.shape[2]
    tf = _ffn_chunk(f)
    nj = f // tf
    n_tiles = p // tm

    def row_idx(i, j, te, nr):
        return (jnp.minimum(i, nr[0] - 1), 0)

    def col_j(i, j, nr):
        return jnp.where(i < nr[0], j, nj - 1)

    grid_spec = pltpu.PrefetchScalarGridSpec(
        num_scalar_prefetch=2,
        grid=(n_tiles, nj),
        in_specs=[pl.BlockSpec((tm, d), row_idx),
                  pl.BlockSpec((1, d, tf), lambda i, j, te, nr: (te[i], 0, col_j(i, j, nr))),
                  pl.BlockSpec((1, d, tf), lambda i, j, te, nr: (te[i], 0, col_j(i, j, nr))),
                  pl.BlockSpec((1, tf, d), lambda i, j, te, nr: (te[i], col_j(i, j, nr), 0))],
        out_specs=pl.BlockSpec((tm, d), lambda i, j, te, nr: (i, 0)),
        scratch_shapes=[pltpu.VMEM((tm, d), BF16), pltpu.VMEM((tm, d), F32)],
    )
    return pl.pallas_call(
        _expert_ffn_kernel,
        grid_spec=grid_spec,
        out_shape=jax.ShapeDtypeStruct((p, d), F32),
        compiler_params=_cparams("arbitrary", "arbitrary"),
        name="moe_expert_ffn",
    )(tile_expert, n_real, xs, wg, wu, wd)


def _combine_kernel(pos_ref, ys_ref, route_ref, x_ref, mod_ref, g_ref, o_ref, a_sc, b_sc, sem):
    tm = x_ref.shape[0]

    def row_copy(r, slot, buf):
        src = pos_ref[0, slot, r]
        return pltpu.make_async_copy(ys_ref.at[pl.ds(src, 1)], buf.at[pl.ds(r, 1)], sem)

    def start(r, carry):
        row_copy(r, 0, a_sc).start()
        row_copy(r, 1, b_sc).start()
        return carry

    def wait(r, carry):
        row_copy(r, 0, a_sc).wait()
        row_copy(r, 1, b_sc).wait()
        return carry

    lax.fori_loop(0, tm, start, 0)
    lax.fori_loop(0, tm, wait, 0)
    route = route_ref[...]
    y = route[:, 0:1] * a_sc[...] + route[:, 1:2] * b_sc[...]
    o_ref[...] = x_ref[...] + mod_ref[0, 5:6, :] * _rms(y, g_ref[3:4, :])


def _combine(ys, pos, route, x2d, mod, g, s):
    t, d = x2d.shape
    tm = _token_tile(s)
    tpb = s // tm
    return pl.pallas_call(
        _combine_kernel,
        grid=(t // tm,),
        in_specs=[pl.BlockSpec((1, TOP_K, tm), lambda i: (i, 0, 0), memory_space=pltpu.SMEM),
                  pl.BlockSpec(memory_space=pl.ANY),
                  pl.BlockSpec((tm, LANES), lambda i: (i, 0)),
                  pl.BlockSpec((tm, d), lambda i: (i, 0)),
                  pl.BlockSpec((1, 6, d), lambda i: (i // tpb, 0, 0)),
                  pl.BlockSpec((4, d), lambda i: (0, 0))],
        out_specs=pl.BlockSpec((tm, d), lambda i: (i, 0)),
        out_shape=jax.ShapeDtypeStruct((t, d), F32),
        scratch_shapes=[pltpu.VMEM((tm, d), F32), pltpu.VMEM((tm, d), F32), pltpu.SemaphoreType.DMA(())],
        compiler_params=_cparams("arbitrary"),
        name="moe_combine",
    )(pos, ys, route, x2d, mod, g)


def _moe(h4, route, counts, x2d, mod, g, wg, wu, wd, s):
    t, d = h4.shape
    e = wg.shape[0]
    tm_tok = _token_tile(s)
    tm = min(512, t)
    n_tiles = (TOP_K * t) // tm + e
    cnt = counts[0, :e].astype(jnp.int32)
    tiles_e = (cnt + tm - 1) // tm
    ends = jnp.cumsum(tiles_e)
    off = (ends - tiles_e) * tm
    e0 = route[:, 2].astype(jnp.int32)
    e1 = route[:, 3].astype(jnp.int32)
    experts = jnp.arange(e, dtype=jnp.int32)[None, :]
    pos0 = jnp.sum(jnp.where(e0[:, None] == experts, off[None, :], 0), axis=1) + route[:, 4].astype(jnp.int32)
    pos1 = jnp.sum(jnp.where(e1[:, None] == experts, off[None, :], 0), axis=1) + route[:, 5].astype(jnp.int32)
    pos = jnp.stack([pos0.reshape(t // tm_tok, tm_tok), pos1.reshape(t // tm_tok, tm_tok)], axis=1)
    n_real = ends[e - 1:e]
    tile_ids = jnp.arange(n_tiles, dtype=jnp.int32)
    tile_expert = jnp.sum(tile_ids[:, None] >= ends[None, :], axis=1).astype(jnp.int32)
    last_expert = jnp.sum(n_real[0] - 1 >= ends).astype(jnp.int32)
    tile_expert = jnp.where(tile_ids < n_real[0], tile_expert, last_expert)
    xs = _dispatch(h4, pos, n_tiles * tm, s)
    ys = _expert_ffn(xs, tile_expert, n_real.astype(jnp.int32), wg, wu, wd, tm)
    return _combine(ys, pos, route, x2d, mod, g, s)


def kernel(x, c, mod_w, mod_b, norm_g, attn_w_in, attn_w_out, attn_lambda, attn_subln, ffn_w_gate, ffn_w_up, ffn_w_down, mlstm_w_in, mlstm_conv_w, mlstm_conv_b, mlstm_w_q, mlstm_w_k, mlstm_w_v, mlstm_w_gate, mlstm_b_gate, mlstm_skip, mlstm_norm, mlstm_w_out, moe_w_router, moe_b_router, moe_w_gate, moe_w_up, moe_w_down):
    b, s, d = x.shape
    depth = mod_w.shape[0]
    mod_all = _modulation(c, mod_w, mod_b)
    x2d = x.reshape(b * s, d)
    da = d // (2 * ATTN_HEADS)
    for i in range(depth):
        mod = mod_all[i]
        g = norm_g[i]
        j = i // N_MIXERS
        if i % N_MIXERS == 0:
            lambda_init = 0.8 - 0.6 * math.exp(-0.3 * i)
            q, k, v = _in_proj(x2d, mod, g, attn_w_in[j].astype(BF16), 3, (da ** -0.5, 1.0, 1.0), s,
                               "attn_in_proj")
            o = _diff_attention(q, k, v, attn_lambda[j], attn_subln[j], lambda_init, b, s)
            x2d, h2 = _attn_out(o, attn_w_out[j].astype(BF16), x2d, mod, g, s)
            x2d = _dense_ffn(h2, ffn_w_gate[j].astype(BF16), ffn_w_up[j].astype(BF16),
                             ffn_w_down[j].astype(BF16), x2d, mod, g, s)
        else:
            xm, z = _in_proj(x2d, mod, g, mlstm_w_in[j].astype(BF16), 2, (1.0, 1.0), s, "mlstm_in_proj")
            xc, q, k, v, gates = _mlstm_qkv(xm, mlstm_conv_w[j], mlstm_conv_b[j], mlstm_w_q[j], mlstm_w_k[j],
                                            mlstm_w_v[j], mlstm_w_gate[j], mlstm_b_gate[j], b, s)
            hn = _mlstm_cell(q, k, v, gates, mlstm_norm[j], b, s)
            di = hn.shape[2]
            x2d, h4, route, counts = _mlstm_out(hn.reshape(b * s, di), xc.reshape(b * s, di), z, mlstm_skip[j],
                                                mlstm_w_out[j].astype(BF16), x2d, mod, g,
                                                moe_w_router[j], moe_b_router[j], s)
            x2d = _moe(h4, route, counts, x2d, mod, g, moe_w_gate[j].astype(BF16), moe_w_up[j].astype(BF16),
                       moe_w_down[j].astype(BF16), s)
    return x2d.reshape(b, s, d)
```

```python
import functools
import math

import jax
import jax.numpy as jnp
from jax import lax
from jax.experimental import pallas as pl
from jax.experimental.pallas import tpu as pltpu

F32 = jnp.float32
BF16 = jnp.bfloat16
HIGHEST = lax.Precision.HIGHEST

CHUNK = 64
ATTN_HEADS = 8
ALIBI_MAX_BIAS = 8.0
MLSTM_HEADS = 4
MLSTM_CONV = 4
QKV_BLOCK = 4
TOP_K = 2
NORM_EPS = 1e-6
N_MIXERS = 2

LANES = 128
SUBLANES = 8
VMEM_LIMIT_BYTES = 56 * 1024 * 1024
NEG_BIG = -1e30
LOG2E = math.log2(math.e)


def _cparams(*sem):
    return pltpu.CompilerParams(dimension_semantics=sem, vmem_limit_bytes=VMEM_LIMIT_BYTES)


def _rms(x, g):
    return x * lax.rsqrt(jnp.mean(x * x, axis=-1, keepdims=True) + NORM_EPS) * g


def _silu(x):
    return x * jax.nn.sigmoid(x)


def _token_tile(s):
    return min(512, s)


def _mod_kernel(c_ref, w_ref, b_ref, o_ref):
    cond = _silu(c_ref[...])
    o_ref[0] = jnp.dot(cond, w_ref[0], preferred_element_type=F32, precision=HIGHEST) + b_ref[0]


def _modulation(c, mod_w, mod_b):
    depth, d, n = mod_w.shape
    b = c.shape[0]
    tn = 1536 if n % 1536 == 0 else n
    out = pl.pallas_call(
        _mod_kernel,
        grid=(depth, n // tn),
        in_specs=[pl.BlockSpec((b, d), lambda i, j: (0, 0)),
                  pl.BlockSpec((1, d, tn), lambda i, j: (i, 0, j)),
                  pl.BlockSpec((1, 1, tn), lambda i, j: (i, 0, j))],
        out_specs=pl.BlockSpec((1, b, tn), lambda i, j: (i, 0, j)),
        out_shape=jax.ShapeDtypeStruct((depth, b, n), F32),
        compiler_params=_cparams("arbitrary", "arbitrary"),
        name="modulation",
    )(c, mod_w, mod_b.reshape(depth, 1, n))
    return out.reshape(depth, b, 6, d)


def _in_proj_kernel(x_ref, mod_ref, g_ref, w_ref, *o_refs, scales):
    x = x_ref[...]
    h = _rms(x, g_ref[0:1, :]) * (1.0 + mod_ref[0, 1:2, :]) + mod_ref[0, 0:1, :]
    hb = h.astype(BF16)
    n = o_refs[0].shape[1]
    for idx, o_ref in enumerate(o_refs):
        r = jnp.dot(hb, w_ref[:, idx * n:(idx + 1) * n], preferred_element_type=F32)
        if scales[idx] != 1.0:
            r = r * scales[idx]
        o_ref[...] = r.astype(BF16)


def _in_proj(x2d, mod, g, w_bf16, n_out, scales, s, name):
    t, d = x2d.shape
    tm = _token_tile(s)
    tpb = s // tm
    n = w_bf16.shape[1] // n_out
    return pl.pallas_call(
        functools.partial(_in_proj_kernel, scales=scales),
        grid=(t // tm,),
        in_specs=[pl.BlockSpec((tm, d), lambda i: (i, 0)),
                  pl.BlockSpec((1, 6, d), lambda i: (i // tpb, 0, 0)),
                  pl.BlockSpec((4, d), lambda i: (0, 0)),
                  pl.BlockSpec(w_bf16.shape, lambda i: (0, 0))],
        out_specs=[pl.BlockSpec((tm, n), lambda i: (i, 0))] * n_out,
        out_shape=[jax.ShapeDtypeStruct((t, n), BF16)] * n_out,
        compiler_params=_cparams("arbitrary"),
        name=name,
    )(x2d, mod, g, w_bf16)


def _attn_kernel(slopes_ref, lam_ref, subln_ref, q_ref, k_ref, v_ref, o_ref, m_sc, acc_sc, bias_sc, vext_sc,
                 s_sc, smax_sc, *, tq, tk, lambda_init):
    h = pl.program_id(1)
    qi = pl.program_id(2)
    slope = slopes_ref[h]
    hd = q_ref.shape[2]
    half = hd // 2

    @pl.when(qi == 0)
    def _():
        vext_sc[:, :hd] = v_ref[0]
        vext_sc[:, hd:] = jnp.ones((vext_sc.shape[0], hd), BF16)
        row = lax.broadcasted_iota(jnp.int32, (tq, tk), 0)
        col = lax.broadcasted_iota(jnp.int32, (tq, tk), 1)
        dist = (row - col).astype(F32)
        bias_sc[0] = -slope * dist
        bias_sc[1] = jnp.where(col // CHUNK <= row // CHUNK, -slope * jnp.abs(dist), NEG_BIG)

    q = q_ref[0]
    lane = lax.broadcasted_iota(jnp.int32, q.shape, 1)
    zero = jnp.zeros_like(q)
    q2 = jnp.concatenate([jnp.where(lane < half, q, zero), jnp.where(lane >= half, q, zero)], axis=0)

    q_base = qi * tq
    n_full = qi

    m_sc[...] = jnp.full(m_sc.shape, NEG_BIG, F32)
    acc_sc[...] = jnp.zeros(acc_sc.shape, F32)

    def scores(j, slot):
        k_base = pl.multiple_of(j * tk, tk)
        kj = k_ref[0, pl.ds(k_base, tk), :]
        s = lax.dot_general(q2, kj, (((1,), (1,)), ((), ())), preferred_element_type=F32)
        bias = bias_sc[(j == n_full).astype(jnp.int32)]
        s = s + jnp.concatenate([bias, bias], axis=0)
        s_sc[slot] = s
        smax_sc[slot] = jnp.broadcast_to(jnp.max(s, axis=1, keepdims=True), smax_sc.shape[1:])

    def accumulate(j, slot):
        k_base = pl.multiple_of(j * tk, tk)
        const = jnp.where(j == n_full, 0.0, -slope * (q_base - k_base).astype(F32))
        m_prev = m_sc[...]
        m_next = jnp.maximum(m_prev, smax_sc[slot] + const)
        alpha = jnp.exp2(m_prev - m_next)
        p = jnp.exp2(s_sc[slot] - jnp.tile(m_next - const, (1, tk // LANES)))
        pv = jnp.dot(p.astype(BF16), vext_sc[pl.ds(k_base, tk), :], preferred_element_type=F32)
        acc_sc[...] = jnp.tile(alpha, (1, 2)) * acc_sc[...] + pv
        m_sc[...] = m_next

    def pair(i, carry):
        j = 2 * i
        scores(j + 1, 1)
        accumulate(j, 0)
        scores(j + 2, 0)
        accumulate(j + 1, 1)
        return carry

    scores(0, 0)
    n_pairs = n_full // 2
    lax.fori_loop(0, n_pairs, pair, 0)

    @pl.when(n_full % 2 == 1)
    def _():
        scores(n_full, 1)
        accumulate(n_full - 1, 0)
        accumulate(n_full, 1)

    @pl.when(n_full % 2 == 0)
    def _():
        accumulate(n_full, 0)

    lam = lam_ref[...]
    lam_full = (jnp.exp(jnp.sum(lam[0:1, :] * lam[1:2, :], axis=1, keepdims=True))
                - jnp.exp(jnp.sum(lam[2:3, :] * lam[3:4, :], axis=1, keepdims=True)) + lambda_init)
    acc = acc_sc[...]
    o_all = acc[:, :hd] / acc[:, hd:]
    o = o_all[:tq] - lam_full * o_all[tq:]
    o = _rms(o, subln_ref[...]) * (1.0 - lambda_init)
    o_ref[0] = o.astype(BF16)


def _diff_attention(q, k, v, lam, subln, lambda_init, b, s):
    d = q.shape[1]
    hd = d // ATTN_HEADS
    tq = tk = min(512, s)
    slopes = LOG2E * jnp.exp2(-ALIBI_MAX_BIAS * jnp.arange(1, ATTN_HEADS + 1, dtype=F32) / ATTN_HEADS)
    q3, k3, v3 = (a.reshape(b, s, d) for a in (q, k, v))
    grid_spec = pltpu.PrefetchScalarGridSpec(
        num_scalar_prefetch=1,
        grid=(b, ATTN_HEADS, s // tq),
        in_specs=[pl.BlockSpec(lam.shape, lambda bi, h, qi, sl: (0, 0)),
                  pl.BlockSpec((1, hd), lambda bi, h, qi, sl: (0, 0)),
                  pl.BlockSpec((1, tq, hd), lambda bi, h, qi, sl: (bi, qi, h)),
                  pl.BlockSpec((1, s, hd), lambda bi, h, qi, sl: (bi, 0, h)),
                  pl.BlockSpec((1, s, hd), lambda bi, h, qi, sl: (bi, 0, h))],
        out_specs=pl.BlockSpec((1, tq, hd), lambda bi, h, qi, sl: (bi, qi, h)),
        scratch_shapes=[pltpu.VMEM((2 * tq, LANES), F32), pltpu.VMEM((2 * tq, 2 * hd), F32),
                        pltpu.VMEM((2, tq, tk), F32), pltpu.VMEM((s, 2 * hd), BF16),
                        pltpu.VMEM((2, 2 * tq, tk), F32), pltpu.VMEM((2, 2 * tq, LANES), F32)],
    )
    o = pl.pallas_call(
        functools.partial(_attn_kernel, tq=tq, tk=tk, lambda_init=lambda_init),
        grid_spec=grid_spec,
        out_shape=jax.ShapeDtypeStruct((b, s, d), BF16),
        compiler_params=_cparams("arbitrary", "arbitrary", "arbitrary"),
        name="diff_attention",
    )(slopes, lam, subln.reshape(1, hd), q3, k3, v3)
    return o.reshape(b * s, d)


def _mixer_out_core(inp_bf16, w_ref, x_ref, mod_ref, g_ref, x_out_ref):
    y = jnp.dot(inp_bf16, w_ref[...], preferred_element_type=F32)
    x1 = x_ref[...] + mod_ref[0, 2:3, :] * _rms(y, g_ref[1:2, :])
    x_out_ref[...] = x1
    return _rms(x1, g_ref[2:3, :]) * (1.0 + mod_ref[0, 4:5, :]) + mod_ref[0, 3:4, :]


def _attn_out_kernel(o_ref, w_ref, x_ref, mod_ref, g_ref, x_out_ref, h_out_ref):
    h2 = _mixer_out_core(o_ref[...], w_ref, x_ref, mod_ref, g_ref, x_out_ref)
    h_out_ref[...] = h2.astype(BF16)


def _attn_out(o, w_bf16, x2d, mod, g, s):
    t, d = x2d.shape
    tm = _token_tile(s)
    tpb = s // tm
    return pl.pallas_call(
        _attn_out_kernel,
        grid=(t // tm,),
        in_specs=[pl.BlockSpec((tm, d), lambda i: (i, 0)),
                  pl.BlockSpec(w_bf16.shape, lambda i: (0, 0)),
                  pl.BlockSpec((tm, d), lambda i: (i, 0)),
                  pl.BlockSpec((1, 6, d), lambda i: (i // tpb, 0, 0)),
                  pl.BlockSpec((4, d), lambda i: (0, 0))],
        out_specs=[pl.BlockSpec((tm, d), lambda i: (i, 0)), pl.BlockSpec((tm, d), lambda i: (i, 0))],
        out_shape=[jax.ShapeDtypeStruct((t, d), F32), jax.ShapeDtypeStruct((t, d), BF16)],
        compiler_params=_cparams("arbitrary"),
        name="attn_out_proj",
    )(o, w_bf16, x2d, mod, g)


def _ffn_kernel(h_ref, wg_ref, wu_ref, wd_ref, x_ref, mod_ref, g_ref, o_ref, acc_ref):
    j = pl.program_id(1)

    @pl.when(j == 0)
    def _():
        acc_ref[...] = jnp.zeros(acc_ref.shape, F32)

    h = h_ref[...]
    gate = jnp.dot(h, wg_ref[...], preferred_element_type=F32)
    up = jnp.dot(h, wu_ref[...], preferred_element_type=F32)
    act = (_silu(gate) * up).astype(BF16)
    acc_ref[...] += jnp.dot(act, wd_ref[...], preferred_element_type=F32)

    @pl.when(j == pl.num_programs(1) - 1)
    def _():
        o_ref[...] = x_ref[...] + mod_ref[0, 5:6, :] * _rms(acc_ref[...], g_ref[3:4, :])


def _ffn_chunk(f):
    for tf in (1408, 1024, 896, 512, 256, 128):
        if f % tf == 0:
            return tf
    return f


def _dense_ffn(h2, wg, wu, wd, x2d, mod, g, s):
    t, d = x2d.shape
    f = wg.shape[1]
    tm = _token_tile(s)
    tpb = s // tm
    tf = _ffn_chunk(f)
    return pl.pallas_call(
        _ffn_kernel,
        grid=(t // tm, f // tf),
        in_specs=[pl.BlockSpec((tm, d), lambda i, j: (i, 0)),
                  pl.BlockSpec((d, tf), lambda i, j: (0, j)),
                  pl.BlockSpec((d, tf), lambda i, j: (0, j)),
                  pl.BlockSpec((tf, d), lambda i, j: (j, 0)),
                  pl.BlockSpec((tm, d), lambda i, j: (i, 0)),
                  pl.BlockSpec((1, 6, d), lambda i, j: (i // tpb, 0, 0)),
                  pl.BlockSpec((4, d), lambda i, j: (0, 0))],
        out_specs=pl.BlockSpec((tm, d), lambda i, j: (i, 0)),
        out_shape=jax.ShapeDtypeStruct((t, d), F32),
        scratch_shapes=[pltpu.VMEM((tm, d), F32)],
        compiler_params=_cparams("arbitrary", "arbitrary"),
        name="dense_ffn",
    )(h2, wg, wu, wd, x2d, mod, g)


def _mlstm_qkv_kernel(xm_ref, cw_ref, cb_ref, wqk_ref, wkt_ref, wv_ref, wgq_ref, wgk_ref, wgv_ref, bg_ref,
                      xc_ref, q_ref, kt_ref, v_ref, gates_ref, pad_sc, *, ts, k_scale):
    c = pl.program_id(1)
    s, cw = xm_ref.shape[1], xm_ref.shape[2]
    front = SUBLANES
    pad_sc[0:front, :] = jnp.zeros((front, cw), F32)
    pad_sc[front:front + s, :] = xm_ref[0].astype(F32)

    @pl.when(c == 0)
    def _():
        gates_ref[0] = jnp.broadcast_to(bg_ref[...], gates_ref.shape[1:])

    for r in range(s // ts):
        r0 = r * ts
        conv = cb_ref[0]
        for j in range(MLSTM_CONV):
            start = r0 + front - (MLSTM_CONV - 1) + j
            conv = conv + pad_sc[start:start + ts, :] * cw_ref[0, j:j + 1, :]
        xc = _silu(conv)
        xcb = xc.astype(BF16)
        qk = jnp.dot(xcb, wqk_ref[0], preferred_element_type=F32)
        qb = qk[:, :cw].astype(BF16)
        kb = qk[:, cw:].astype(BF16)
        vb = jnp.dot(xm_ref[0, r0:r0 + ts, :], wv_ref[0], preferred_element_type=F32).astype(BF16)
        gates_ref[0, r0:r0 + ts, :] += (jnp.dot(qb, wgq_ref[...], preferred_element_type=F32)
                                        + jnp.dot(kb, wgk_ref[...], preferred_element_type=F32)
                                        + jnp.dot(vb, wgv_ref[...], preferred_element_type=F32))
        xc_ref[0, r0:r0 + ts, :] = xcb
        q_ref[0, r0:r0 + ts, :] = qb
        kt = lax.dot_general(wkt_ref[0], xcb, (((1,), (1,)), ((), ())), preferred_element_type=F32)
        kt_ref[0, :, r0:r0 + ts] = (kt * k_scale).astype(BF16)
        v_ref[0, r0:r0 + ts, :] = vb


def _block_diag(w, cw):
    g, qb, _ = w.shape
    per = cw // qb
    wr = w.reshape(g // per, per, qb, qb)
    eye = jnp.eye(per, dtype=w.dtype)
    return jnp.einsum("cgio,gh->cgiho", wr, eye).reshape(g // per, cw, cw)


def _mlstm_qkv(xm, conv_w, conv_b, w_q, w_k, w_v, w_gate, b_gate, b, s):
    di = xm.shape[1]
    cw = 256
    nchunk = di // cw
    nh = MLSTM_HEADS
    dh = di // nh
    ts = min(256, s)
    wk_bd = _block_diag(w_k, cw)
    wqk = jnp.concatenate([_block_diag(w_q, cw), wk_bd], axis=2).astype(BF16)
    wkt = jnp.swapaxes(wk_bd, 1, 2).astype(BF16)
    wv = _block_diag(w_v, cw).astype(BF16)
    wg = jnp.zeros((3 * di, LANES), F32).at[:, :2 * nh].set(w_gate).astype(BF16)
    bg = jnp.zeros((1, LANES), F32).at[0, :2 * nh].set(b_gate)
    xm3 = xm.reshape(b, s, di)
    blk = pl.BlockSpec((1, s, cw), lambda bi, c: (bi, 0, c))
    outs = pl.pallas_call(
        functools.partial(_mlstm_qkv_kernel, ts=ts, k_scale=dh ** -0.5),
        grid=(b, nchunk),
        in_specs=[blk,
                  pl.BlockSpec((1, MLSTM_CONV, cw), lambda bi, c: (c, 0, 0)),
                  pl.BlockSpec((1, 1, cw), lambda bi, c: (c, 0, 0)),
                  pl.BlockSpec((1, cw, 2 * cw), lambda bi, c: (c, 0, 0)),
                  pl.BlockSpec((1, cw, cw), lambda bi, c: (c, 0, 0)),
                  pl.BlockSpec((1, cw, cw), lambda bi, c: (c, 0, 0)),
                  pl.BlockSpec((cw, LANES), lambda bi, c: (c, 0)),
                  pl.BlockSpec((cw, LANES), lambda bi, c: (nchunk + c, 0)),
                  pl.BlockSpec((cw, LANES), lambda bi, c: (2 * nchunk + c, 0)),
                  pl.BlockSpec((1, LANES), lambda bi, c: (0, 0))],
        out_specs=[blk, blk, pl.BlockSpec((1, cw, s), lambda bi, c: (bi, c, 0)), blk,
                   pl.BlockSpec((1, s, LANES), lambda bi, c: (bi, 0, 0))],
        out_shape=[jax.ShapeDtypeStruct((b, s, di), BF16)] * 2 + [jax.ShapeDtypeStruct((b, di, s), BF16)]
        + [jax.ShapeDtypeStruct((b, s, di), BF16), jax.ShapeDtypeStruct((b, s, LANES), F32)],
        scratch_shapes=[pltpu.VMEM((s + SUBLANES, cw), F32)],
        compiler_params=_cparams("arbitrary", "arbitrary"),
        name="mlstm_qkv",
    )(xm3, conv_w.reshape(MLSTM_CONV, nchunk, cw).transpose(1, 0, 2), conv_b.reshape(nchunk, 1, cw),
      wqk, wkt, wv, wg, wg, wg, bg)
    return outs


def _log_sigmoid(x):
    return jnp.minimum(x, 0.0) - jnp.log1p(jnp.exp(-jnp.abs(x)))


def _split3(x):
    hi = x.astype(BF16)
    r = x - hi.astype(F32)
    mid = r.astype(BF16)
    lo = (r - mid.astype(F32)).astype(BF16)
    return hi, mid, lo


def _mlstm_cell_kernel(q_ref, kt_ref, v_ref, gc_ref, gr_ref, ng_ref, o_ref, ct_sc, n_sc, m_sc, *, nh):
    c = pl.program_id(1)
    L = q_ref.shape[1]
    dh = q_ref.shape[2] // nh

    @pl.when(c == 0)
    def _():
        ct_sc[...] = jnp.zeros(ct_sc.shape, F32)
        n_sc[...] = jnp.zeros(n_sc.shape, F32)
        m_sc[...] = jnp.zeros(m_sc.shape, F32)

    t_idx = lax.broadcasted_iota(jnp.int32, (L, L), 0)
    s_idx = lax.broadcasted_iota(jnp.int32, (L, L), 1)
    causal = s_idx <= t_idx
    tri = causal.astype(BF16)
    tri_t = (t_idx <= s_idx).astype(F32)
    gc = gc_ref[0]
    gr = gr_ref[0]
    b_cols = sum(jnp.dot(tri, part, preferred_element_type=F32) for part in _split3(_log_sigmoid(gc)))
    b_rows = jnp.dot(_log_sigmoid(gr), tri_t, preferred_element_type=F32, precision=HIGHEST)
    lane = lax.broadcasted_iota(jnp.int32, gc.shape, 1)

    for h in range(nh):
        cols = slice(h * dh, (h + 1) * dh)
        i_col = jnp.sum(jnp.where(lane == h, gc, 0.0), axis=1, keepdims=True)
        b_col = jnp.sum(jnp.where(lane == h + nh, b_cols, 0.0), axis=1, keepdims=True)
        i_row = gr[h:h + 1, :]
        b_row = b_rows[nh + h:nh + h + 1, :]
        u_row = i_row - b_row
        u_col = i_col - b_col

        m_prev = m_sc[h, 0:1, 0:1]
        dlog = jnp.where(causal, b_col + u_row, NEG_BIG)
        g = b_col + m_prev
        m_t = jnp.maximum(g, jnp.max(dlog, axis=1, keepdims=True))
        qb = q_ref[0, :, cols]
        ktb = kt_ref[0, cols, :]
        vb = v_ref[0, :, cols]
        w = jnp.exp(dlog - m_t) * jnp.dot(qb, ktb, preferred_element_type=F32)
        inter = jnp.exp(g - m_t)
        ct = ct_sc[h]
        num = (jnp.dot(w.astype(BF16), vb, preferred_element_type=F32)
               + inter * jnp.dot(qb, ct.astype(BF16), preferred_element_type=F32))
        den = (jnp.sum(w, axis=1, keepdims=True)
               + inter * jnp.dot(qb, n_sc[h].astype(BF16), preferred_element_type=F32))
        scale = 1.0 / jnp.maximum(jnp.abs(den), jnp.exp(-m_t))
        hh = num * jnp.tile(scale, (1, dh // LANES))

        mu = jnp.mean(hh, axis=1, keepdims=True)
        cen = hh - mu
        var = jnp.mean(cen * cen, axis=1, keepdims=True)
        o_ref[0, :, cols] = (cen * lax.rsqrt(var + NORM_EPS) * ng_ref[h]).astype(BF16)

        b_last = b_row[:, L - 1:L]
        m_new = jnp.maximum(b_last + m_prev, jnp.max(b_last + u_row, axis=1, keepdims=True))
        decay = jnp.exp(b_last + m_prev - m_new)
        ws_col = jnp.exp(b_last - m_new + u_col).astype(BF16)
        ct_sc[h] = decay * ct + jnp.dot(ktb, vb * ws_col, preferred_element_type=F32)
        n_sc[h] = decay * n_sc[h] + jnp.dot(ktb, jnp.broadcast_to(ws_col, (L, LANES)),
                                            preferred_element_type=F32)
        m_sc[h] = jnp.broadcast_to(m_new, m_sc.shape[1:])


def _mlstm_cell(q, kt, v, gates, norm_g, b, s):
    di = q.shape[2]
    nh = MLSTM_HEADS
    dh = di // nh
    L = min(256, s)
    gates_t = jnp.transpose(gates[:, :, :SUBLANES], (0, 2, 1))
    blk = pl.BlockSpec((1, L, di), lambda bi, c: (bi, c, 0))
    return pl.pallas_call(
        functools.partial(_mlstm_cell_kernel, nh=nh),
        grid=(b, s // L),
        in_specs=[blk,
                  pl.BlockSpec((1, di, L), lambda bi, c: (bi, 0, c)),
                  blk,
                  pl.BlockSpec((1, L, LANES), lambda bi, c: (bi, c, 0)),
                  pl.BlockSpec((1, SUBLANES, L), lambda bi, c: (bi, 0, c)),
                  pl.BlockSpec((nh, 1, dh), lambda bi, c: (0, 0, 0))],
        out_specs=blk,
        out_shape=jax.ShapeDtypeStruct((b, s, di), BF16),
        scratch_shapes=[pltpu.VMEM((nh, dh, dh), F32), pltpu.VMEM((nh, dh, LANES), F32),
                        pltpu.VMEM((nh, SUBLANES, LANES), F32)],
        compiler_params=_cparams("arbitrary", "arbitrary"),
        name="mlstm_cell",
    )(q, kt, v, gates, gates_t, norm_g.reshape(nh, 1, dh))


def _mlstm_out_kernel(hn_ref, xc_ref, z_ref, skip_ref, w_ref, x_ref, mod_ref, g_ref, wr_ref, br_ref,
                      x_out_ref, h_out_ref, route_ref, counts_ref, carry_sc, *, n_experts):
    i = pl.program_id(0)

    @pl.when(i == 0)
    def _():
        carry_sc[...] = jnp.zeros(carry_sc.shape, F32)

    inner = ((hn_ref[...].astype(F32) + skip_ref[...] * xc_ref[...].astype(F32))
             * _silu(z_ref[...].astype(F32)))
    h4 = _mixer_out_core(inner.astype(BF16), w_ref, x_ref, mod_ref, g_ref, x_out_ref)
    h_out_ref[...] = h4

    tm = h4.shape[0]
    lane = lax.broadcasted_iota(jnp.int32, (tm, LANES), 1)
    logits = jnp.dot(h4, wr_ref[...], preferred_element_type=F32, precision=HIGHEST) + br_ref[...]
    logits = jnp.where(lane < n_experts, logits, NEG_BIG)
    ex = jnp.exp(logits - jnp.max(logits, axis=1, keepdims=True))
    probs = ex / jnp.sum(ex, axis=1, keepdims=True)
    probs = jnp.where(lane < n_experts, probs, -1.0)
    lane_f = lane.astype(F32)
    p0 = jnp.max(probs, axis=1, keepdims=True)
    e0 = jnp.min(jnp.where(probs == p0, lane_f, float(LANES)), axis=1, keepdims=True)
    rest = jnp.where(lane_f == e0, -1.0, probs)
    p1 = jnp.max(rest, axis=1, keepdims=True)
    e1 = jnp.min(jnp.where(rest == p1, lane_f, float(LANES)), axis=1, keepdims=True)
    tot = p0 + p1
    sel0 = lane_f == e0
    sel1 = lane_f == e1
    sel = jnp.where(sel0 | sel1, 1.0, 0.0)
    r_idx = lax.broadcasted_iota(jnp.int32, (tm, tm), 0)
    c_idx = lax.broadcasted_iota(jnp.int32, (tm, tm), 1)
    before = (c_idx < r_idx).astype(BF16)
    cum = jnp.dot(before, sel.astype(BF16), preferred_element_type=F32) + carry_sc[0:1, :]
    rank0 = jnp.sum(jnp.where(sel0, cum, 0.0), axis=1, keepdims=True)
    rank1 = jnp.sum(jnp.where(sel1, cum, 0.0), axis=1, keepdims=True)
    carry = carry_sc[0:1, :] + jnp.sum(sel, axis=0, keepdims=True)
    carry_sc[...] = jnp.broadcast_to(carry, carry_sc.shape)
    counts_ref[...] = jnp.broadcast_to(carry, counts_ref.shape)
    vals = (p0 / tot, p1 / tot, e0, e1, rank0, rank1)
    route = jnp.zeros((tm, LANES), F32)
    for idx, val in enumerate(vals):
        route = jnp.where(lane == idx, val, route)
    route_ref[...] = route


def _mlstm_out(hn, xc, z, skip, w_bf16, x2d, mod, g, w_router, b_router, s):
    t, d = x2d.shape
    di = hn.shape[1]
    e = w_router.shape[1]
    tm = _token_tile(s)
    tpb = s // tm
    wr = jnp.zeros((d, LANES), F32).at[:, :e].set(w_router)
    br = jnp.zeros((1, LANES), F32).at[0, :e].set(b_router)
    tok = lambda n: pl.BlockSpec((tm, n), lambda i: (i, 0))
    return pl.pallas_call(
        functools.partial(_mlstm_out_kernel, n_experts=e),
        grid=(t // tm,),
        in_specs=[tok(di), tok(di), tok(di),
                  pl.BlockSpec((1, di), lambda i: (0, 0)),
                  pl.BlockSpec(w_bf16.shape, lambda i: (0, 0)),
                  tok(d),
                  pl.BlockSpec((1, 6, d), lambda i: (i // tpb, 0, 0)),
                  pl.BlockSpec((4, d), lambda i: (0, 0)),
                  pl.BlockSpec((d, LANES), lambda i: (0, 0)),
                  pl.BlockSpec((1, LANES), lambda i: (0, 0))],
        out_specs=[tok(d), tok(d), tok(LANES), pl.BlockSpec((SUBLANES, LANES), lambda i: (0, 0))],
        out_shape=[jax.ShapeDtypeStruct((t, d), F32), jax.ShapeDtypeStruct((t, d), F32),
                   jax.ShapeDtypeStruct((t, LANES), F32), jax.ShapeDtypeStruct((SUBLANES, LANES), F32)],
        scratch_shapes=[pltpu.VMEM((SUBLANES, LANES), F32)],
        compiler_params=_cparams("arbitrary"),
        name="mlstm_out_router",
    )(hn, xc, z, skip.reshape(1, di), w_bf16, x2d, mod, g, wr, br)


def _dispatch_kernel(pos_ref, h_ref, xs_in_ref, xs_ref, sem):
    del xs_in_ref
    tm = h_ref.shape[0]

    def row_copy(r, slot):
        dst = pos_ref[0, slot, r]
        return pltpu.make_async_copy(h_ref.at[pl.ds(r, 1)], xs_ref.at[pl.ds(dst, 1)], sem)

    def start(r, carry):
        row_copy(r, 0).start()
        row_copy(r, 1).start()
        return carry

    def wait(r, carry):
        row_copy(r, 0).wait()
        row_copy(r, 1).wait()
        return carry

    lax.fori_loop(0, tm, start, 0)
    lax.fori_loop(0, tm, wait, 0)


def _dispatch(h4, pos, n_rows, s):
    t, d = h4.shape
    tm = _token_tile(s)
    xs0 = jnp.zeros((n_rows, d), F32)
    return pl.pallas_call(
        _dispatch_kernel,
        grid=(t // tm,),
        in_specs=[pl.BlockSpec((1, TOP_K, tm), lambda i: (i, 0, 0), memory_space=pltpu.SMEM),
                  pl.BlockSpec((tm, d), lambda i: (i, 0)),
                  pl.BlockSpec(memory_space=pl.ANY)],
        out_specs=pl.BlockSpec(memory_space=pl.ANY),
        out_shape=jax.ShapeDtypeStruct((n_rows, d), F32),
        scratch_shapes=[pltpu.SemaphoreType.DMA(())],
        input_output_aliases={2: 0},
        compiler_params=_cparams("arbitrary"),
        name="moe_dispatch",
    )(pos, h4, xs0)


def _expert_ffn_kernel(te_ref, nreal_ref, xs_ref, wg_ref, wu_ref, wd_ref, ys_ref, xb_sc, acc_sc):
    i = pl.program_id(0)
    j = pl.program_id(1)
    last = pl.num_programs(1) - 1
    real = i < nreal_ref[0]

    @pl.when(real & (j == 0))
    def _():
        xb_sc[...] = xs_ref[...].astype(BF16)
        acc_sc[...] = jnp.zeros(acc_sc.shape, F32)

    @pl.when(real)
    def _():
        xb = xb_sc[...]
        gate = jnp.dot(xb, wg_ref[0], preferred_element_type=F32)
        up = jnp.dot(xb, wu_ref[0], preferred_element_type=F32)
        act = (_silu(gate) * up).astype(BF16)
        acc_sc[...] += jnp.dot(act, wd_ref[0], preferred_element_type=F32)

    @pl.when(real & (j == last))
    def _():
        ys_ref[...] = acc_sc[...]

    @pl.when(jnp.logical_not(real) & (j == last))
    def _():
        ys_ref[...] = jnp.zeros(ys_ref.shape, F32)


def _expert_ffn(xs, tile_expert, n_real, wg, wu, wd, tm):
    p, d = xs.shape
    f = wg.shape[2]
    tf = _ffn_chunk(f)
    nj = f // tf
    n_tiles = p // tm

    def row_idx(i, j, te, nr):
        return (jnp.minimum(i, nr[0] - 1), 0)

    def col_j(i, j, nr):
        return jnp.where(i < nr[0], j, nj - 1)

    grid_spec = pltpu.PrefetchScalarGridSpec(
        num_scalar_prefetch=2,
        grid=(n_tiles, nj),
        in_specs=[pl.BlockSpec((tm, d), row_idx),
                  pl.BlockSpec((1, d, tf), lambda i, j, te, nr: (te[i], 0, col_j(i, j, nr))),
                  pl.BlockSpec((1, d, tf), lambda i, j, te, nr: (te[i], 0, col_j(i, j, nr))),
                  pl.BlockSpec((1, tf, d), lambda i, j, te, nr: (te[i], col_j(i, j, nr), 0))],
        out_specs=pl.BlockSpec((tm, d), lambda i, j, te, nr: (i, 0)),
        scratch_shapes=[pltpu.VMEM((tm, d), BF16), pltpu.VMEM((tm, d), F32)],
    )
    return pl.pallas_call(
        _expert_ffn_kernel,
        grid_spec=grid_spec,
        out_shape=jax.ShapeDtypeStruct((p, d), F32),
        compiler_params=_cparams("arbitrary", "arbitrary"),
        name="moe_expert_ffn",
    )(tile_expert, n_real, xs, wg, wu, wd)


def _combine_kernel(pos_ref, ys_ref, route_ref, x_ref, mod_ref, g_ref, o_ref, a_sc, b_sc, sem):
    tm = x_ref.shape[0]

    def row_copy(r, slot, buf):
        src = pos_ref[0, slot, r]
        return pltpu.make_async_copy(ys_ref.at[pl.ds(src, 1)], buf.at[pl.ds(r, 1)], sem)

    def start(r, carry):
        row_copy(r, 0, a_sc).start()
        row_copy(r, 1, b_sc).start()
        return carry

    def wait(r, carry):
        row_copy(r, 0, a_sc).wait()
        row_copy(r, 1, b_sc).wait()
        return carry

    lax.fori_loop(0, tm, start, 0)
    lax.fori_loop(0, tm, wait, 0)
    route = route_ref[...]
    y = route[:, 0:1] * a_sc[...] + route[:, 1:2] * b_sc[...]
    o_ref[...] = x_ref[...] + mod_ref[0, 5:6, :] * _rms(y, g_ref[3:4, :])


def _combine(ys, pos, route, x2d, mod, g, s):
    t, d = x2d.shape
    tm = _token_tile(s)
    tpb = s // tm
    return pl.pallas_call(
        _combine_kernel,
        grid=(t // tm,),
        in_specs=[pl.BlockSpec((1, TOP_K, tm), lambda i: (i, 0, 0), memory_space=pltpu.SMEM),
                  pl.BlockSpec(memory_space=pl.ANY),
                  pl.BlockSpec((tm, LANES), lambda i: (i, 0)),
                  pl.BlockSpec((tm, d), lambda i: (i, 0)),
                  pl.BlockSpec((1, 6, d), lambda i: (i // tpb, 0, 0)),
                  pl.BlockSpec((4, d), lambda i: (0, 0))],
        out_specs=pl.BlockSpec((tm, d), lambda i: (i, 0)),
        out_shape=jax.ShapeDtypeStruct((t, d), F32),
        scratch_shapes=[pltpu.VMEM((tm, d), F32), pltpu.VMEM((tm, d), F32), pltpu.SemaphoreType.DMA(())],
        compiler_params=_cparams("arbitrary"),
        name="moe_combine",
    )(pos, ys, route, x2d, mod, g)


def _moe(h4, route, counts, x2d, mod, g, wg, wu, wd, s):
    t, d = h4.shape
    e = wg.shape[0]
    tm_tok = _token_tile(s)
    tm = min(512, t)
    n_tiles = (TOP_K * t) // tm + e
    cnt = counts[0, :e].astype(jnp.int32)
    tiles_e = (cnt + tm - 1) // tm
    ends = jnp.cumsum(tiles_e)
    off = (ends - tiles_e) * tm
    e0 = route[:, 2].astype(jnp.int32)
    e1 = route[:, 3].astype(jnp.int32)
    experts = jnp.arange(e, dtype=jnp.int32)[None, :]
    pos0 = jnp.sum(jnp.where(e0[:, None] == experts, off[None, :], 0), axis=1) + route[:, 4].astype(jnp.int32)
    pos1 = jnp.sum(jnp.where(e1[:, None] == experts, off[None, :], 0), axis=1) + route[:, 5].astype(jnp.int32)
    pos = jnp.stack([pos0.reshape(t // tm_tok, tm_tok), pos1.reshape(t // tm_tok, tm_tok)], axis=1)
    n_real = ends[e - 1:e]
    tile_ids = jnp.arange(n_tiles, dtype=jnp.int32)
    tile_expert = jnp.sum(tile_ids[:, None] >= ends[None, :], axis=1).astype(jnp.int32)
    last_expert = jnp.sum(n_real[0] - 1 >= ends).astype(jnp.int32)
    tile_expert = jnp.where(tile_ids < n_real[0], tile_expert, last_expert)
    xs = _dispatch(h4, pos, n_tiles * tm, s)
    ys = _expert_ffn(xs, tile_expert, n_real.astype(jnp.int32), wg, wu, wd, tm)
    return _combine(ys, pos, route, x2d, mod, g, s)


def kernel(x, c, mod_w, mod_b, norm_g, attn_w_in, attn_w_out, attn_lambda, attn_subln, ffn_w_gate, ffn_w_up, ffn_w_down, mlstm_w_in, mlstm_conv_w, mlstm_conv_b, mlstm_w_q, mlstm_w_k, mlstm_w_v, mlstm_w_gate, mlstm_b_gate, mlstm_skip, mlstm_norm, mlstm_w_out, moe_w_router, moe_b_router, moe_w_gate, moe_w_up, moe_w_down):
    b, s, d = x.shape
    depth = mod_w.shape[0]
    mod_all = _modulation(c, mod_w, mod_b)
    x2d = x.reshape(b * s, d)
    da = d // (2 * ATTN_HEADS)
    for i in range(depth):
        mod = mod_all[i]
        g = norm_g[i]
        j = i // N_MIXERS
        if i % N_MIXERS == 0:
            lambda_init = 0.8 - 0.6 * math.exp(-0.3 * i)
            q, k, v = _in_proj(x2d, mod, g, attn_w_in[j].astype(BF16), 3, (LOG2E * da ** -0.5, 1.0, 1.0), s,
                               "attn_in_proj")
            o = _diff_attention(q, k, v, attn_lambda[j], attn_subln[j], lambda_init, b, s)
            x2d, h2 = _attn_out(o, attn_w_out[j].astype(BF16), x2d, mod, g, s)
            x2d = _dense_ffn(h2, ffn_w_gate[j].astype(BF16), ffn_w_up[j].astype(BF16),
                             ffn_w_down[j].astype(BF16), x2d, mod, g, s)
        else:
            xm, z = _in_proj(x2d, mod, g, mlstm_w_in[j].astype(BF16), 2, (1.0, 1.0), s, "mlstm_in_proj")
            xc, q, kt, v, gates = _mlstm_qkv(xm, mlstm_conv_w[j], mlstm_conv_b[j], mlstm_w_q[j], mlstm_w_k[j],
                                             mlstm_w_v[j], mlstm_w_gate[j], mlstm_b_gate[j], b, s)
            hn = _mlstm_cell(q, kt, v, gates, mlstm_norm[j], b, s)
            di = hn.shape[2]
            x2d, h4, route, counts = _mlstm_out(hn.reshape(b * s, di), xc.reshape(b * s, di), z, mlstm_skip[j],
                                                mlstm_w_out[j].astype(BF16), x2d, mod, g,
                                                moe_w_router[j], moe_b_router[j], s)
            x2d = _moe(h4, route, counts, x2d, mod, g, moe_w_gate[j].astype(BF16), moe_w_up[j].astype(BF16),
                       moe_w_down[j].astype(BF16), s)
    return x2d.reshape(b, s, d)
```

```python
import functools
import math

import jax
import jax.numpy as jnp
from jax import lax
from jax.experimental import pallas as pl
from jax.experimental.pallas import tpu as pltpu

F32 = jnp.float32
BF16 = jnp.bfloat16
HIGHEST = lax.Precision.HIGHEST

CHUNK = 64
ATTN_HEADS = 8
ALIBI_MAX_BIAS = 8.0
MLSTM_HEADS = 4
MLSTM_CONV = 4
QKV_BLOCK = 4
TOP_K = 2
NORM_EPS = 1e-6
N_MIXERS = 2

LANES = 128
SUBLANES = 8
VMEM_LIMIT_BYTES = 56 * 1024 * 1024
EXPERT_TILE = 512
DISPATCH_ROWS = 128
NEG_BIG = -1e30
LOG2E = math.log2(math.e)


def _cparams(*sem):
    return pltpu.CompilerParams(dimension_semantics=sem, vmem_limit_bytes=VMEM_LIMIT_BYTES)


def _rms(x, g):
    return x * lax.rsqrt(jnp.mean(x * x, axis=-1, keepdims=True) + NORM_EPS) * g


def _silu(x):
    return x * jax.nn.sigmoid(x)


def _token_tile(s):
    return min(512, s)


def _mod_kernel(c_ref, w_ref, b_ref, o_ref):
    cond = _silu(c_ref[...])
    o_ref[0] = jnp.dot(cond, w_ref[0], preferred_element_type=F32, precision=HIGHEST) + b_ref[0]


def _modulation(c, mod_w, mod_b):
    depth, d, n = mod_w.shape
    b = c.shape[0]
    tn = 1536 if n % 1536 == 0 else n
    out = pl.pallas_call(
        _mod_kernel,
        grid=(depth, n // tn),
        in_specs=[pl.BlockSpec((b, d), lambda i, j: (0, 0)),
                  pl.BlockSpec((1, d, tn), lambda i, j: (i, 0, j)),
                  pl.BlockSpec((1, 1, tn), lambda i, j: (i, 0, j))],
        out_specs=pl.BlockSpec((1, b, tn), lambda i, j: (i, 0, j)),
        out_shape=jax.ShapeDtypeStruct((depth, b, n), F32),
        compiler_params=_cparams("arbitrary", "arbitrary"),
        name="modulation",
    )(c, mod_w, mod_b.reshape(depth, 1, n))
    return out.reshape(depth, b, 6, d)


def _in_proj_kernel(x_ref, mod_ref, g_ref, w_ref, *o_refs, scales):
    x = x_ref[...]
    h = _rms(x, g_ref[0:1, :]) * (1.0 + mod_ref[0, 1:2, :]) + mod_ref[0, 0:1, :]
    hb = h.astype(BF16)
    n = o_refs[0].shape[1]
    for idx, o_ref in enumerate(o_refs):
        r = jnp.dot(hb, w_ref[:, idx * n:(idx + 1) * n], preferred_element_type=F32)
        if scales[idx] != 1.0:
            r = r * scales[idx]
        o_ref[...] = r.astype(BF16)


def _in_proj(x2d, mod, g, w_bf16, n_out, scales, s, name):
    t, d = x2d.shape
    tm = _token_tile(s)
    tpb = s // tm
    n = w_bf16.shape[1] // n_out
    return pl.pallas_call(
        functools.partial(_in_proj_kernel, scales=scales),
        grid=(t // tm,),
        in_specs=[pl.BlockSpec((tm, d), lambda i: (i, 0)),
                  pl.BlockSpec((1, 6, d), lambda i: (i // tpb, 0, 0)),
                  pl.BlockSpec((4, d), lambda i: (0, 0)),
                  pl.BlockSpec(w_bf16.shape, lambda i: (0, 0))],
        out_specs=[pl.BlockSpec((tm, n), lambda i: (i, 0))] * n_out,
        out_shape=[jax.ShapeDtypeStruct((t, n), BF16)] * n_out,
        compiler_params=_cparams("arbitrary"),
        name=name,
    )(x2d, mod, g, w_bf16)


def _attn_kernel(slopes_ref, lam_ref, subln_ref, q_ref, k_ref, v_ref, o_ref, m_sc, acc_sc, bias_sc, vext_sc,
                 s_sc, smax_sc, *, tq, tk, lambda_init):
    h = pl.program_id(1)
    qi = pl.program_id(2)
    slope = slopes_ref[h]
    hd = q_ref.shape[2]
    half = hd // 2

    @pl.when(qi == 0)
    def _():
        vext_sc[:, :hd] = v_ref[0]
        vext_sc[:, hd:] = jnp.ones((vext_sc.shape[0], hd), BF16)
        row = lax.broadcasted_iota(jnp.int32, (tq, tk), 0)
        col = lax.broadcasted_iota(jnp.int32, (tq, tk), 1)
        dist = (row - col).astype(F32)
        bias_sc[0] = -slope * dist
        bias_sc[1] = jnp.where(col // CHUNK <= row // CHUNK, -slope * jnp.abs(dist), NEG_BIG)

    q = q_ref[0]
    lane = lax.broadcasted_iota(jnp.int32, q.shape, 1)
    zero = jnp.zeros_like(q)
    q2 = jnp.concatenate([jnp.where(lane < half, q, zero), jnp.where(lane >= half, q, zero)], axis=0)

    q_base = qi * tq
    n_full = qi

    m_sc[...] = jnp.full(m_sc.shape, NEG_BIG, F32)
    acc_sc[...] = jnp.zeros(acc_sc.shape, F32)

    def scores(j, slot):
        k_base = pl.multiple_of(j * tk, tk)
        kj = k_ref[0, pl.ds(k_base, tk), :]
        s = lax.dot_general(q2, kj, (((1,), (1,)), ((), ())), preferred_element_type=F32)
        bias = bias_sc[(j == n_full).astype(jnp.int32)]
        s = s + jnp.concatenate([bias, bias], axis=0)
        s_sc[slot] = s
        smax_sc[slot] = jnp.broadcast_to(jnp.max(s, axis=1, keepdims=True), smax_sc.shape[1:])

    def accumulate(j, slot):
        k_base = pl.multiple_of(j * tk, tk)
        const = jnp.where(j == n_full, 0.0, -slope * (q_base - k_base).astype(F32))
        m_prev = m_sc[...]
        m_next = jnp.maximum(m_prev, smax_sc[slot] + const)
        alpha = jnp.exp2(m_prev - m_next)
        p = jnp.exp2(s_sc[slot] - jnp.tile(m_next - const, (1, tk // LANES)))
        pv = jnp.dot(p.astype(BF16), vext_sc[pl.ds(k_base, tk), :], preferred_element_type=F32)
        acc_sc[...] = jnp.tile(alpha, (1, 2)) * acc_sc[...] + pv
        m_sc[...] = m_next

    def pair(i, carry):
        j = 2 * i
        scores(j + 1, 1)
        accumulate(j, 0)
        scores(j + 2, 0)
        accumulate(j + 1, 1)
        return carry

    scores(0, 0)
    n_pairs = n_full // 2
    lax.fori_loop(0, n_pairs, pair, 0)

    @pl.when(n_full % 2 == 1)
    def _():
        scores(n_full, 1)
        accumulate(n_full - 1, 0)
        accumulate(n_full, 1)

    @pl.when(n_full % 2 == 0)
    def _():
        accumulate(n_full, 0)

    lam = lam_ref[...]
    lam_full = (jnp.exp(jnp.sum(lam[0:1, :] * lam[1:2, :], axis=1, keepdims=True))
                - jnp.exp(jnp.sum(lam[2:3, :] * lam[3:4, :], axis=1, keepdims=True)) + lambda_init)
    acc = acc_sc[...]
    o_all = acc[:, :hd] / acc[:, hd:]
    o = o_all[:tq] - lam_full * o_all[tq:]
    o = _rms(o, subln_ref[...]) * (1.0 - lambda_init)
    o_ref[0] = o.astype(BF16)


def _diff_attention(q, k, v, lam, subln, lambda_init, b, s):
    d = q.shape[1]
    hd = d // ATTN_HEADS
    tq = tk = min(512, s)
    slopes = LOG2E * jnp.exp2(-ALIBI_MAX_BIAS * jnp.arange(1, ATTN_HEADS + 1, dtype=F32) / ATTN_HEADS)
    q3, k3, v3 = (a.reshape(b, s, d) for a in (q, k, v))
    grid_spec = pltpu.PrefetchScalarGridSpec(
        num_scalar_prefetch=1,
        grid=(b, ATTN_HEADS, s // tq),
        in_specs=[pl.BlockSpec(lam.shape, lambda bi, h, qi, sl: (0, 0)),
                  pl.BlockSpec((1, hd), lambda bi, h, qi, sl: (0, 0)),
                  pl.BlockSpec((1, tq, hd), lambda bi, h, qi, sl: (bi, qi, h)),
                  pl.BlockSpec((1, s, hd), lambda bi, h, qi, sl: (bi, 0, h)),
                  pl.BlockSpec((1, s, hd), lambda bi, h, qi, sl: (bi, 0, h))],
        out_specs=pl.BlockSpec((1, tq, hd), lambda bi, h, qi, sl: (bi, qi, h)),
        scratch_shapes=[pltpu.VMEM((2 * tq, LANES), F32), pltpu.VMEM((2 * tq, 2 * hd), F32),
                        pltpu.VMEM((2, tq, tk), F32), pltpu.VMEM((s, 2 * hd), BF16),
                        pltpu.VMEM((2, 2 * tq, tk), F32), pltpu.VMEM((2, 2 * tq, LANES), F32)],
    )
    o = pl.pallas_call(
        functools.partial(_attn_kernel, tq=tq, tk=tk, lambda_init=lambda_init),
        grid_spec=grid_spec,
        out_shape=jax.ShapeDtypeStruct((b, s, d), BF16),
        compiler_params=_cparams("arbitrary", "arbitrary", "arbitrary"),
        name="diff_attention",
    )(slopes, lam, subln.reshape(1, hd), q3, k3, v3)
    return o.reshape(b * s, d)


def _mixer_out_core(inp_bf16, w_ref, x_ref, mod_ref, g_ref, x_out_ref):
    y = jnp.dot(inp_bf16, w_ref[...], preferred_element_type=F32)
    x1 = x_ref[...] + mod_ref[0, 2:3, :] * _rms(y, g_ref[1:2, :])
    x_out_ref[...] = x1
    return _rms(x1, g_ref[2:3, :]) * (1.0 + mod_ref[0, 4:5, :]) + mod_ref[0, 3:4, :]


def _attn_out_kernel(o_ref, w_ref, x_ref, mod_ref, g_ref, x_out_ref, h_out_ref):
    h2 = _mixer_out_core(o_ref[...], w_ref, x_ref, mod_ref, g_ref, x_out_ref)
    h_out_ref[...] = h2.astype(BF16)


def _attn_out(o, w_bf16, x2d, mod, g, s):
    t, d = x2d.shape
    tm = _token_tile(s)
    tpb = s // tm
    return pl.pallas_call(
        _attn_out_kernel,
        grid=(t // tm,),
        in_specs=[pl.BlockSpec((tm, d), lambda i: (i, 0)),
                  pl.BlockSpec(w_bf16.shape, lambda i: (0, 0)),
                  pl.BlockSpec((tm, d), lambda i: (i, 0)),
                  pl.BlockSpec((1, 6, d), lambda i: (i // tpb, 0, 0)),
                  pl.BlockSpec((4, d), lambda i: (0, 0))],
        out_specs=[pl.BlockSpec((tm, d), lambda i: (i, 0)), pl.BlockSpec((tm, d), lambda i: (i, 0))],
        out_shape=[jax.ShapeDtypeStruct((t, d), F32), jax.ShapeDtypeStruct((t, d), BF16)],
        compiler_params=_cparams("arbitrary"),
        name="attn_out_proj",
    )(o, w_bf16, x2d, mod, g)


def _ffn_kernel(h_ref, wg_ref, wu_ref, wd_ref, x_ref, mod_ref, g_ref, o_ref, acc_ref):
    j = pl.program_id(1)

    @pl.when(j == 0)
    def _():
        acc_ref[...] = jnp.zeros(acc_ref.shape, F32)

    h = h_ref[...]
    gate = jnp.dot(h, wg_ref[...], preferred_element_type=F32)
    up = jnp.dot(h, wu_ref[...], preferred_element_type=F32)
    act = (_silu(gate) * up).astype(BF16)
    acc_ref[...] += jnp.dot(act, wd_ref[...], preferred_element_type=F32)

    @pl.when(j == pl.num_programs(1) - 1)
    def _():
        o_ref[...] = x_ref[...] + mod_ref[0, 5:6, :] * _rms(acc_ref[...], g_ref[3:4, :])


def _ffn_chunk(f):
    for tf in (1408, 1024, 896, 512, 256, 128):
        if f % tf == 0:
            return tf
    return f


def _dense_ffn(h2, wg, wu, wd, x2d, mod, g, s):
    t, d = x2d.shape
    f = wg.shape[1]
    tm = _token_tile(s)
    tpb = s // tm
    tf = _ffn_chunk(f)
    return pl.pallas_call(
        _ffn_kernel,
        grid=(t // tm, f // tf),
        in_specs=[pl.BlockSpec((tm, d), lambda i, j: (i, 0)),
                  pl.BlockSpec((d, tf), lambda i, j: (0, j)),
                  pl.BlockSpec((d, tf), lambda i, j: (0, j)),
                  pl.BlockSpec((tf, d), lambda i, j: (j, 0)),
                  pl.BlockSpec((tm, d), lambda i, j: (i, 0)),
                  pl.BlockSpec((1, 6, d), lambda i, j: (i // tpb, 0, 0)),
                  pl.BlockSpec((4, d), lambda i, j: (0, 0))],
        out_specs=pl.BlockSpec((tm, d), lambda i, j: (i, 0)),
        out_shape=jax.ShapeDtypeStruct((t, d), F32),
        scratch_shapes=[pltpu.VMEM((tm, d), F32)],
        compiler_params=_cparams("arbitrary", "arbitrary"),
        name="dense_ffn",
    )(h2, wg, wu, wd, x2d, mod, g)


def _mlstm_qkv_kernel(xm_ref, cw_ref, cb_ref, wqk_ref, wkt_ref, wv_ref, wgq_ref, wgk_ref, wgv_ref, bg_ref,
                      xc_ref, q_ref, kt_ref, v_ref, gates_ref, pad_sc, *, ts, k_scale):
    c = pl.program_id(1)
    s, cw = xm_ref.shape[1], xm_ref.shape[2]
    front = SUBLANES
    pad_sc[0:front, :] = jnp.zeros((front, cw), F32)
    pad_sc[front:front + s, :] = xm_ref[0].astype(F32)

    @pl.when(c == 0)
    def _():
        gates_ref[0] = jnp.broadcast_to(bg_ref[...], gates_ref.shape[1:])

    for r in range(s // ts):
        r0 = r * ts
        conv = cb_ref[0]
        for j in range(MLSTM_CONV):
            start = r0 + front - (MLSTM_CONV - 1) + j
            conv = conv + pad_sc[start:start + ts, :] * cw_ref[0, j:j + 1, :]
        xc = _silu(conv)
        xcb = xc.astype(BF16)
        qk = jnp.dot(xcb, wqk_ref[0], preferred_element_type=F32)
        qb = qk[:, :cw].astype(BF16)
        kb = qk[:, cw:].astype(BF16)
        vb = jnp.dot(xm_ref[0, r0:r0 + ts, :], wv_ref[0], preferred_element_type=F32).astype(BF16)
        gates_ref[0, r0:r0 + ts, :] += (jnp.dot(qb, wgq_ref[...], preferred_element_type=F32)
                                        + jnp.dot(kb, wgk_ref[...], preferred_element_type=F32)
                                        + jnp.dot(vb, wgv_ref[...], preferred_element_type=F32))
        xc_ref[0, r0:r0 + ts, :] = xcb
        q_ref[0, r0:r0 + ts, :] = qb
        kt = lax.dot_general(wkt_ref[0], xcb, (((1,), (1,)), ((), ())), preferred_element_type=F32)
        kt_ref[0, :, r0:r0 + ts] = (kt * k_scale).astype(BF16)
        v_ref[0, r0:r0 + ts, :] = vb


def _block_diag(w, cw):
    g, qb, _ = w.shape
    per = cw // qb
    wr = w.reshape(g // per, per, qb, qb)
    eye = jnp.eye(per, dtype=w.dtype)
    return jnp.einsum("cgio,gh->cgiho", wr, eye).reshape(g // per, cw, cw)


def _mlstm_qkv(xm, conv_w, conv_b, w_q, w_k, w_v, w_gate, b_gate, b, s):
    di = xm.shape[1]
    cw = 256
    nchunk = di // cw
    nh = MLSTM_HEADS
    dh = di // nh
    ts = min(256, s)
    wk_bd = _block_diag(w_k, cw)
    wqk = jnp.concatenate([_block_diag(w_q, cw), wk_bd], axis=2).astype(BF16)
    wkt = jnp.swapaxes(wk_bd, 1, 2).astype(BF16)
    wv = _block_diag(w_v, cw).astype(BF16)
    wg = jnp.zeros((3 * di, LANES), F32).at[:, :2 * nh].set(w_gate).astype(BF16)
    bg = jnp.zeros((1, LANES), F32).at[0, :2 * nh].set(b_gate)
    xm3 = xm.reshape(b, s, di)
    blk = pl.BlockSpec((1, s, cw), lambda bi, c: (bi, 0, c))
    outs = pl.pallas_call(
        functools.partial(_mlstm_qkv_kernel, ts=ts, k_scale=dh ** -0.5),
        grid=(b, nchunk),
        in_specs=[blk,
                  pl.BlockSpec((1, MLSTM_CONV, cw), lambda bi, c: (c, 0, 0)),
                  pl.BlockSpec((1, 1, cw), lambda bi, c: (c, 0, 0)),
                  pl.BlockSpec((1, cw, 2 * cw), lambda bi, c: (c, 0, 0)),
                  pl.BlockSpec((1, cw, cw), lambda bi, c: (c, 0, 0)),
                  pl.BlockSpec((1, cw, cw), lambda bi, c: (c, 0, 0)),
                  pl.BlockSpec((cw, LANES), lambda bi, c: (c, 0)),
                  pl.BlockSpec((cw, LANES), lambda bi, c: (nchunk + c, 0)),
                  pl.BlockSpec((cw, LANES), lambda bi, c: (2 * nchunk + c, 0)),
                  pl.BlockSpec((1, LANES), lambda bi, c: (0, 0))],
        out_specs=[blk, blk, pl.BlockSpec((1, cw, s), lambda bi, c: (bi, c, 0)), blk,
                   pl.BlockSpec((1, s, LANES), lambda bi, c: (bi, 0, 0))],
        out_shape=[jax.ShapeDtypeStruct((b, s, di), BF16)] * 2 + [jax.ShapeDtypeStruct((b, di, s), BF16)]
        + [jax.ShapeDtypeStruct((b, s, di), BF16), jax.ShapeDtypeStruct((b, s, LANES), F32)],
        scratch_shapes=[pltpu.VMEM((s + SUBLANES, cw), F32)],
        compiler_params=_cparams("arbitrary", "arbitrary"),
        name="mlstm_qkv",
    )(xm3, conv_w.reshape(MLSTM_CONV, nchunk, cw).transpose(1, 0, 2), conv_b.reshape(nchunk, 1, cw),
      wqk, wkt, wv, wg, wg, wg, bg)
    return outs


def _log_sigmoid(x):
    return jnp.minimum(x, 0.0) - jnp.log1p(jnp.exp(-jnp.abs(x)))


def _split3(x):
    hi = x.astype(BF16)
    r = x - hi.astype(F32)
    mid = r.astype(BF16)
    lo = (r - mid.astype(F32)).astype(BF16)
    return hi, mid, lo


def _mlstm_cell_kernel(q_ref, kt_ref, v_ref, gc_ref, gr_ref, ng_ref, o_ref, ct_sc, n_sc, m_sc, *, nh):
    c = pl.program_id(1)
    L = q_ref.shape[1]
    dh = q_ref.shape[2] // nh

    @pl.when(c == 0)
    def _():
        ct_sc[...] = jnp.zeros(ct_sc.shape, F32)
        n_sc[...] = jnp.zeros(n_sc.shape, F32)
        m_sc[...] = jnp.zeros(m_sc.shape, F32)

    t_idx = lax.broadcasted_iota(jnp.int32, (L, L), 0)
    s_idx = lax.broadcasted_iota(jnp.int32, (L, L), 1)
    causal = s_idx <= t_idx
    tri = causal.astype(BF16)
    tri_t = (t_idx <= s_idx).astype(F32)
    gc = gc_ref[0]
    gr = gr_ref[0]
    b_cols = sum(jnp.dot(tri, part, preferred_element_type=F32) for part in _split3(_log_sigmoid(gc)))
    b_rows = jnp.dot(_log_sigmoid(gr), tri_t, preferred_element_type=F32, precision=HIGHEST)
    lane = lax.broadcasted_iota(jnp.int32, gc.shape, 1)

    for h in range(nh):
        cols = slice(h * dh, (h + 1) * dh)
        i_col = jnp.sum(jnp.where(lane == h, gc, 0.0), axis=1, keepdims=True)
        b_col = jnp.sum(jnp.where(lane == h + nh, b_cols, 0.0), axis=1, keepdims=True)
        i_row = gr[h:h + 1, :]
        b_row = b_rows[nh + h:nh + h + 1, :]
        u_row = i_row - b_row
        u_col = i_col - b_col

        m_prev = m_sc[h, 0:1, 0:1]
        dlog = jnp.where(causal, b_col + u_row, NEG_BIG)
        g = b_col + m_prev
        m_t = jnp.maximum(g, jnp.max(dlog, axis=1, keepdims=True))
        qb = q_ref[0, :, cols]
        ktb = kt_ref[0, cols, :]
        vb = v_ref[0, :, cols]
        w = jnp.exp(dlog - m_t) * jnp.dot(qb, ktb, preferred_element_type=F32)
        inter = jnp.exp(g - m_t)
        ct = ct_sc[h]
        num = (jnp.dot(w.astype(BF16), vb, preferred_element_type=F32)
               + inter * jnp.dot(qb, ct.astype(BF16), preferred_element_type=F32))
        den = (jnp.sum(w, axis=1, keepdims=True)
               + inter * jnp.dot(qb, n_sc[h].astype(BF16), preferred_element_type=F32))
        scale = 1.0 / jnp.maximum(jnp.abs(den), jnp.exp(-m_t))
        hh = num * jnp.tile(scale, (1, dh // LANES))

        mu = jnp.mean(hh, axis=1, keepdims=True)
        cen = hh - mu
        var = jnp.mean(cen * cen, axis=1, keepdims=True)
        o_ref[0, :, cols] = (cen * lax.rsqrt(var + NORM_EPS) * ng_ref[h]).astype(BF16)

        b_last = b_row[:, L - 1:L]
        m_new = jnp.maximum(b_last + m_prev, jnp.max(b_last + u_row, axis=1, keepdims=True))
        decay = jnp.exp(b_last + m_prev - m_new)
        ws_col = jnp.exp(b_last - m_new + u_col).astype(BF16)
        ct_sc[h] = decay * ct + jnp.dot(ktb, vb * ws_col, preferred_element_type=F32)
        n_sc[h] = decay * n_sc[h] + jnp.dot(ktb, jnp.broadcast_to(ws_col, (L, LANES)),
                                            preferred_element_type=F32)
        m_sc[h] = jnp.broadcast_to(m_new, m_sc.shape[1:])


def _mlstm_cell(q, kt, v, gates, norm_g, b, s):
    di = q.shape[2]
    nh = MLSTM_HEADS
    dh = di // nh
    L = min(256, s)
    gates_t = jnp.transpose(gates[:, :, :SUBLANES], (0, 2, 1))
    blk = pl.BlockSpec((1, L, di), lambda bi, c: (bi, c, 0))
    return pl.pallas_call(
        functools.partial(_mlstm_cell_kernel, nh=nh),
        grid=(b, s // L),
        in_specs=[blk,
                  pl.BlockSpec((1, di, L), lambda bi, c: (bi, 0, c)),
                  blk,
                  pl.BlockSpec((1, L, LANES), lambda bi, c: (bi, c, 0)),
                  pl.BlockSpec((1, SUBLANES, L), lambda bi, c: (bi, 0, c)),
                  pl.BlockSpec((nh, 1, dh), lambda bi, c: (0, 0, 0))],
        out_specs=blk,
        out_shape=jax.ShapeDtypeStruct((b, s, di), BF16),
        scratch_shapes=[pltpu.VMEM((nh, dh, dh), F32), pltpu.VMEM((nh, dh, LANES), F32),
                        pltpu.VMEM((nh, SUBLANES, LANES), F32)],
        compiler_params=_cparams("arbitrary", "arbitrary"),
        name="mlstm_cell",
    )(q, kt, v, gates, gates_t, norm_g.reshape(nh, 1, dh))


def _mlstm_out_kernel(hn_ref, xc_ref, z_ref, skip_ref, w_ref, x_ref, mod_ref, g_ref, wr_ref, br_ref,
                      x_out_ref, h_out_ref, route_ref, meta_ref, *, n_experts):
    inner = ((hn_ref[...].astype(F32) + skip_ref[...] * xc_ref[...].astype(F32))
             * _silu(z_ref[...].astype(F32)))
    h4 = _mixer_out_core(inner.astype(BF16), w_ref, x_ref, mod_ref, g_ref, x_out_ref)
    h_out_ref[...] = h4.astype(BF16)

    tm = h4.shape[0]
    lane = lax.broadcasted_iota(jnp.int32, (tm, LANES), 1)
    logits = jnp.dot(h4, wr_ref[...], preferred_element_type=F32, precision=HIGHEST) + br_ref[...]
    logits = jnp.where(lane < n_experts, logits, NEG_BIG)
    ex = jnp.exp(logits - jnp.max(logits, axis=1, keepdims=True))
    probs = ex / jnp.sum(ex, axis=1, keepdims=True)
    probs = jnp.where(lane < n_experts, probs, -1.0)
    lane_f = lane.astype(F32)
    p0 = jnp.max(probs, axis=1, keepdims=True)
    e0 = jnp.min(jnp.where(probs == p0, lane_f, float(LANES)), axis=1, keepdims=True)
    rest = jnp.where(lane_f == e0, -1.0, probs)
    p1 = jnp.max(rest, axis=1, keepdims=True)
    e1 = jnp.min(jnp.where(rest == p1, lane_f, float(LANES)), axis=1, keepdims=True)
    tot = p0 + p1
    sel0 = lane_f == e0
    sel1 = lane_f == e1
    sel = jnp.where(sel0 | sel1, 1.0, 0.0)
    r_idx = lax.broadcasted_iota(jnp.int32, (tm, tm), 0)
    c_idx = lax.broadcasted_iota(jnp.int32, (tm, tm), 1)
    before = (c_idx < r_idx).astype(BF16)
    cum = jnp.dot(before, sel.astype(BF16), preferred_element_type=F32)
    run8 = jnp.floor((jnp.sum(sel, axis=0, keepdims=True) + 7.0) * 0.125)
    e_r = lax.broadcasted_iota(jnp.int32, (LANES, LANES), 0)
    e_c = lax.broadcasted_iota(jnp.int32, (LANES, LANES), 1)
    seg8 = jnp.dot(jnp.broadcast_to(run8, (SUBLANES, LANES)).astype(BF16), (e_r < e_c).astype(BF16),
                   preferred_element_type=F32)[0:1, :]
    pos = cum + 8.0 * seg8
    pos0 = jnp.sum(jnp.where(sel0, pos, 0.0), axis=1, keepdims=True)
    pos1 = jnp.sum(jnp.where(sel1, pos, 0.0), axis=1, keepdims=True)
    vals = (p0 / tot, p1 / tot, e0, e1, pos0, pos1)
    route = jnp.zeros((tm, LANES), F32)
    for idx, val in enumerate(vals):
        route = jnp.where(lane == idx, val, route)
    route_ref[...] = route
    meta = jnp.where(lane[0:1, :] < n_experts, run8, 0.0)
    meta = jnp.where((lane[0:1, :] >= n_experts) & (lane[0:1, :] < 2 * n_experts),
                     pltpu.roll(8.0 * seg8, n_experts, 1), meta)
    meta_ref[0] = jnp.broadcast_to(meta, meta_ref.shape[1:]).astype(jnp.int32)


def _mlstm_out(hn, xc, z, skip, w_bf16, x2d, mod, g, w_router, b_router, s):
    t, d = x2d.shape
    di = hn.shape[1]
    e = w_router.shape[1]
    tm = _token_tile(s)
    tpb = s // tm
    n_tok_tiles = t // tm
    wr = jnp.zeros((d, LANES), F32).at[:, :e].set(w_router)
    br = jnp.zeros((1, LANES), F32).at[0, :e].set(b_router)
    tok = lambda n: pl.BlockSpec((tm, n), lambda i: (i, 0))
    return pl.pallas_call(
        functools.partial(_mlstm_out_kernel, n_experts=e),
        grid=(n_tok_tiles,),
        in_specs=[tok(di), tok(di), tok(di),
                  pl.BlockSpec((1, di), lambda i: (0, 0)),
                  pl.BlockSpec(w_bf16.shape, lambda i: (0, 0)),
                  tok(d),
                  pl.BlockSpec((1, 6, d), lambda i: (i // tpb, 0, 0)),
                  pl.BlockSpec((4, d), lambda i: (0, 0)),
                  pl.BlockSpec((d, LANES), lambda i: (0, 0)),
                  pl.BlockSpec((1, LANES), lambda i: (0, 0))],
        out_specs=[tok(d), tok(d), tok(LANES), pl.BlockSpec((1, SUBLANES, LANES), lambda i: (i, 0, 0))],
        out_shape=[jax.ShapeDtypeStruct((t, d), F32), jax.ShapeDtypeStruct((t, d), BF16),
                   jax.ShapeDtypeStruct((t, LANES), F32),
                   jax.ShapeDtypeStruct((n_tok_tiles, SUBLANES, LANES), jnp.int32)],
        compiler_params=_cparams("arbitrary"),
        name="mlstm_out_router",
    )(hn, xc, z, skip.reshape(1, di), w_bf16, x2d, mod, g, wr, br)


def _dispatch_kernel(start_ref, run_ref, seg_ref, fill_ref, h_ref, route_ref, xs_ref, stage_sc, inflight_sm, sem,
                     *, n_experts):
    i = pl.program_id(0)
    tm = h_ref.shape[0]
    r_stage = stage_sc.shape[0] - DISPATCH_ROWS

    def copy(src_row, dst_row, rows):
        return pltpu.make_async_copy(stage_sc.at[pl.ds(src_row, rows)], xs_ref.at[pl.ds(dst_row, rows)], sem)

    def wait_copies(n, rows):
        def body(_, carry):
            copy(0, 0, rows).wait()
            return carry
        lax.fori_loop(0, n, body, 0)

    def zero_fill(first_row, n, rows):
        def body(j, carry):
            copy(0, pl.multiple_of(first_row + j * rows, SUBLANES), rows).start()
            return carry
        lax.fori_loop(0, n, body, 0)
        wait_copies(n, rows)

    @pl.when(i == 0)
    def _():
        inflight_sm[0] = 0
        stage_sc[r_stage:, :] = jnp.zeros((DISPATCH_ROWS, stage_sc.shape[1]), F32)

    route_t = jnp.transpose(route_ref[...])
    slot_row = lax.broadcasted_iota(jnp.int32, (r_stage, tm), 0).astype(F32)
    onehot = jnp.where((route_t[4:5, :] == slot_row) | (route_t[5:6, :] == slot_row), 1.0, 0.0).astype(BF16)
    sorted_rows = jnp.dot(onehot, h_ref[...], preferred_element_type=F32)

    wait_copies(inflight_sm[0], DISPATCH_ROWS)
    stage_sc[0:r_stage, :] = sorted_rows
    issued = 0
    for e in range(n_experts):
        rows = run_ref[i * n_experts + e] * SUBLANES
        start = start_ref[i * n_experts + e]
        seg = seg_ref[i * n_experts + e]
        for k in range(tm // DISPATCH_ROWS):
            @pl.when(rows > k * DISPATCH_ROWS)
            def _(seg=seg, start=start, k=k):
                copy(pl.multiple_of(seg + k * DISPATCH_ROWS, SUBLANES),
                     pl.multiple_of(start + k * DISPATCH_ROWS, SUBLANES), DISPATCH_ROWS).start()
        issued = issued + (rows + DISPATCH_ROWS - 1) // DISPATCH_ROWS
    inflight_sm[0] = issued

    @pl.when(i == pl.num_programs(0) - 1)
    def _():
        wait_copies(inflight_sm[0], DISPATCH_ROWS)
        stage_sc[0:DISPATCH_ROWS, :] = jnp.zeros((DISPATCH_ROWS, stage_sc.shape[1]), F32)
        for e in range(n_experts):
            zero_fill(fill_ref[e], fill_ref[n_experts + e], SUBLANES)
        zero_fill(fill_ref[2 * n_experts], fill_ref[2 * n_experts + 1], DISPATCH_ROWS)


def _dispatch(h4, route, starts, runs, segs, fill, n_rows, s, n_experts):
    t, d = h4.shape
    tm = _token_tile(s)
    r_stage = TOP_K * tm + SUBLANES * n_experts
    grid_spec = pltpu.PrefetchScalarGridSpec(
        num_scalar_prefetch=4,
        grid=(t // tm,),
        in_specs=[pl.BlockSpec((tm, d), lambda i, *_: (i, 0)),
                  pl.BlockSpec((tm, LANES), lambda i, *_: (i, 0))],
        out_specs=pl.BlockSpec(memory_space=pl.ANY),
        scratch_shapes=[pltpu.VMEM((r_stage + DISPATCH_ROWS, d), F32), pltpu.SMEM((1,), jnp.int32),
                        pltpu.SemaphoreType.DMA(())],
    )
    return pl.pallas_call(
        functools.partial(_dispatch_kernel, n_experts=n_experts),
        grid_spec=grid_spec,
        out_shape=jax.ShapeDtypeStruct((n_rows, d), F32),
        compiler_params=_cparams("arbitrary"),
        name="moe_dispatch",
    )(starts, runs, segs, fill, h4, route)


def _expert_ffn_kernel(te_ref, nreal_ref, xs_ref, wg_ref, wu_ref, wd_ref, ys_ref, xb_sc, acc_sc):
    i = pl.program_id(0)
    j = pl.program_id(1)
    last = pl.num_programs(1) - 1
    real = i < nreal_ref[0]

    @pl.when(real & (j == 0))
    def _():
        xb_sc[...] = xs_ref[...].astype(BF16)
        acc_sc[...] = jnp.zeros(acc_sc.shape, F32)

    @pl.when(real)
    def _():
        xb = xb_sc[...]
        gate = jnp.dot(xb, wg_ref[0], preferred_element_type=F32)
        up = jnp.dot(xb, wu_ref[0], preferred_element_type=F32)
        act = (_silu(gate) * up).astype(BF16)
        acc_sc[...] += jnp.dot(act, wd_ref[0], preferred_element_type=F32)

    @pl.when(real & (j == last))
    def _():
        ys_ref[...] = acc_sc[...]

    @pl.when(jnp.logical_not(real) & (j == last))
    def _():
        ys_ref[...] = jnp.zeros(ys_ref.shape, F32)


def _expert_ffn(xs, tile_expert, n_real, wg, wu, wd):
    p, d = xs.shape
    f = wg.shape[2]
    tm = EXPERT_TILE
    tf = _ffn_chunk(f)
    nj = f // tf

    def row_idx(i, j, te, nr):
        return (jnp.minimum(i, nr[0] - 1), 0)

    def col_j(i, j, nr):
        return jnp.where(i < nr[0], j, nj - 1)

    grid_spec = pltpu.PrefetchScalarGridSpec(
        num_scalar_prefetch=2,
        grid=(p // tm, nj),
        in_specs=[pl.BlockSpec((tm, d), row_idx),
                  pl.BlockSpec((1, d, tf), lambda i, j, te, nr: (te[i], 0, col_j(i, j, nr))),
                  pl.BlockSpec((1, d, tf), lambda i, j, te, nr: (te[i], 0, col_j(i, j, nr))),
                  pl.BlockSpec((1, tf, d), lambda i, j, te, nr: (te[i], col_j(i, j, nr), 0))],
        out_specs=pl.BlockSpec((tm, d), lambda i, j, te, nr: (i, 0)),
        scratch_shapes=[pltpu.VMEM((tm, d), BF16), pltpu.VMEM((tm, d), F32)],
    )
    return pl.pallas_call(
        _expert_ffn_kernel,
        grid_spec=grid_spec,
        out_shape=jax.ShapeDtypeStruct((p, d), F32),
        compiler_params=_cparams("arbitrary", "arbitrary"),
        name="moe_expert_ffn",
    )(tile_expert, n_real, xs, wg, wu, wd)


def _combine_kernel(start_ref, run_ref, seg_ref, ys_ref, route_ref, x_ref, mod_ref, g_ref, o_ref, buf_sc, y_sc,
                    src_sm, exp_sm, off_sm, sems, *, n_experts):
    i = pl.program_id(0)
    tm = x_ref.shape[0]

    def chunk_copy(ci):
        return pltpu.make_async_copy(ys_ref.at[pl.ds(pl.multiple_of(src_sm[ci], SUBLANES), DISPATCH_ROWS)],
                                     buf_sc.at[ci], sems.at[ci])

    n_chunks = 0
    for e in range(n_experts):
        start = start_ref[i * n_experts + e]
        rows = run_ref[i * n_experts + e] * SUBLANES
        seg = seg_ref[i * n_experts + e]
        for k in range(tm // DISPATCH_ROWS):
            @pl.when(rows > k * DISPATCH_ROWS)
            def _(start=start, seg=seg, k=k, ci=n_chunks + k):
                src_sm[ci] = start + k * DISPATCH_ROWS
                exp_sm[ci] = e
                off_sm[ci] = seg + k * DISPATCH_ROWS
                chunk_copy(ci).start()
        n_chunks = n_chunks + (rows + DISPATCH_ROWS - 1) // DISPATCH_ROWS

    route = route_ref[...]
    c0, c1 = route[:, 0:1], route[:, 1:2]
    e0, e1 = route[:, 2:3], route[:, 3:4]
    p0, p1 = route[:, 4:5], route[:, 5:6]
    lane = lax.broadcasted_iota(jnp.int32, (tm, DISPATCH_ROWS), 1).astype(F32)
    y_sc[...] = jnp.zeros(y_sc.shape, F32)

    def body(ci, carry):
        chunk_copy(ci).wait()
        ef = exp_sm[ci].astype(F32)
        target = lane + off_sm[ci].astype(F32)
        first = e0 == ef
        spread = jnp.where((first & (p0 == target)) | ((e1 == ef) & (p1 == target)), 1.0, 0.0).astype(BF16)
        rows_out = jnp.dot(spread, buf_sc[ci].astype(BF16), preferred_element_type=F32)
        y_sc[...] += jnp.where(first, c0, c1) * rows_out
        return carry

    lax.fori_loop(0, n_chunks, body, 0)
    o_ref[...] = x_ref[...] + mod_ref[0, 5:6, :] * _rms(y_sc[...], g_ref[3:4, :])


def _combine(ys, starts, runs, segs, route, x2d, mod, g, s, n_experts):
    t, d = x2d.shape
    tm = _token_tile(s)
    tpb = s // tm
    max_chunks = TOP_K * tm // DISPATCH_ROWS + n_experts
    grid_spec = pltpu.PrefetchScalarGridSpec(
        num_scalar_prefetch=3,
        grid=(t // tm,),
        in_specs=[pl.BlockSpec(memory_space=pl.ANY),
                  pl.BlockSpec((tm, LANES), lambda i, *_: (i, 0)),
                  pl.BlockSpec((tm, d), lambda i, *_: (i, 0)),
                  pl.BlockSpec((1, 6, d), lambda i, *_: (i // tpb, 0, 0)),
                  pl.BlockSpec((4, d), lambda i, *_: (0, 0))],
        out_specs=pl.BlockSpec((tm, d), lambda i, *_: (i, 0)),
        scratch_shapes=[pltpu.VMEM((max_chunks, DISPATCH_ROWS, d), F32), pltpu.VMEM((tm, d), F32),
                        pltpu.SMEM((max_chunks,), jnp.int32), pltpu.SMEM((max_chunks,), jnp.int32),
                        pltpu.SMEM((max_chunks,), jnp.int32), pltpu.SemaphoreType.DMA((max_chunks,))],
    )
    return pl.pallas_call(
        functools.partial(_combine_kernel, n_experts=n_experts),
        grid_spec=grid_spec,
        out_shape=jax.ShapeDtypeStruct((t, d), F32),
        compiler_params=_cparams("arbitrary"),
        name="moe_combine",
    )(starts, runs, segs, ys, route, x2d, mod, g)


def _moe(h4, route, meta, x2d, mod, g, wg, wu, wd, s):
    t, d = x2d.shape
    e = wg.shape[0]
    tm = EXPERT_TILE
    n_tok_tiles = t // _token_tile(s)
    n_tiles = (TOP_K * t + (SUBLANES * n_tok_tiles + DISPATCH_ROWS) * e) // tm + e
    runs = meta[:, 0, :e]
    segs = meta[:, 0, e:2 * e]
    used = SUBLANES * jnp.sum(runs, axis=0)
    tiles_e = jnp.where(used > 0, (used + DISPATCH_ROWS + tm - 1) // tm, 0)
    ends = jnp.cumsum(tiles_e)
    off = (ends - tiles_e) * tm
    starts = off[None, :] + SUBLANES * (jnp.cumsum(runs, axis=0) - runs)
    n_real = ends[e - 1:e].astype(jnp.int32)
    fill = jnp.concatenate([off + used, (tiles_e * tm - used) // SUBLANES,
                            n_real * tm, (n_tiles - n_real) * (tm // DISPATCH_ROWS)]).astype(jnp.int32)
    tile_ids = jnp.arange(n_tiles, dtype=jnp.int32)
    tile_expert = jnp.minimum(jnp.sum(tile_ids[:, None] >= ends[None, :], axis=1), e - 1)
    last_expert = jnp.minimum(jnp.sum(n_real[0] - 1 >= ends), e - 1)
    tile_expert = jnp.where(tile_ids < n_real[0], tile_expert, last_expert).astype(jnp.int32)
    flat = lambda a: a.reshape(-1).astype(jnp.int32)
    xs = _dispatch(h4, route, flat(starts), flat(runs), flat(segs), fill, n_tiles * tm, s, e)
    ys = _expert_ffn(xs, tile_expert, n_real, wg, wu, wd)
    return _combine(ys, flat(starts), flat(runs), flat(segs), route, x2d, mod, g, s, e)


def kernel(x, c, mod_w, mod_b, norm_g, attn_w_in, attn_w_out, attn_lambda, attn_subln, ffn_w_gate, ffn_w_up, ffn_w_down, mlstm_w_in, mlstm_conv_w, mlstm_conv_b, mlstm_w_q, mlstm_w_k, mlstm_w_v, mlstm_w_gate, mlstm_b_gate, mlstm_skip, mlstm_norm, mlstm_w_out, moe_w_router, moe_b_router, moe_w_gate, moe_w_up, moe_w_down):
    b, s, d = x.shape
    depth = mod_w.shape[0]
    mod_all = _modulation(c, mod_w, mod_b)
    x2d = x.reshape(b * s, d)
    da = d // (2 * ATTN_HEADS)
    for i in range(depth):
        mod = mod_all[i]
        g = norm_g[i]
        j = i // N_MIXERS
        if i % N_MIXERS == 0:
            lambda_init = 0.8 - 0.6 * math.exp(-0.3 * i)
            q, k, v = _in_proj(x2d, mod, g, attn_w_in[j].astype(BF16), 3, (LOG2E * da ** -0.5, 1.0, 1.0), s,
                               "attn_in_proj")
            o = _diff_attention(q, k, v, attn_lambda[j], attn_subln[j], lambda_init, b, s)
            x2d, h2 = _attn_out(o, attn_w_out[j].astype(BF16), x2d, mod, g, s)
            x2d = _dense_ffn(h2, ffn_w_gate[j].astype(BF16), ffn_w_up[j].astype(BF16),
                             ffn_w_down[j].astype(BF16), x2d, mod, g, s)
        else:
            xm, z = _in_proj(x2d, mod, g, mlstm_w_in[j].astype(BF16), 2, (1.0, 1.0), s, "mlstm_in_proj")
            xc, q, kt, v, gates = _mlstm_qkv(xm, mlstm_conv_w[j], mlstm_conv_b[j], mlstm_w_q[j], mlstm_w_k[j],
                                             mlstm_w_v[j], mlstm_w_gate[j], mlstm_b_gate[j], b, s)
            hn = _mlstm_cell(q, kt, v, gates, mlstm_norm[j], b, s)
            di = hn.shape[2]
            x2d, h4, route, meta = _mlstm_out(hn.reshape(b * s, di), xc.reshape(b * s, di), z, mlstm_skip[j],
                                              mlstm_w_out[j].astype(BF16), x2d, mod, g,
                                              moe_w_router[j], moe_b_router[j], s)
            x2d = _moe(h4, route, meta, x2d, mod, g, moe_w_gate[j].astype(BF16), moe_w_up[j].astype(BF16),
                       moe_w_down[j].astype(BF16), s)
    return x2d.reshape(b, s, d)
```

```python
import functools
import math

import jax
import jax.numpy as jnp
from jax import lax
from jax.experimental import pallas as pl
from jax.experimental.pallas import tpu as pltpu

F32 = jnp.float32
BF16 = jnp.bfloat16
HIGHEST = lax.Precision.HIGHEST

CHUNK = 64
ATTN_HEADS = 8
ALIBI_MAX_BIAS = 8.0
MLSTM_HEADS = 4
MLSTM_CONV = 4
QKV_BLOCK = 4
TOP_K = 2
NORM_EPS = 1e-6
N_MIXERS = 2

LANES = 128
SUBLANES = 8
VMEM_LIMIT_BYTES = 56 * 1024 * 1024
FFN_SUBCHUNK = 256
EXPERT_TILE = 512
DISPATCH_ROWS = 128
NEG_BIG = -1e30
LOG2E = math.log2(math.e)


def _cparams(*sem):
    return pltpu.CompilerParams(dimension_semantics=sem, vmem_limit_bytes=VMEM_LIMIT_BYTES)


def _rms(x, g):
    return x * lax.rsqrt(jnp.mean(x * x, axis=-1, keepdims=True) + NORM_EPS) * g


def _silu(x):
    return x * jax.nn.sigmoid(x)


def _token_tile(s):
    return min(512, s)


def _mod_kernel(c_ref, w_ref, b_ref, o_ref):
    cond = _silu(c_ref[...])
    o_ref[0] = jnp.dot(cond, w_ref[0], preferred_element_type=F32, precision=HIGHEST) + b_ref[0]


def _modulation(c, mod_w, mod_b):
    depth, d, n = mod_w.shape
    b = c.shape[0]
    tn = 1536 if n % 1536 == 0 else n
    out = pl.pallas_call(
        _mod_kernel,
        grid=(depth, n // tn),
        in_specs=[pl.BlockSpec((b, d), lambda i, j: (0, 0)),
                  pl.BlockSpec((1, d, tn), lambda i, j: (i, 0, j)),
                  pl.BlockSpec((1, 1, tn), lambda i, j: (i, 0, j))],
        out_specs=pl.BlockSpec((1, b, tn), lambda i, j: (i, 0, j)),
        out_shape=jax.ShapeDtypeStruct((depth, b, n), F32),
        compiler_params=_cparams("arbitrary", "arbitrary"),
        name="modulation",
    )(c, mod_w, mod_b.reshape(depth, 1, n))
    return out.reshape(depth, b, 6, d)


def _in_proj_kernel(x_ref, mod_ref, g_ref, w_ref, *o_refs, scales):
    x = x_ref[...]
    h = _rms(x, g_ref[0:1, :]) * (1.0 + mod_ref[0, 1:2, :]) + mod_ref[0, 0:1, :]
    hb = h.astype(BF16)
    n = o_refs[0].shape[1]
    for idx, o_ref in enumerate(o_refs):
        r = jnp.dot(hb, w_ref[:, idx * n:(idx + 1) * n], preferred_element_type=F32)
        if scales[idx] != 1.0:
            r = r * scales[idx]
        o_ref[...] = r.astype(BF16)


def _in_proj(x2d, mod, g, w_bf16, n_out, scales, s, name):
    t, d = x2d.shape
    tm = _token_tile(s)
    tpb = s // tm
    n = w_bf16.shape[1] // n_out
    return pl.pallas_call(
        functools.partial(_in_proj_kernel, scales=scales),
        grid=(t // tm,),
        in_specs=[pl.BlockSpec((tm, d), lambda i: (i, 0)),
                  pl.BlockSpec((1, 6, d), lambda i: (i // tpb, 0, 0)),
                  pl.BlockSpec((4, d), lambda i: (0, 0)),
                  pl.BlockSpec(w_bf16.shape, lambda i: (0, 0))],
        out_specs=[pl.BlockSpec((tm, n), lambda i: (i, 0))] * n_out,
        out_shape=[jax.ShapeDtypeStruct((t, n), BF16)] * n_out,
        compiler_params=_cparams("arbitrary"),
        name=name,
    )(x2d, mod, g, w_bf16)


def _attn_kernel(slopes_ref, lam_ref, subln_ref, q_ref, k_ref, v_ref, o_ref, m_sc, acc_sc, bias_sc, vext_sc,
                 s_sc, smax_sc, *, tq, tk, lambda_init):
    h = pl.program_id(1)
    qi = pl.program_id(2)
    slope = slopes_ref[h]
    hd = q_ref.shape[2]
    half = hd // 2

    @pl.when(qi == 0)
    def _():
        vext_sc[:, :hd] = v_ref[0]
        vext_sc[:, hd:] = jnp.ones((vext_sc.shape[0], hd), BF16)
        row = lax.broadcasted_iota(jnp.int32, (tq, tk), 0)
        col = lax.broadcasted_iota(jnp.int32, (tq, tk), 1)
        dist = (row - col).astype(F32)
        bias_sc[0] = -slope * dist
        bias_sc[1] = jnp.where(col // CHUNK <= row // CHUNK, -slope * jnp.abs(dist), NEG_BIG)

    q = q_ref[0]
    lane = lax.broadcasted_iota(jnp.int32, q.shape, 1)
    zero = jnp.zeros_like(q)
    q2 = jnp.concatenate([jnp.where(lane < half, q, zero), jnp.where(lane >= half, q, zero)], axis=0)

    q_base = qi * tq
    n_full = qi

    m_sc[...] = jnp.full(m_sc.shape, NEG_BIG, F32)
    acc_sc[...] = jnp.zeros(acc_sc.shape, F32)

    def scores(j, slot):
        k_base = pl.multiple_of(j * tk, tk)
        kj = k_ref[0, pl.ds(k_base, tk), :]
        s = lax.dot_general(q2, kj, (((1,), (1,)), ((), ())), preferred_element_type=F32)
        bias = bias_sc[(j == n_full).astype(jnp.int32)]
        s = s + jnp.concatenate([bias, bias], axis=0)
        s_sc[slot] = s
        smax_sc[slot] = jnp.broadcast_to(jnp.max(s, axis=1, keepdims=True), smax_sc.shape[1:])

    def accumulate(j, slot):
        k_base = pl.multiple_of(j * tk, tk)
        const = jnp.where(j == n_full, 0.0, -slope * (q_base - k_base).astype(F32))
        m_prev = m_sc[...]
        m_next = jnp.maximum(m_prev, smax_sc[slot] + const)
        alpha = jnp.exp2(m_prev - m_next)
        p = jnp.exp2(s_sc[slot] - jnp.tile(m_next - const, (1, tk // LANES)))
        pv = jnp.dot(p.astype(BF16), vext_sc[pl.ds(k_base, tk), :], preferred_element_type=F32)
        acc_sc[...] = jnp.tile(alpha, (1, 2)) * acc_sc[...] + pv
        m_sc[...] = m_next

    def pair(i, carry):
        j = 2 * i
        scores(j + 1, 1)
        accumulate(j, 0)
        scores(j + 2, 0)
        accumulate(j + 1, 1)
        return carry

    scores(0, 0)
    n_pairs = n_full // 2
    lax.fori_loop(0, n_pairs, pair, 0)

    @pl.when(n_full % 2 == 1)
    def _():
        scores(n_full, 1)
        accumulate(n_full - 1, 0)
        accumulate(n_full, 1)

    @pl.when(n_full % 2 == 0)
    def _():
        accumulate(n_full, 0)

    lam = lam_ref[...]
    lam_full = (jnp.exp(jnp.sum(lam[0:1, :] * lam[1:2, :], axis=1, keepdims=True))
                - jnp.exp(jnp.sum(lam[2:3, :] * lam[3:4, :], axis=1, keepdims=True)) + lambda_init)
    acc = acc_sc[...]
    o_all = acc[:, :hd] / acc[:, hd:]
    o = o_all[:tq] - lam_full * o_all[tq:]
    o = _rms(o, subln_ref[...]) * (1.0 - lambda_init)
    o_ref[0] = o.astype(BF16)


def _diff_attention(q, k, v, lam, subln, lambda_init, b, s):
    d = q.shape[1]
    hd = d // ATTN_HEADS
    tq = tk = min(512, s)
    slopes = LOG2E * jnp.exp2(-ALIBI_MAX_BIAS * jnp.arange(1, ATTN_HEADS + 1, dtype=F32) / ATTN_HEADS)
    q3, k3, v3 = (a.reshape(b, s, d) for a in (q, k, v))
    grid_spec = pltpu.PrefetchScalarGridSpec(
        num_scalar_prefetch=1,
        grid=(b, ATTN_HEADS, s // tq),
        in_specs=[pl.BlockSpec(lam.shape, lambda bi, h, qi, sl: (0, 0)),
                  pl.BlockSpec((1, hd), lambda bi, h, qi, sl: (0, 0)),
                  pl.BlockSpec((1, tq, hd), lambda bi, h, qi, sl: (bi, qi, h)),
                  pl.BlockSpec((1, s, hd), lambda bi, h, qi, sl: (bi, 0, h)),
                  pl.BlockSpec((1, s, hd), lambda bi, h, qi, sl: (bi, 0, h))],
        out_specs=pl.BlockSpec((1, tq, hd), lambda bi, h, qi, sl: (bi, qi, h)),
        scratch_shapes=[pltpu.VMEM((2 * tq, LANES), F32), pltpu.VMEM((2 * tq, 2 * hd), F32),
                        pltpu.VMEM((2, tq, tk), F32), pltpu.VMEM((s, 2 * hd), BF16),
                        pltpu.VMEM((2, 2 * tq, tk), F32), pltpu.VMEM((2, 2 * tq, LANES), F32)],
    )
    o = pl.pallas_call(
        functools.partial(_attn_kernel, tq=tq, tk=tk, lambda_init=lambda_init),
        grid_spec=grid_spec,
        out_shape=jax.ShapeDtypeStruct((b, s, d), BF16),
        compiler_params=_cparams("arbitrary", "arbitrary", "arbitrary"),
        name="diff_attention",
    )(slopes, lam, subln.reshape(1, hd), q3, k3, v3)
    return o.reshape(b * s, d)


def _mixer_out_core(inp_bf16, w_ref, x_ref, mod_ref, g_ref, x_out_ref):
    y = jnp.dot(inp_bf16, w_ref[...], preferred_element_type=F32)
    x1 = x_ref[...] + mod_ref[0, 2:3, :] * _rms(y, g_ref[1:2, :])
    x_out_ref[...] = x1
    return _rms(x1, g_ref[2:3, :]) * (1.0 + mod_ref[0, 4:5, :]) + mod_ref[0, 3:4, :]


def _attn_out_kernel(o_ref, w_ref, x_ref, mod_ref, g_ref, x_out_ref, h_out_ref):
    h2 = _mixer_out_core(o_ref[...], w_ref, x_ref, mod_ref, g_ref, x_out_ref)
    h_out_ref[...] = h2.astype(BF16)


def _attn_out(o, w_bf16, x2d, mod, g, s):
    t, d = x2d.shape
    tm = _token_tile(s)
    tpb = s // tm
    return pl.pallas_call(
        _attn_out_kernel,
        grid=(t // tm,),
        in_specs=[pl.BlockSpec((tm, d), lambda i: (i, 0)),
                  pl.BlockSpec(w_bf16.shape, lambda i: (0, 0)),
                  pl.BlockSpec((tm, d), lambda i: (i, 0)),
                  pl.BlockSpec((1, 6, d), lambda i: (i // tpb, 0, 0)),
                  pl.BlockSpec((4, d), lambda i: (0, 0))],
        out_specs=[pl.BlockSpec((tm, d), lambda i: (i, 0)), pl.BlockSpec((tm, d), lambda i: (i, 0))],
        out_shape=[jax.ShapeDtypeStruct((t, d), F32), jax.ShapeDtypeStruct((t, d), BF16)],
        compiler_params=_cparams("arbitrary"),
        name="attn_out_proj",
    )(o, w_bf16, x2d, mod, g)


def _swiglu_partial(h, width, wg, wu, wd):
    out = None
    for c0 in range(0, width, FFN_SUBCHUNK):
        c = slice(c0, min(c0 + FFN_SUBCHUNK, width))
        gate = jnp.dot(h, wg(c), preferred_element_type=F32)
        up = jnp.dot(h, wu(c), preferred_element_type=F32)
        part = jnp.dot((_silu(gate) * up).astype(BF16), wd(c), preferred_element_type=F32)
        out = part if out is None else out + part
    return out


def _ffn_kernel(h_ref, wg_ref, wu_ref, wd_ref, x_ref, mod_ref, g_ref, o_ref, acc_ref):
    j = pl.program_id(1)

    @pl.when(j == 0)
    def _():
        acc_ref[...] = jnp.zeros(acc_ref.shape, F32)

    acc_ref[...] += _swiglu_partial(h_ref[...], wg_ref.shape[1], lambda c: wg_ref[:, c], lambda c: wu_ref[:, c],
                                    lambda c: wd_ref[c, :])

    @pl.when(j == pl.num_programs(1) - 1)
    def _():
        o_ref[...] = x_ref[...] + mod_ref[0, 5:6, :] * _rms(acc_ref[...], g_ref[3:4, :])


def _ffn_chunk(f):
    for tf in (1408, 1024, 896, 512, 256, 128):
        if f % tf == 0:
            return tf
    return f


def _dense_ffn(h2, wg, wu, wd, x2d, mod, g, s):
    t, d = x2d.shape
    f = wg.shape[1]
    tm = _token_tile(s)
    tpb = s // tm
    tf = _ffn_chunk(f)
    return pl.pallas_call(
        _ffn_kernel,
        grid=(t // tm, f // tf),
        in_specs=[pl.BlockSpec((tm, d), lambda i, j: (i, 0)),
                  pl.BlockSpec((d, tf), lambda i, j: (0, j)),
                  pl.BlockSpec((d, tf), lambda i, j: (0, j)),
                  pl.BlockSpec((tf, d), lambda i, j: (j, 0)),
                  pl.BlockSpec((tm, d), lambda i, j: (i, 0)),
                  pl.BlockSpec((1, 6, d), lambda i, j: (i // tpb, 0, 0)),
                  pl.BlockSpec((4, d), lambda i, j: (0, 0))],
        out_specs=pl.BlockSpec((tm, d), lambda i, j: (i, 0)),
        out_shape=jax.ShapeDtypeStruct((t, d), F32),
        scratch_shapes=[pltpu.VMEM((tm, d), F32)],
        compiler_params=_cparams("arbitrary", "arbitrary"),
        name="dense_ffn",
    )(h2, wg, wu, wd, x2d, mod, g)


def _mlstm_qkv_kernel(xm_ref, cw_ref, cb_ref, wqk_ref, wkt_ref, wv_ref, wgq_ref, wgk_ref, wgv_ref, bg_ref,
                      xc_ref, q_ref, kt_ref, v_ref, gates_ref, pad_sc, *, ts, k_scale):
    c = pl.program_id(1)
    s, cw = xm_ref.shape[1], xm_ref.shape[2]
    front = SUBLANES
    pad_sc[0:front, :] = jnp.zeros((front, cw), F32)
    pad_sc[front:front + s, :] = xm_ref[0].astype(F32)

    @pl.when(c == 0)
    def _():
        gates_ref[0] = jnp.broadcast_to(bg_ref[...], gates_ref.shape[1:])

    for r in range(s // ts):
        r0 = r * ts
        conv = cb_ref[0]
        for j in range(MLSTM_CONV):
            start = r0 + front - (MLSTM_CONV - 1) + j
            conv = conv + pad_sc[start:start + ts, :] * cw_ref[0, j:j + 1, :]
        xc = _silu(conv)
        xcb = xc.astype(BF16)
        qk = jnp.dot(xcb, wqk_ref[0], preferred_element_type=F32)
        qb = qk[:, :cw].astype(BF16)
        kb = qk[:, cw:].astype(BF16)
        vb = jnp.dot(xm_ref[0, r0:r0 + ts, :], wv_ref[0], preferred_element_type=F32).astype(BF16)
        gates_ref[0, r0:r0 + ts, :] += (jnp.dot(qb, wgq_ref[...], preferred_element_type=F32)
                                        + jnp.dot(kb, wgk_ref[...], preferred_element_type=F32)
                                        + jnp.dot(vb, wgv_ref[...], preferred_element_type=F32))
        xc_ref[0, r0:r0 + ts, :] = xcb
        q_ref[0, r0:r0 + ts, :] = qb
        kt = lax.dot_general(wkt_ref[0], xcb, (((1,), (1,)), ((), ())), preferred_element_type=F32)
        kt_ref[0, :, r0:r0 + ts] = (kt * k_scale).astype(BF16)
        v_ref[0, r0:r0 + ts, :] = vb


def _block_diag(w, cw):
    g, qb, _ = w.shape
    per = cw // qb
    wr = w.reshape(g // per, per, qb, qb)
    eye = jnp.eye(per, dtype=w.dtype)
    return jnp.einsum("cgio,gh->cgiho", wr, eye).reshape(g // per, cw, cw)


def _mlstm_qkv(xm, conv_w, conv_b, w_q, w_k, w_v, w_gate, b_gate, b, s):
    di = xm.shape[1]
    cw = 256
    nchunk = di // cw
    nh = MLSTM_HEADS
    dh = di // nh
    ts = min(256, s)
    wk_bd = _block_diag(w_k, cw)
    wqk = jnp.concatenate([_block_diag(w_q, cw), wk_bd], axis=2).astype(BF16)
    wkt = jnp.swapaxes(wk_bd, 1, 2).astype(BF16)
    wv = _block_diag(w_v, cw).astype(BF16)
    wg = jnp.zeros((3 * di, LANES), F32).at[:, :2 * nh].set(w_gate).astype(BF16)
    bg = jnp.zeros((1, LANES), F32).at[0, :2 * nh].set(b_gate)
    xm3 = xm.reshape(b, s, di)
    blk = pl.BlockSpec((1, s, cw), lambda bi, c: (bi, 0, c))
    outs = pl.pallas_call(
        functools.partial(_mlstm_qkv_kernel, ts=ts, k_scale=dh ** -0.5),
        grid=(b, nchunk),
        in_specs=[blk,
                  pl.BlockSpec((1, MLSTM_CONV, cw), lambda bi, c: (c, 0, 0)),
                  pl.BlockSpec((1, 1, cw), lambda bi, c: (c, 0, 0)),
                  pl.BlockSpec((1, cw, 2 * cw), lambda bi, c: (c, 0, 0)),
                  pl.BlockSpec((1, cw, cw), lambda bi, c: (c, 0, 0)),
                  pl.BlockSpec((1, cw, cw), lambda bi, c: (c, 0, 0)),
                  pl.BlockSpec((cw, LANES), lambda bi, c: (c, 0)),
                  pl.BlockSpec((cw, LANES), lambda bi, c: (nchunk + c, 0)),
                  pl.BlockSpec((cw, LANES), lambda bi, c: (2 * nchunk + c, 0)),
                  pl.BlockSpec((1, LANES), lambda bi, c: (0, 0))],
        out_specs=[blk, blk, pl.BlockSpec((1, cw, s), lambda bi, c: (bi, c, 0)), blk,
                   pl.BlockSpec((1, s, LANES), lambda bi, c: (bi, 0, 0))],
        out_shape=[jax.ShapeDtypeStruct((b, s, di), BF16)] * 2 + [jax.ShapeDtypeStruct((b, di, s), BF16)]
        + [jax.ShapeDtypeStruct((b, s, di), BF16), jax.ShapeDtypeStruct((b, s, LANES), F32)],
        scratch_shapes=[pltpu.VMEM((s + SUBLANES, cw), F32)],
        compiler_params=_cparams("arbitrary", "arbitrary"),
        name="mlstm_qkv",
    )(xm3, conv_w.reshape(MLSTM_CONV, nchunk, cw).transpose(1, 0, 2), conv_b.reshape(nchunk, 1, cw),
      wqk, wkt, wv, wg, wg, wg, bg)
    return outs


def _log_sigmoid(x):
    return jnp.minimum(x, 0.0) - jnp.log1p(jnp.exp(-jnp.abs(x)))


def _split3(x):
    hi = x.astype(BF16)
    r = x - hi.astype(F32)
    mid = r.astype(BF16)
    lo = (r - mid.astype(F32)).astype(BF16)
    return hi, mid, lo


def _mlstm_cell_kernel(q_ref, kt_ref, v_ref, gc_ref, gr_ref, ng_ref, o_ref, ct_sc, n_sc, m_sc, *, nh):
    c = pl.program_id(1)
    L = q_ref.shape[1]
    dh = q_ref.shape[2] // nh

    @pl.when(c == 0)
    def _():
        ct_sc[...] = jnp.zeros(ct_sc.shape, F32)
        n_sc[...] = jnp.zeros(n_sc.shape, F32)
        m_sc[...] = jnp.zeros(m_sc.shape, F32)

    t_idx = lax.broadcasted_iota(jnp.int32, (L, L), 0)
    s_idx = lax.broadcasted_iota(jnp.int32, (L, L), 1)
    causal = s_idx <= t_idx
    tri = causal.astype(BF16)
    tri_t = (t_idx <= s_idx).astype(F32)
    gc = gc_ref[0]
    gr = gr_ref[0]
    b_cols = sum(jnp.dot(tri, part, preferred_element_type=F32) for part in _split3(_log_sigmoid(gc)))
    b_rows = jnp.dot(_log_sigmoid(gr), tri_t, preferred_element_type=F32, precision=HIGHEST)
    lane = lax.broadcasted_iota(jnp.int32, gc.shape, 1)

    for h in range(nh):
        cols = slice(h * dh, (h + 1) * dh)
        i_col = jnp.sum(jnp.where(lane == h, gc, 0.0), axis=1, keepdims=True)
        b_col = jnp.sum(jnp.where(lane == h + nh, b_cols, 0.0), axis=1, keepdims=True)
        i_row = gr[h:h + 1, :]
        b_row = b_rows[nh + h:nh + h + 1, :]
        u_row = i_row - b_row
        u_col = i_col - b_col

        m_prev = m_sc[h, 0:1, 0:1]
        dlog = jnp.where(causal, b_col + u_row, NEG_BIG)
        g = b_col + m_prev
        m_t = jnp.maximum(g, jnp.max(dlog, axis=1, keepdims=True))
        qb = q_ref[0, :, cols]
        ktb = kt_ref[0, cols, :]
        vb = v_ref[0, :, cols]
        w = jnp.exp(dlog - m_t) * jnp.dot(qb, ktb, preferred_element_type=F32)
        inter = jnp.exp(g - m_t)
        ct = ct_sc[h]
        num = (jnp.dot(w.astype(BF16), vb, preferred_element_type=F32)
               + inter * jnp.dot(qb, ct.astype(BF16), preferred_element_type=F32))
        den = (jnp.sum(w, axis=1, keepdims=True)
               + inter * jnp.dot(qb, n_sc[h].astype(BF16), preferred_element_type=F32))
        scale = 1.0 / jnp.maximum(jnp.abs(den), jnp.exp(-m_t))
        hh = num * jnp.tile(scale, (1, dh // LANES))

        mu = jnp.mean(hh, axis=1, keepdims=True)
        cen = hh - mu
        var = jnp.mean(cen * cen, axis=1, keepdims=True)
        o_ref[0, :, cols] = (cen * lax.rsqrt(var + NORM_EPS) * ng_ref[h]).astype(BF16)

        b_last = b_row[:, L - 1:L]
        m_new = jnp.maximum(b_last + m_prev, jnp.max(b_last + u_row, axis=1, keepdims=True))
        decay = jnp.exp(b_last + m_prev - m_new)
        ws_col = jnp.exp(b_last - m_new + u_col).astype(BF16)
        ct_sc[h] = decay * ct + jnp.dot(ktb, vb * ws_col, preferred_element_type=F32)
        n_sc[h] = decay * n_sc[h] + jnp.dot(ktb, jnp.broadcast_to(ws_col, (L, LANES)),
                                            preferred_element_type=F32)
        m_sc[h] = jnp.broadcast_to(m_new, m_sc.shape[1:])


def _mlstm_cell(q, kt, v, gates, norm_g, b, s):
    di = q.shape[2]
    nh = MLSTM_HEADS
    dh = di // nh
    L = min(256, s)
    gates_t = jnp.transpose(gates[:, :, :SUBLANES], (0, 2, 1))
    blk = pl.BlockSpec((1, L, di), lambda bi, c: (bi, c, 0))
    return pl.pallas_call(
        functools.partial(_mlstm_cell_kernel, nh=nh),
        grid=(b, s // L),
        in_specs=[blk,
                  pl.BlockSpec((1, di, L), lambda bi, c: (bi, 0, c)),
                  blk,
                  pl.BlockSpec((1, L, LANES), lambda bi, c: (bi, c, 0)),
                  pl.BlockSpec((1, SUBLANES, L), lambda bi, c: (bi, 0, c)),
                  pl.BlockSpec((nh, 1, dh), lambda bi, c: (0, 0, 0))],
        out_specs=blk,
        out_shape=jax.ShapeDtypeStruct((b, s, di), BF16),
        scratch_shapes=[pltpu.VMEM((nh, dh, dh), F32), pltpu.VMEM((nh, dh, LANES), F32),
                        pltpu.VMEM((nh, SUBLANES, LANES), F32)],
        compiler_params=_cparams("arbitrary", "arbitrary"),
        name="mlstm_cell",
    )(q, kt, v, gates, gates_t, norm_g.reshape(nh, 1, dh))


def _mlstm_out_kernel(hn_ref, xc_ref, z_ref, skip_ref, w_ref, x_ref, mod_ref, g_ref, wr_ref, br_ref,
                      x_out_ref, h_out_ref, route_ref, meta_ref, *, n_experts):
    inner = ((hn_ref[...].astype(F32) + skip_ref[...] * xc_ref[...].astype(F32))
             * _silu(z_ref[...].astype(F32)))
    h4 = _mixer_out_core(inner.astype(BF16), w_ref, x_ref, mod_ref, g_ref, x_out_ref)
    h_out_ref[...] = h4.astype(BF16)

    tm = h4.shape[0]
    lane = lax.broadcasted_iota(jnp.int32, (tm, LANES), 1)
    logits = jnp.dot(h4, wr_ref[...], preferred_element_type=F32, precision=HIGHEST) + br_ref[...]
    logits = jnp.where(lane < n_experts, logits, NEG_BIG)
    ex = jnp.exp(logits - jnp.max(logits, axis=1, keepdims=True))
    probs = ex / jnp.sum(ex, axis=1, keepdims=True)
    probs = jnp.where(lane < n_experts, probs, -1.0)
    lane_f = lane.astype(F32)
    p0 = jnp.max(probs, axis=1, keepdims=True)
    e0 = jnp.min(jnp.where(probs == p0, lane_f, float(LANES)), axis=1, keepdims=True)
    rest = jnp.where(lane_f == e0, -1.0, probs)
    p1 = jnp.max(rest, axis=1, keepdims=True)
    e1 = jnp.min(jnp.where(rest == p1, lane_f, float(LANES)), axis=1, keepdims=True)
    tot = p0 + p1
    sel0 = lane_f == e0
    sel1 = lane_f == e1
    sel = jnp.where(sel0 | sel1, 1.0, 0.0)
    r_idx = lax.broadcasted_iota(jnp.int32, (tm, tm), 0)
    c_idx = lax.broadcasted_iota(jnp.int32, (tm, tm), 1)
    before = (c_idx < r_idx).astype(BF16)
    cum = jnp.dot(before, sel.astype(BF16), preferred_element_type=F32)
    run8 = jnp.floor((jnp.sum(sel, axis=0, keepdims=True) + 7.0) * 0.125)
    e_r = lax.broadcasted_iota(jnp.int32, (LANES, LANES), 0)
    e_c = lax.broadcasted_iota(jnp.int32, (LANES, LANES), 1)
    seg8 = jnp.dot(jnp.broadcast_to(run8, (SUBLANES, LANES)).astype(BF16), (e_r < e_c).astype(BF16),
                   preferred_element_type=F32)[0:1, :]
    pos = cum + 8.0 * seg8
    pos0 = jnp.sum(jnp.where(sel0, pos, 0.0), axis=1, keepdims=True)
    pos1 = jnp.sum(jnp.where(sel1, pos, 0.0), axis=1, keepdims=True)
    vals = (p0 / tot, p1 / tot, e0, e1, pos0, pos1)
    route = jnp.zeros((tm, LANES), F32)
    for idx, val in enumerate(vals):
        route = jnp.where(lane == idx, val, route)
    route_ref[...] = route
    meta = jnp.where(lane[0:1, :] < n_experts, run8, 0.0)
    meta = jnp.where((lane[0:1, :] >= n_experts) & (lane[0:1, :] < 2 * n_experts),
                     pltpu.roll(8.0 * seg8, n_experts, 1), meta)
    meta_ref[0] = jnp.broadcast_to(meta, meta_ref.shape[1:]).astype(jnp.int32)


def _mlstm_out(hn, xc, z, skip, w_bf16, x2d, mod, g, w_router, b_router, s):
    t, d = x2d.shape
    di = hn.shape[1]
    e = w_router.shape[1]
    tm = _token_tile(s)
    tpb = s // tm
    n_tok_tiles = t // tm
    wr = jnp.zeros((d, LANES), F32).at[:, :e].set(w_router)
    br = jnp.zeros((1, LANES), F32).at[0, :e].set(b_router)
    tok = lambda n: pl.BlockSpec((tm, n), lambda i: (i, 0))
    return pl.pallas_call(
        functools.partial(_mlstm_out_kernel, n_experts=e),
        grid=(n_tok_tiles,),
        in_specs=[tok(di), tok(di), tok(di),
                  pl.BlockSpec((1, di), lambda i: (0, 0)),
                  pl.BlockSpec(w_bf16.shape, lambda i: (0, 0)),
                  tok(d),
                  pl.BlockSpec((1, 6, d), lambda i: (i // tpb, 0, 0)),
                  pl.BlockSpec((4, d), lambda i: (0, 0)),
                  pl.BlockSpec((d, LANES), lambda i: (0, 0)),
                  pl.BlockSpec((1, LANES), lambda i: (0, 0))],
        out_specs=[tok(d), tok(d), tok(LANES), pl.BlockSpec((1, SUBLANES, LANES), lambda i: (i, 0, 0))],
        out_shape=[jax.ShapeDtypeStruct((t, d), F32), jax.ShapeDtypeStruct((t, d), BF16),
                   jax.ShapeDtypeStruct((t, LANES), F32),
                   jax.ShapeDtypeStruct((n_tok_tiles, SUBLANES, LANES), jnp.int32)],
        compiler_params=_cparams("arbitrary"),
        name="mlstm_out_router",
    )(hn, xc, z, skip.reshape(1, di), w_bf16, x2d, mod, g, wr, br)


def _dispatch_kernel(start_ref, run_ref, seg_ref, fill_ref, h_ref, route_ref, xs_ref, stage_sc, inflight_sm, sem,
                     *, n_experts):
    i = pl.program_id(0)
    tm = h_ref.shape[0]
    r_stage = stage_sc.shape[0] - DISPATCH_ROWS

    def copy(src_row, dst_row, rows):
        return pltpu.make_async_copy(stage_sc.at[pl.ds(src_row, rows)], xs_ref.at[pl.ds(dst_row, rows)], sem)

    def wait_copies(n, rows):
        def body(_, carry):
            copy(0, 0, rows).wait()
            return carry
        lax.fori_loop(0, n, body, 0)

    def zero_fill(first_row, n, rows):
        def body(j, carry):
            copy(0, pl.multiple_of(first_row + j * rows, SUBLANES), rows).start()
            return carry
        lax.fori_loop(0, n, body, 0)
        wait_copies(n, rows)

    @pl.when(i == 0)
    def _():
        inflight_sm[0] = 0
        stage_sc[r_stage:, :] = jnp.zeros((DISPATCH_ROWS, stage_sc.shape[1]), F32)

    route_t = jnp.transpose(route_ref[...])
    slot_row = lax.broadcasted_iota(jnp.int32, (r_stage, tm), 0).astype(F32)
    onehot = jnp.where((route_t[4:5, :] == slot_row) | (route_t[5:6, :] == slot_row), 1.0, 0.0).astype(BF16)
    sorted_rows = jnp.dot(onehot, h_ref[...], preferred_element_type=F32)

    wait_copies(inflight_sm[0], DISPATCH_ROWS)
    stage_sc[0:r_stage, :] = sorted_rows
    issued = 0
    for e in range(n_experts):
        rows = run_ref[i * n_experts + e] * SUBLANES
        start = start_ref[i * n_experts + e]
        seg = seg_ref[i * n_experts + e]
        for k in range(tm // DISPATCH_ROWS):
            @pl.when(rows > k * DISPATCH_ROWS)
            def _(seg=seg, start=start, k=k):
                copy(pl.multiple_of(seg + k * DISPATCH_ROWS, SUBLANES),
                     pl.multiple_of(start + k * DISPATCH_ROWS, SUBLANES), DISPATCH_ROWS).start()
        issued = issued + (rows + DISPATCH_ROWS - 1) // DISPATCH_ROWS
    inflight_sm[0] = issued

    @pl.when(i == pl.num_programs(0) - 1)
    def _():
        wait_copies(inflight_sm[0], DISPATCH_ROWS)
        stage_sc[0:DISPATCH_ROWS, :] = jnp.zeros((DISPATCH_ROWS, stage_sc.shape[1]), F32)
        for e in range(n_experts):
            zero_fill(fill_ref[e], fill_ref[n_experts + e], SUBLANES)
        zero_fill(fill_ref[2 * n_experts], fill_ref[2 * n_experts + 1], DISPATCH_ROWS)


def _dispatch(h4, route, starts, runs, segs, fill, n_rows, s, n_experts):
    t, d = h4.shape
    tm = _token_tile(s)
    r_stage = TOP_K * tm + SUBLANES * n_experts
    grid_spec = pltpu.PrefetchScalarGridSpec(
        num_scalar_prefetch=4,
        grid=(t // tm,),
        in_specs=[pl.BlockSpec((tm, d), lambda i, *_: (i, 0)),
                  pl.BlockSpec((tm, LANES), lambda i, *_: (i, 0))],
        out_specs=pl.BlockSpec(memory_space=pl.ANY),
        scratch_shapes=[pltpu.VMEM((r_stage + DISPATCH_ROWS, d), F32), pltpu.SMEM((1,), jnp.int32),
                        pltpu.SemaphoreType.DMA(())],
    )
    return pl.pallas_call(
        functools.partial(_dispatch_kernel, n_experts=n_experts),
        grid_spec=grid_spec,
        out_shape=jax.ShapeDtypeStruct((n_rows, d), F32),
        compiler_params=_cparams("arbitrary"),
        name="moe_dispatch",
    )(starts, runs, segs, fill, h4, route)


def _expert_ffn_kernel(te_ref, nreal_ref, xs_ref, wg_ref, wu_ref, wd_ref, ys_ref, xb_sc, acc_sc):
    i = pl.program_id(0)
    j = pl.program_id(1)
    last = pl.num_programs(1) - 1
    real = i < nreal_ref[0]

    @pl.when(real & (j == 0))
    def _():
        xb_sc[...] = xs_ref[...].astype(BF16)
        acc_sc[...] = jnp.zeros(acc_sc.shape, F32)

    @pl.when(real)
    def _():
        acc_sc[...] += _swiglu_partial(xb_sc[...], wg_ref.shape[2], lambda c: wg_ref[0, :, c],
                                       lambda c: wu_ref[0, :, c], lambda c: wd_ref[0, c, :])

    @pl.when(real & (j == last))
    def _():
        ys_ref[...] = acc_sc[...]

    @pl.when(jnp.logical_not(real) & (j == last))
    def _():
        ys_ref[...] = jnp.zeros(ys_ref.shape, F32)


def _expert_ffn(xs, tile_expert, n_real, wg, wu, wd):
    p, d = xs.shape
    f = wg.shape[2]
    tm = EXPERT_TILE
    tf = _ffn_chunk(f)
    nj = f // tf

    def row_idx(i, j, te, nr):
        return (jnp.minimum(i, nr[0] - 1), 0)

    def col_j(i, j, nr):
        return jnp.where(i < nr[0], j, nj - 1)

    grid_spec = pltpu.PrefetchScalarGridSpec(
        num_scalar_prefetch=2,
        grid=(p // tm, nj),
        in_specs=[pl.BlockSpec((tm, d), row_idx),
                  pl.BlockSpec((1, d, tf), lambda i, j, te, nr: (te[i], 0, col_j(i, j, nr))),
                  pl.BlockSpec((1, d, tf), lambda i, j, te, nr: (te[i], 0, col_j(i, j, nr))),
                  pl.BlockSpec((1, tf, d), lambda i, j, te, nr: (te[i], col_j(i, j, nr), 0))],
        out_specs=pl.BlockSpec((tm, d), lambda i, j, te, nr: (i, 0)),
        scratch_shapes=[pltpu.VMEM((tm, d), BF16), pltpu.VMEM((tm, d), F32)],
    )
    return pl.pallas_call(
        _expert_ffn_kernel,
        grid_spec=grid_spec,
        out_shape=jax.ShapeDtypeStruct((p, d), F32),
        compiler_params=_cparams("arbitrary", "arbitrary"),
        name="moe_expert_ffn",
    )(tile_expert, n_real, xs, wg, wu, wd)


def _combine_kernel(start_ref, run_ref, seg_ref, ys_ref, route_ref, x_ref, mod_ref, g_ref, o_ref, buf_sc, y_sc,
                    col_sc, src_sm, exp_sm, off_sm, count_sm, sems, *, n_experts):
    i = pl.program_id(0)
    tm = x_ref.shape[0]
    d = x_ref.shape[1]
    max_chunks = buf_sc.shape[1]

    def chunk_copy(slot, ci):
        return pltpu.make_async_copy(
            ys_ref.at[pl.ds(pl.multiple_of(src_sm[slot * max_chunks + ci], SUBLANES), DISPATCH_ROWS)],
            buf_sc.at[slot, ci], sems.at[slot, ci])

    def fetch(tile, slot):
        n_chunks = 0
        for e in range(n_experts):
            start = start_ref[tile * n_experts + e]
            rows = run_ref[tile * n_experts + e] * SUBLANES
            seg = seg_ref[tile * n_experts + e]
            for k in range(tm // DISPATCH_ROWS):
                @pl.when(rows > k * DISPATCH_ROWS)
                def _(start=start, seg=seg, k=k, ci=n_chunks + k):
                    src_sm[slot * max_chunks + ci] = start + k * DISPATCH_ROWS
                    exp_sm[slot * max_chunks + ci] = e
                    off_sm[slot * max_chunks + ci] = seg + k * DISPATCH_ROWS
                    chunk_copy(slot, ci).start()
            n_chunks = n_chunks + (rows + DISPATCH_ROWS - 1) // DISPATCH_ROWS
        count_sm[slot] = n_chunks

    slot = i % 2

    @pl.when(i == 0)
    def _():
        fetch(i, slot)

    @pl.when(i + 1 < pl.num_programs(0))
    def _():
        fetch(i + 1, 1 - slot)

    route = route_ref[...]
    for idx in range(col_sc.shape[0]):
        col_sc[idx] = jnp.broadcast_to(route[:, idx:idx + 1], col_sc.shape[1:])
    lane = lax.broadcasted_iota(jnp.int32, (tm, DISPATCH_ROWS), 1).astype(F32)
    y_sc[...] = jnp.zeros(y_sc.shape, F32)

    def body(ci, carry):
        chunk_copy(slot, ci).wait()
        ef = exp_sm[slot * max_chunks + ci].astype(F32)
        target = lane + off_sm[slot * max_chunks + ci].astype(F32)
        first = col_sc[2] == ef
        spread = jnp.where((first & (col_sc[4] == target)) | ((col_sc[3] == ef) & (col_sc[5] == target)),
                           1.0, 0.0).astype(BF16)
        rows_out = jnp.dot(spread, buf_sc[slot, ci].astype(BF16), preferred_element_type=F32)
        weight = jnp.where(first, col_sc[0], col_sc[1])
        y_sc[...] += jnp.tile(weight, (1, d // LANES)) * rows_out
        return carry

    lax.fori_loop(0, count_sm[slot], body, 0)
    o_ref[...] = x_ref[...] + mod_ref[0, 5:6, :] * _rms(y_sc[...], g_ref[3:4, :])


def _combine(ys, starts, runs, segs, route, x2d, mod, g, s, n_experts):
    t, d = x2d.shape
    tm = _token_tile(s)
    tpb = s // tm
    max_chunks = TOP_K * tm // DISPATCH_ROWS + n_experts
    grid_spec = pltpu.PrefetchScalarGridSpec(
        num_scalar_prefetch=3,
        grid=(t // tm,),
        in_specs=[pl.BlockSpec(memory_space=pl.ANY),
                  pl.BlockSpec((tm, LANES), lambda i, *_: (i, 0)),
                  pl.BlockSpec((tm, d), lambda i, *_: (i, 0)),
                  pl.BlockSpec((1, 6, d), lambda i, *_: (i // tpb, 0, 0)),
                  pl.BlockSpec((4, d), lambda i, *_: (0, 0))],
        out_specs=pl.BlockSpec((tm, d), lambda i, *_: (i, 0)),
        scratch_shapes=[pltpu.VMEM((2, max_chunks, DISPATCH_ROWS, d), F32), pltpu.VMEM((tm, d), F32),
                        pltpu.VMEM((6, tm, LANES), F32),
                        pltpu.SMEM((2 * max_chunks,), jnp.int32), pltpu.SMEM((2 * max_chunks,), jnp.int32),
                        pltpu.SMEM((2 * max_chunks,), jnp.int32), pltpu.SMEM((2,), jnp.int32),
                        pltpu.SemaphoreType.DMA((2, max_chunks))],
    )
    return pl.pallas_call(
        functools.partial(_combine_kernel, n_experts=n_experts),
        grid_spec=grid_spec,
        out_shape=jax.ShapeDtypeStruct((t, d), F32),
        compiler_params=_cparams("arbitrary"),
        name="moe_combine",
    )(starts, runs, segs, ys, route, x2d, mod, g)


def _moe(h4, route, meta, x2d, mod, g, wg, wu, wd, s):
    t, d = x2d.shape
    e = wg.shape[0]
    tm = EXPERT_TILE
    n_tok_tiles = t // _token_tile(s)
    n_tiles = (TOP_K * t + (SUBLANES * n_tok_tiles + DISPATCH_ROWS) * e) // tm + e
    runs = meta[:, 0, :e]
    segs = meta[:, 0, e:2 * e]
    used = SUBLANES * jnp.sum(runs, axis=0)
    tiles_e = jnp.where(used > 0, (used + DISPATCH_ROWS + tm - 1) // tm, 0)
    ends = jnp.cumsum(tiles_e)
    off = (ends - tiles_e) * tm
    starts = off[None, :] + SUBLANES * (jnp.cumsum(runs, axis=0) - runs)
    n_real = ends[e - 1:e].astype(jnp.int32)
    fill = jnp.concatenate([off + used, (tiles_e * tm - used) // SUBLANES,
                            n_real * tm, (n_tiles - n_real) * (tm // DISPATCH_ROWS)]).astype(jnp.int32)
    tile_ids = jnp.arange(n_tiles, dtype=jnp.int32)
    tile_expert = jnp.minimum(jnp.sum(tile_ids[:, None] >= ends[None, :], axis=1), e - 1)
    last_expert = jnp.minimum(jnp.sum(n_real[0] - 1 >= ends), e - 1)
    tile_expert = jnp.where(tile_ids < n_real[0], tile_expert, last_expert).astype(jnp.int32)
    flat = lambda a: a.reshape(-1).astype(jnp.int32)
    xs = _dispatch(h4, route, flat(starts), flat(runs), flat(segs), fill, n_tiles * tm, s, e)
    ys = _expert_ffn(xs, tile_expert, n_real, wg, wu, wd)
    return _combine(ys, flat(starts), flat(runs), flat(segs), route, x2d, mod, g, s, e)


def kernel(x, c, mod_w, mod_b, norm_g, attn_w_in, attn_w_out, attn_lambda, attn_subln, ffn_w_gate, ffn_w_up, ffn_w_down, mlstm_w_in, mlstm_conv_w, mlstm_conv_b, mlstm_w_q, mlstm_w_k, mlstm_w_v, mlstm_w_gate, mlstm_b_gate, mlstm_skip, mlstm_norm, mlstm_w_out, moe_w_router, moe_b_router, moe_w_gate, moe_w_up, moe_w_down):
    b, s, d = x.shape
    depth = mod_w.shape[0]
    mod_all = _modulation(c, mod_w, mod_b)
    x2d = x.reshape(b * s, d)
    da = d // (2 * ATTN_HEADS)
    for i in range(depth):
        mod = mod_all[i]
        g = norm_g[i]
        j = i // N_MIXERS
        if i % N_MIXERS == 0:
            lambda_init = 0.8 - 0.6 * math.exp(-0.3 * i)
            q, k, v = _in_proj(x2d, mod, g, attn_w_in[j].astype(BF16), 3, (LOG2E * da ** -0.5, 1.0, 1.0), s,
                               "attn_in_proj")
            o = _diff_attention(q, k, v, attn_lambda[j], attn_subln[j], lambda_init, b, s)
            x2d, h2 = _attn_out(o, attn_w_out[j].astype(BF16), x2d, mod, g, s)
            x2d = _dense_ffn(h2, ffn_w_gate[j].astype(BF16), ffn_w_up[j].astype(BF16),
                             ffn_w_down[j].astype(BF16), x2d, mod, g, s)
        else:
            xm, z = _in_proj(x2d, mod, g, mlstm_w_in[j].astype(BF16), 2, (1.0, 1.0), s, "mlstm_in_proj")
            xc, q, kt, v, gates = _mlstm_qkv(xm, mlstm_conv_w[j], mlstm_conv_b[j], mlstm_w_q[j], mlstm_w_k[j],
                                             mlstm_w_v[j], mlstm_w_gate[j], mlstm_b_gate[j], b, s)
            hn = _mlstm_cell(q, kt, v, gates, mlstm_norm[j], b, s)
            di = hn.shape[2]
            x2d, h4, route, meta = _mlstm_out(hn.reshape(b * s, di), xc.reshape(b * s, di), z, mlstm_skip[j],
                                              mlstm_w_out[j].astype(BF16), x2d, mod, g,
                                              moe_w_router[j], moe_b_router[j], s)
            x2d = _moe(h4, route, meta, x2d, mod, g, moe_w_gate[j].astype(BF16), moe_w_up[j].astype(BF16),
                       moe_w_down[j].astype(BF16), s)
    return x2d.reshape(b, s, d)
```

```python
import functools
import math

import jax
import jax.numpy as jnp
from jax import lax
from jax.experimental import pallas as pl
from jax.experimental.pallas import tpu as pltpu

F32 = jnp.float32
BF16 = jnp.bfloat16
HIGHEST = lax.Precision.HIGHEST

CHUNK = 64
ATTN_HEADS = 8
ALIBI_MAX_BIAS = 8.0
MLSTM_HEADS = 4
MLSTM_HEADS_PER_STEP = 4
MLSTM_CONV = 4
QKV_BLOCK = 4
TOP_K = 2
NORM_EPS = 1e-6
N_MIXERS = 2

LANES = 128
SUBLANES = 8
VMEM_LIMIT_BYTES = 56 * 1024 * 1024
FFN_SUBCHUNK = 256
EXPERT_TILE = 512
DISPATCH_ROWS = 128
NEG_BIG = -1e30
LOG2E = math.log2(math.e)


def _cparams(*sem):
    return pltpu.CompilerParams(dimension_semantics=sem, vmem_limit_bytes=VMEM_LIMIT_BYTES)


def _rms(x, g):
    return x * lax.rsqrt(jnp.mean(x * x, axis=-1, keepdims=True) + NORM_EPS) * g


def _silu(x):
    return x * jax.nn.sigmoid(x)


def _token_tile(s):
    return min(512, s)


def _mod_kernel(c_ref, w_ref, b_ref, o_ref):
    cond = _silu(c_ref[...])
    o_ref[0] = jnp.dot(cond, w_ref[0], preferred_element_type=F32, precision=HIGHEST) + b_ref[0]


def _modulation(c, mod_w, mod_b):
    depth, d, n = mod_w.shape
    b = c.shape[0]
    tn = 1536 if n % 1536 == 0 else n
    out = pl.pallas_call(
        _mod_kernel,
        grid=(depth, n // tn),
        in_specs=[pl.BlockSpec((b, d), lambda i, j: (0, 0)),
                  pl.BlockSpec((1, d, tn), lambda i, j: (i, 0, j)),
                  pl.BlockSpec((1, 1, tn), lambda i, j: (i, 0, j))],
        out_specs=pl.BlockSpec((1, b, tn), lambda i, j: (i, 0, j)),
        out_shape=jax.ShapeDtypeStruct((depth, b, n), F32),
        compiler_params=_cparams("arbitrary", "arbitrary"),
        name="modulation",
    )(c, mod_w, mod_b.reshape(depth, 1, n))
    return out.reshape(depth, b, 6, d)


def _in_proj_kernel(x_ref, mod_ref, g_ref, w_ref, *o_refs, scales):
    x = x_ref[...]
    h = _rms(x, g_ref[0:1, :]) * (1.0 + mod_ref[0, 1:2, :]) + mod_ref[0, 0:1, :]
    hb = h.astype(BF16)
    n = o_refs[0].shape[1]
    for idx, o_ref in enumerate(o_refs):
        r = jnp.dot(hb, w_ref[:, idx * n:(idx + 1) * n], preferred_element_type=F32)
        if scales[idx] != 1.0:
            r = r * scales[idx]
        o_ref[...] = r.astype(BF16)


def _in_proj(x2d, mod, g, w_bf16, n_out, scales, s, name):
    t, d = x2d.shape
    tm = _token_tile(s)
    tpb = s // tm
    n = w_bf16.shape[1] // n_out
    return pl.pallas_call(
        functools.partial(_in_proj_kernel, scales=scales),
        grid=(t // tm,),
        in_specs=[pl.BlockSpec((tm, d), lambda i: (i, 0)),
                  pl.BlockSpec((1, 6, d), lambda i: (i // tpb, 0, 0)),
                  pl.BlockSpec((4, d), lambda i: (0, 0)),
                  pl.BlockSpec(w_bf16.shape, lambda i: (0, 0))],
        out_specs=[pl.BlockSpec((tm, n), lambda i: (i, 0))] * n_out,
        out_shape=[jax.ShapeDtypeStruct((t, n), BF16)] * n_out,
        compiler_params=_cparams("arbitrary"),
        name=name,
    )(x2d, mod, g, w_bf16)


def _attn_kernel(slopes_ref, lam_ref, subln_ref, q_ref, k_ref, v_ref, o_ref, m_sc, acc_sc, bias_sc, vext_sc,
                 s_sc, smax_sc, *, tq, tk, lambda_init):
    slope = slopes_ref[pl.program_id(1)]
    hd = q_ref.shape[2]
    half = hd // 2
    n_q = q_ref.shape[1] // tq

    vext_sc[:, :hd] = v_ref[0]
    vext_sc[:, hd:] = jnp.ones((vext_sc.shape[0], hd), BF16)
    row = lax.broadcasted_iota(jnp.int32, (tq, tk), 0)
    col = lax.broadcasted_iota(jnp.int32, (tq, tk), 1)
    dist = (row - col).astype(F32)
    bias_sc[0] = -slope * dist
    bias_sc[1] = jnp.where(col // CHUNK <= row // CHUNK, -slope * jnp.abs(dist), NEG_BIG)

    lam = lam_ref[...]
    lam_full = (jnp.exp(jnp.sum(lam[0:1, :] * lam[1:2, :], axis=1, keepdims=True))
                - jnp.exp(jnp.sum(lam[2:3, :] * lam[3:4, :], axis=1, keepdims=True)) + lambda_init)

    def scores(qi, j, slot):
        q = q_ref[0, qi * tq:(qi + 1) * tq, :]
        lane = lax.broadcasted_iota(jnp.int32, q.shape, 1)
        zero = jnp.zeros_like(q)
        q2 = jnp.concatenate([jnp.where(lane < half, q, zero), jnp.where(lane >= half, q, zero)], axis=0)
        kj = k_ref[0, j * tk:(j + 1) * tk, :]
        s = lax.dot_general(q2, kj, (((1,), (1,)), ((), ())), preferred_element_type=F32)
        bias = bias_sc[1 if j == qi else 0]
        s = s + jnp.concatenate([bias, bias], axis=0)
        s_sc[slot] = s
        smax_sc[slot] = jnp.broadcast_to(jnp.max(s, axis=1, keepdims=True), smax_sc.shape[1:])

    def accumulate(qi, j, slot):
        const = 0.0 if j == qi else -slope * float((qi - j) * tq)
        m_prev = m_sc[...]
        m_next = jnp.maximum(m_prev, smax_sc[slot] + const)
        alpha = jnp.exp2(m_prev - m_next)
        p = jnp.exp2(s_sc[slot] - jnp.tile(m_next - const, (1, tk // LANES)))
        pv = jnp.dot(p.astype(BF16), vext_sc[j * tk:(j + 1) * tk, :], preferred_element_type=F32)
        acc_sc[...] = jnp.tile(alpha, (1, 2)) * acc_sc[...] + pv
        m_sc[...] = m_next

    def finalize(qi):
        acc = acc_sc[...]
        o_all = acc[:, :hd] / acc[:, hd:]
        o = o_all[:tq] - lam_full * o_all[tq:]
        o = _rms(o, subln_ref[...]) * (1.0 - lambda_init)
        o_ref[0, qi * tq:(qi + 1) * tq, :] = o.astype(BF16)

    blocks = [(qi, j) for qi in range(n_q) for j in range(qi + 1)]
    scores(*blocks[0], 0)
    for n, (qi, j) in enumerate(blocks):
        if n + 1 < len(blocks):
            scores(*blocks[n + 1], (n + 1) % 2)
        if j == 0:
            m_sc[...] = jnp.full(m_sc.shape, NEG_BIG, F32)
            acc_sc[...] = jnp.zeros(acc_sc.shape, F32)
        accumulate(qi, j, n % 2)
        if j == qi:
            finalize(qi)


def _diff_attention(q, k, v, lam, subln, lambda_init, b, s):
    d = q.shape[1]
    hd = d // ATTN_HEADS
    tq = tk = min(512, s)
    slopes = LOG2E * jnp.exp2(-ALIBI_MAX_BIAS * jnp.arange(1, ATTN_HEADS + 1, dtype=F32) / ATTN_HEADS)
    q3, k3, v3 = (a.reshape(b, s, d) for a in (q, k, v))
    grid_spec = pltpu.PrefetchScalarGridSpec(
        num_scalar_prefetch=1,
        grid=(b, ATTN_HEADS),
        in_specs=[pl.BlockSpec(lam.shape, lambda bi, h, sl: (0, 0)),
                  pl.BlockSpec((1, hd), lambda bi, h, sl: (0, 0)),
                  pl.BlockSpec((1, s, hd), lambda bi, h, sl: (bi, 0, h)),
                  pl.BlockSpec((1, s, hd), lambda bi, h, sl: (bi, 0, h)),
                  pl.BlockSpec((1, s, hd), lambda bi, h, sl: (bi, 0, h))],
        out_specs=pl.BlockSpec((1, s, hd), lambda bi, h, sl: (bi, 0, h)),
        scratch_shapes=[pltpu.VMEM((2 * tq, LANES), F32), pltpu.VMEM((2 * tq, 2 * hd), F32),
                        pltpu.VMEM((2, tq, tk), F32), pltpu.VMEM((s, 2 * hd), BF16),
                        pltpu.VMEM((2, 2 * tq, tk), F32), pltpu.VMEM((2, 2 * tq, LANES), F32)],
    )
    o = pl.pallas_call(
        functools.partial(_attn_kernel, tq=tq, tk=tk, lambda_init=lambda_init),
        grid_spec=grid_spec,
        out_shape=jax.ShapeDtypeStruct((b, s, d), BF16),
        compiler_params=_cparams("arbitrary", "arbitrary"),
        name="diff_attention",
    )(slopes, lam, subln.reshape(1, hd), q3, k3, v3)
    return o.reshape(b * s, d)


def _mixer_out_core(inp_bf16, w_ref, x_ref, mod_ref, g_ref, x_out_ref):
    y = jnp.dot(inp_bf16, w_ref[...], preferred_element_type=F32)
    x1 = x_ref[...] + mod_ref[0, 2:3, :] * _rms(y, g_ref[1:2, :])
    x_out_ref[...] = x1
    return _rms(x1, g_ref[2:3, :]) * (1.0 + mod_ref[0, 4:5, :]) + mod_ref[0, 3:4, :]


def _attn_out_kernel(o_ref, w_ref, x_ref, mod_ref, g_ref, x_out_ref, h_out_ref):
    h2 = _mixer_out_core(o_ref[...], w_ref, x_ref, mod_ref, g_ref, x_out_ref)
    h_out_ref[...] = h2.astype(BF16)


def _attn_out(o, w_bf16, x2d, mod, g, s):
    t, d = x2d.shape
    tm = _token_tile(s)
    tpb = s // tm
    return pl.pallas_call(
        _attn_out_kernel,
        grid=(t // tm,),
        in_specs=[pl.BlockSpec((tm, d), lambda i: (i, 0)),
                  pl.BlockSpec(w_bf16.shape, lambda i: (0, 0)),
                  pl.BlockSpec((tm, d), lambda i: (i, 0)),
                  pl.BlockSpec((1, 6, d), lambda i: (i // tpb, 0, 0)),
                  pl.BlockSpec((4, d), lambda i: (0, 0))],
        out_specs=[pl.BlockSpec((tm, d), lambda i: (i, 0)), pl.BlockSpec((tm, d), lambda i: (i, 0))],
        out_shape=[jax.ShapeDtypeStruct((t, d), F32), jax.ShapeDtypeStruct((t, d), BF16)],
        compiler_params=_cparams("arbitrary"),
        name="attn_out_proj",
    )(o, w_bf16, x2d, mod, g)


def _swiglu_partial(h, width, wg, wu, wd):
    out = None
    for c0 in range(0, width, FFN_SUBCHUNK):
        c = slice(c0, min(c0 + FFN_SUBCHUNK, width))
        gate = jnp.dot(h, wg(c), preferred_element_type=F32)
        up = jnp.dot(h, wu(c), preferred_element_type=F32)
        part = jnp.dot((_silu(gate) * up).astype(BF16), wd(c), preferred_element_type=F32)
        out = part if out is None else out + part
    return out


def _ffn_kernel(h_ref, wg_ref, wu_ref, wd_ref, x_ref, mod_ref, g_ref, o_ref, acc_ref):
    j = pl.program_id(1)

    @pl.when(j == 0)
    def _():
        acc_ref[...] = jnp.zeros(acc_ref.shape, F32)

    acc_ref[...] += _swiglu_partial(h_ref[...], wg_ref.shape[1], lambda c: wg_ref[:, c], lambda c: wu_ref[:, c],
                                    lambda c: wd_ref[c, :])

    @pl.when(j == pl.num_programs(1) - 1)
    def _():
        o_ref[...] = x_ref[...] + mod_ref[0, 5:6, :] * _rms(acc_ref[...], g_ref[3:4, :])


def _ffn_chunk(f):
    for tf in (1408, 1024, 896, 512, 256, 128):
        if f % tf == 0:
            return tf
    return f


def _dense_ffn(h2, wg, wu, wd, x2d, mod, g, s):
    t, d = x2d.shape
    f = wg.shape[1]
    tm = _token_tile(s)
    tpb = s // tm
    tf = _ffn_chunk(f)
    return pl.pallas_call(
        _ffn_kernel,
        grid=(t // tm, f // tf),
        in_specs=[pl.BlockSpec((tm, d), lambda i, j: (i, 0)),
                  pl.BlockSpec((d, tf), lambda i, j: (0, j)),
                  pl.BlockSpec((d, tf), lambda i, j: (0, j)),
                  pl.BlockSpec((tf, d), lambda i, j: (j, 0)),
                  pl.BlockSpec((tm, d), lambda i, j: (i, 0)),
                  pl.BlockSpec((1, 6, d), lambda i, j: (i // tpb, 0, 0)),
                  pl.BlockSpec((4, d), lambda i, j: (0, 0))],
        out_specs=pl.BlockSpec((tm, d), lambda i, j: (i, 0)),
        out_shape=jax.ShapeDtypeStruct((t, d), F32),
        scratch_shapes=[pltpu.VMEM((tm, d), F32)],
        compiler_params=_cparams("arbitrary", "arbitrary"),
        name="dense_ffn",
    )(h2, wg, wu, wd, x2d, mod, g)


def _mlstm_qkv_kernel(xm_ref, cw_ref, cb_ref, wqk_ref, wkt_ref, wv_ref, wgq_ref, wgk_ref, wgv_ref, bg_ref,
                      xc_ref, q_ref, kt_ref, v_ref, gates_ref, pad_sc, *, ts, k_scale):
    c = pl.program_id(1)
    s, cw = xm_ref.shape[1], xm_ref.shape[2]
    front = SUBLANES
    pad_sc[0:front, :] = jnp.zeros((front, cw), F32)
    pad_sc[front:front + s, :] = xm_ref[0].astype(F32)

    @pl.when(c == 0)
    def _():
        gates_ref[0] = jnp.broadcast_to(bg_ref[...], gates_ref.shape[1:])

    for r in range(s // ts):
        r0 = r * ts
        conv = cb_ref[0]
        for j in range(MLSTM_CONV):
            start = r0 + front - (MLSTM_CONV - 1) + j
            conv = conv + pad_sc[start:start + ts, :] * cw_ref[0, j:j + 1, :]
        xc = _silu(conv)
        xcb = xc.astype(BF16)
        qk = jnp.dot(xcb, wqk_ref[0], preferred_element_type=F32)
        qb = qk[:, :cw].astype(BF16)
        kb = qk[:, cw:].astype(BF16)
        vb = jnp.dot(xm_ref[0, r0:r0 + ts, :], wv_ref[0], preferred_element_type=F32).astype(BF16)
        gates_ref[0, r0:r0 + ts, :] += (jnp.dot(qb, wgq_ref[...], preferred_element_type=F32)
                                        + jnp.dot(kb, wgk_ref[...], preferred_element_type=F32)
                                        + jnp.dot(vb, wgv_ref[...], preferred_element_type=F32))
        xc_ref[0, r0:r0 + ts, :] = xcb
        q_ref[0, r0:r0 + ts, :] = qb
        kt = lax.dot_general(wkt_ref[0], xcb, (((1,), (1,)), ((), ())), preferred_element_type=F32)
        kt_ref[0, :, r0:r0 + ts] = (kt * k_scale).astype(BF16)
        v_ref[0, r0:r0 + ts, :] = vb


def _block_diag(w, cw):
    g, qb, _ = w.shape
    per = cw // qb
    wr = w.reshape(g // per, per, qb, qb)
    eye = jnp.eye(per, dtype=w.dtype)
    return jnp.einsum("cgio,gh->cgiho", wr, eye).reshape(g // per, cw, cw)


def _mlstm_qkv(xm, conv_w, conv_b, w_q, w_k, w_v, w_gate, b_gate, b, s):
    di = xm.shape[1]
    cw = 256
    nchunk = di // cw
    nh = MLSTM_HEADS
    dh = di // nh
    ts = min(256, s)
    wk_bd = _block_diag(w_k, cw)
    wqk = jnp.concatenate([_block_diag(w_q, cw), wk_bd], axis=2).astype(BF16)
    wkt = jnp.swapaxes(wk_bd, 1, 2).astype(BF16)
    wv = _block_diag(w_v, cw).astype(BF16)
    wg = jnp.zeros((3 * di, LANES), F32).at[:, :2 * nh].set(w_gate).astype(BF16)
    bg = jnp.zeros((1, LANES), F32).at[0, :2 * nh].set(b_gate)
    xm3 = xm.reshape(b, s, di)
    blk = pl.BlockSpec((1, s, cw), lambda bi, c: (bi, 0, c))
    outs = pl.pallas_call(
        functools.partial(_mlstm_qkv_kernel, ts=ts, k_scale=dh ** -0.5),
        grid=(b, nchunk),
        in_specs=[blk,
                  pl.BlockSpec((1, MLSTM_CONV, cw), lambda bi, c: (c, 0, 0)),
                  pl.BlockSpec((1, 1, cw), lambda bi, c: (c, 0, 0)),
                  pl.BlockSpec((1, cw, 2 * cw), lambda bi, c: (c, 0, 0)),
                  pl.BlockSpec((1, cw, cw), lambda bi, c: (c, 0, 0)),
                  pl.BlockSpec((1, cw, cw), lambda bi, c: (c, 0, 0)),
                  pl.BlockSpec((cw, LANES), lambda bi, c: (c, 0)),
                  pl.BlockSpec((cw, LANES), lambda bi, c: (nchunk + c, 0)),
                  pl.BlockSpec((cw, LANES), lambda bi, c: (2 * nchunk + c, 0)),
                  pl.BlockSpec((1, LANES), lambda bi, c: (0, 0))],
        out_specs=[blk, blk, pl.BlockSpec((1, cw, s), lambda bi, c: (bi, c, 0)), blk,
                   pl.BlockSpec((1, s, LANES), lambda bi, c: (bi, 0, 0))],
        out_shape=[jax.ShapeDtypeStruct((b, s, di), BF16)] * 2 + [jax.ShapeDtypeStruct((b, di, s), BF16)]
        + [jax.ShapeDtypeStruct((b, s, di), BF16), jax.ShapeDtypeStruct((b, s, LANES), F32)],
        scratch_shapes=[pltpu.VMEM((s + SUBLANES, cw), F32)],
        compiler_params=_cparams("arbitrary", "arbitrary"),
        name="mlstm_qkv",
    )(xm3, conv_w.reshape(MLSTM_CONV, nchunk, cw).transpose(1, 0, 2), conv_b.reshape(nchunk, 1, cw),
      wqk, wkt, wv, wg, wg, wg, bg)
    return outs


def _log_sigmoid(x):
    return jnp.minimum(x, 0.0) - jnp.log1p(jnp.exp(-jnp.abs(x)))


def _split3(x):
    hi = x.astype(BF16)
    r = x - hi.astype(F32)
    mid = r.astype(BF16)
    lo = (r - mid.astype(F32)).astype(BF16)
    return hi, mid, lo


def _mlstm_cell_kernel(q_ref, kt_ref, v_ref, gc_ref, gr_ref, ng_ref, o_ref, ct_sc, n_sc, m_sc, *, nh):
    grp = pl.program_id(1)
    c = pl.program_id(2)
    L = q_ref.shape[1]
    hp = ct_sc.shape[0]
    dh = q_ref.shape[2] // hp

    @pl.when(c == 0)
    def _():
        ct_sc[...] = jnp.zeros(ct_sc.shape, F32)
        n_sc[...] = jnp.zeros(n_sc.shape, F32)
        m_sc[...] = jnp.zeros(m_sc.shape, F32)

    t_idx = lax.broadcasted_iota(jnp.int32, (L, L), 0)
    s_idx = lax.broadcasted_iota(jnp.int32, (L, L), 1)
    causal = s_idx <= t_idx
    tri = causal.astype(BF16)
    tri_t = (t_idx <= s_idx).astype(F32)
    gc = gc_ref[0]
    gr = gr_ref[0]
    b_cols = sum(jnp.dot(tri, part, preferred_element_type=F32) for part in _split3(_log_sigmoid(gc)))
    b_rows = jnp.dot(_log_sigmoid(gr), tri_t, preferred_element_type=F32, precision=HIGHEST)
    lane = lax.broadcasted_iota(jnp.int32, gc.shape, 1)
    sub = lax.broadcasted_iota(jnp.int32, gr.shape, 0)

    for hl in range(hp):
        h = grp * hp + hl
        cols = slice(hl * dh, (hl + 1) * dh)
        i_col = jnp.sum(jnp.where(lane == h, gc, 0.0), axis=1, keepdims=True)
        b_col = jnp.sum(jnp.where(lane == h + nh, b_cols, 0.0), axis=1, keepdims=True)
        i_row = jnp.sum(jnp.where(sub == h, gr, 0.0), axis=0, keepdims=True)
        b_row = jnp.sum(jnp.where(sub == h + nh, b_rows, 0.0), axis=0, keepdims=True)
        u_row = i_row - b_row
        u_col = i_col - b_col

        m_prev = m_sc[hl, 0:1, 0:1]
        dlog = jnp.where(causal, b_col + u_row, NEG_BIG)
        g = b_col + m_prev
        m_t = jnp.maximum(g, jnp.max(dlog, axis=1, keepdims=True))
        qb = q_ref[0, :, cols]
        ktb = kt_ref[0, cols, :]
        vb = v_ref[0, :, cols]
        w = jnp.exp(dlog - m_t) * jnp.dot(qb, ktb, preferred_element_type=F32)
        inter = jnp.exp(g - m_t)
        ct = ct_sc[hl]
        num = (jnp.dot(w.astype(BF16), vb, preferred_element_type=F32)
               + inter * jnp.dot(qb, ct.astype(BF16), preferred_element_type=F32))
        den = (jnp.sum(w, axis=1, keepdims=True)
               + inter * jnp.dot(qb, n_sc[hl].astype(BF16), preferred_element_type=F32))
        scale = 1.0 / jnp.maximum(jnp.abs(den), jnp.exp(-m_t))
        hh = num * jnp.tile(scale, (1, dh // LANES))

        mu = jnp.mean(hh, axis=1, keepdims=True)
        cen = hh - mu
        var = jnp.mean(cen * cen, axis=1, keepdims=True)
        o_ref[0, :, cols] = (cen * lax.rsqrt(var + NORM_EPS) * ng_ref[h]).astype(BF16)

        b_last = b_row[:, L - 1:L]
        m_new = jnp.maximum(b_last + m_prev, jnp.max(b_last + u_row, axis=1, keepdims=True))
        decay = jnp.exp(b_last + m_prev - m_new)
        ws_col = jnp.exp(b_last - m_new + u_col).astype(BF16)
        ct_sc[hl] = decay * ct + jnp.dot(ktb, vb * ws_col, preferred_element_type=F32)
        n_sc[hl] = decay * n_sc[hl] + jnp.dot(ktb, jnp.broadcast_to(ws_col, (L, LANES)),
                                              preferred_element_type=F32)
        m_sc[hl] = jnp.broadcast_to(m_new, m_sc.shape[1:])


def _mlstm_cell(q, kt, v, gates, norm_g, b, s):
    di = q.shape[2]
    nh = MLSTM_HEADS
    dh = di // nh
    L = min(256, s)
    gates_t = jnp.transpose(gates[:, :, :SUBLANES], (0, 2, 1))
    hp = MLSTM_HEADS_PER_STEP
    blk = pl.BlockSpec((1, L, hp * dh), lambda bi, g, c: (bi, c, g))
    return pl.pallas_call(
        functools.partial(_mlstm_cell_kernel, nh=nh),
        grid=(b, nh // hp, s // L),
        in_specs=[blk,
                  pl.BlockSpec((1, hp * dh, L), lambda bi, g, c: (bi, g, c)),
                  blk,
                  pl.BlockSpec((1, L, LANES), lambda bi, g, c: (bi, c, 0)),
                  pl.BlockSpec((1, SUBLANES, L), lambda bi, g, c: (bi, 0, c)),
                  pl.BlockSpec((nh, 1, dh), lambda bi, g, c: (0, 0, 0))],
        out_specs=blk,
        out_shape=jax.ShapeDtypeStruct((b, s, di), BF16),
        scratch_shapes=[pltpu.VMEM((hp, dh, dh), F32), pltpu.VMEM((hp, dh, LANES), F32),
                        pltpu.VMEM((hp, SUBLANES, LANES), F32)],
        compiler_params=_cparams("arbitrary", "arbitrary", "arbitrary"),
        name="mlstm_cell",
    )(q, kt, v, gates, gates_t, norm_g.reshape(nh, 1, dh))


def _mlstm_out_kernel(hn_ref, xc_ref, z_ref, skip_ref, w_ref, x_ref, mod_ref, g_ref, wr_ref, br_ref,
                      x_out_ref, h_out_ref, route_ref, meta_ref, *, n_experts):
    inner = ((hn_ref[...].astype(F32) + skip_ref[...] * xc_ref[...].astype(F32))
             * _silu(z_ref[...].astype(F32)))
    h4 = _mixer_out_core(inner.astype(BF16), w_ref, x_ref, mod_ref, g_ref, x_out_ref)
    h_hi = h4.astype(BF16)
    h_out_ref[...] = h_hi

    tm = h4.shape[0]
    lane = lax.broadcasted_iota(jnp.int32, (tm, LANES), 1)
    h_lo = (h4 - h_hi.astype(F32)).astype(BF16)
    w_hi = wr_ref[...].astype(BF16)
    w_lo = (wr_ref[...] - w_hi.astype(F32)).astype(BF16)
    hi_terms = jnp.dot(h_hi, jnp.concatenate([w_hi, w_lo], axis=1), preferred_element_type=F32)
    logits = (hi_terms[:, :LANES] + hi_terms[:, LANES:] + jnp.dot(h_lo, w_hi, preferred_element_type=F32)
              + br_ref[...])
    logits = jnp.where(lane < n_experts, logits, NEG_BIG)
    ex = jnp.exp(logits - jnp.max(logits, axis=1, keepdims=True))
    probs = ex / jnp.sum(ex, axis=1, keepdims=True)
    probs = jnp.where(lane < n_experts, probs, -1.0)
    lane_f = lane.astype(F32)
    p0 = jnp.max(probs, axis=1, keepdims=True)
    e0 = jnp.min(jnp.where(probs == p0, lane_f, float(LANES)), axis=1, keepdims=True)
    rest = jnp.where(lane_f == e0, -1.0, probs)
    p1 = jnp.max(rest, axis=1, keepdims=True)
    e1 = jnp.min(jnp.where(rest == p1, lane_f, float(LANES)), axis=1, keepdims=True)
    tot = p0 + p1
    sel0 = lane_f == e0
    sel1 = lane_f == e1
    sel = jnp.where(sel0 | sel1, 1.0, 0.0)
    r_idx = lax.broadcasted_iota(jnp.int32, (tm, tm), 0)
    c_idx = lax.broadcasted_iota(jnp.int32, (tm, tm), 1)
    before = (c_idx < r_idx).astype(BF16)
    cum = jnp.dot(before, sel.astype(BF16), preferred_element_type=F32)
    run8 = jnp.floor((jnp.sum(sel, axis=0, keepdims=True) + 7.0) * 0.125)
    e_r = lax.broadcasted_iota(jnp.int32, (LANES, LANES), 0)
    e_c = lax.broadcasted_iota(jnp.int32, (LANES, LANES), 1)
    seg8 = jnp.dot(jnp.broadcast_to(run8, (SUBLANES, LANES)).astype(BF16), (e_r < e_c).astype(BF16),
                   preferred_element_type=F32)[0:1, :]
    pos = cum + 8.0 * seg8
    pos0 = jnp.sum(jnp.where(sel0, pos, 0.0), axis=1, keepdims=True)
    pos1 = jnp.sum(jnp.where(sel1, pos, 0.0), axis=1, keepdims=True)
    vals = (p0 / tot, p1 / tot, e0, e1, pos0, pos1)
    route = jnp.zeros((tm, LANES), F32)
    for idx, val in enumerate(vals):
        route = jnp.where(lane == idx, val, route)
    route_ref[...] = route
    meta = jnp.where(lane[0:1, :] < n_experts, run8, 0.0)
    meta = jnp.where((lane[0:1, :] >= n_experts) & (lane[0:1, :] < 2 * n_experts),
                     pltpu.roll(8.0 * seg8, n_experts, 1), meta)
    meta_ref[0] = jnp.broadcast_to(meta, meta_ref.shape[1:]).astype(jnp.int32)


def _mlstm_out(hn, xc, z, skip, w_bf16, x2d, mod, g, w_router, b_router, s):
    t, d = x2d.shape
    di = hn.shape[1]
    e = w_router.shape[1]
    tm = _token_tile(s)
    tpb = s // tm
    n_tok_tiles = t // tm
    wr = jnp.zeros((d, LANES), F32).at[:, :e].set(w_router)
    br = jnp.zeros((1, LANES), F32).at[0, :e].set(b_router)
    tok = lambda n: pl.BlockSpec((tm, n), lambda i: (i, 0))
    return pl.pallas_call(
        functools.partial(_mlstm_out_kernel, n_experts=e),
        grid=(n_tok_tiles,),
        in_specs=[tok(di), tok(di), tok(di),
                  pl.BlockSpec((1, di), lambda i: (0, 0)),
                  pl.BlockSpec(w_bf16.shape, lambda i: (0, 0)),
                  tok(d),
                  pl.BlockSpec((1, 6, d), lambda i: (i // tpb, 0, 0)),
                  pl.BlockSpec((4, d), lambda i: (0, 0)),
                  pl.BlockSpec((d, LANES), lambda i: (0, 0)),
                  pl.BlockSpec((1, LANES), lambda i: (0, 0))],
        out_specs=[tok(d), tok(d), tok(LANES), pl.BlockSpec((1, SUBLANES, LANES), lambda i: (i, 0, 0))],
        out_shape=[jax.ShapeDtypeStruct((t, d), F32), jax.ShapeDtypeStruct((t, d), BF16),
                   jax.ShapeDtypeStruct((t, LANES), F32),
                   jax.ShapeDtypeStruct((n_tok_tiles, SUBLANES, LANES), jnp.int32)],
        compiler_params=_cparams("arbitrary"),
        name="mlstm_out_router",
    )(hn, xc, z, skip.reshape(1, di), w_bf16, x2d, mod, g, wr, br)


def _dispatch_kernel(start_ref, run_ref, seg_ref, fill_ref, h_ref, route_ref, xs_ref, stage_sc, inflight_sm, sem,
                     *, n_experts):
    i = pl.program_id(0)
    tm = h_ref.shape[0]
    r_stage = stage_sc.shape[0] - DISPATCH_ROWS

    def copy(src_row, dst_row, rows):
        return pltpu.make_async_copy(stage_sc.at[pl.ds(src_row, rows)], xs_ref.at[pl.ds(dst_row, rows)], sem)

    def wait_copies(n, rows):
        def body(_, carry):
            copy(0, 0, rows).wait()
            return carry
        lax.fori_loop(0, n, body, 0)

    def zero_fill(first_row, n, rows):
        def body(j, carry):
            copy(0, pl.multiple_of(first_row + j * rows, SUBLANES), rows).start()
            return carry
        lax.fori_loop(0, n, body, 0)
        wait_copies(n, rows)

    @pl.when(i == 0)
    def _():
        inflight_sm[0] = 0
        stage_sc[r_stage:, :] = jnp.zeros((DISPATCH_ROWS, stage_sc.shape[1]), F32)

    route_t = jnp.transpose(route_ref[...])
    slot_row = lax.broadcasted_iota(jnp.int32, (r_stage, tm), 0).astype(F32)
    onehot = jnp.where((route_t[4:5, :] == slot_row) | (route_t[5:6, :] == slot_row), 1.0, 0.0).astype(BF16)
    sorted_rows = jnp.dot(onehot, h_ref[...], preferred_element_type=F32)

    wait_copies(inflight_sm[0], DISPATCH_ROWS)
    stage_sc[0:r_stage, :] = sorted_rows
    issued = 0
    for e in range(n_experts):
        rows = run_ref[i * n_experts + e] * SUBLANES
        start = start_ref[i * n_experts + e]
        seg = seg_ref[i * n_experts + e]
        for k in range(tm // DISPATCH_ROWS):
            @pl.when(rows > k * DISPATCH_ROWS)
            def _(seg=seg, start=start, k=k):
                copy(pl.multiple_of(seg + k * DISPATCH_ROWS, SUBLANES),
                     pl.multiple_of(start + k * DISPATCH_ROWS, SUBLANES), DISPATCH_ROWS).start()
        issued = issued + (rows + DISPATCH_ROWS - 1) // DISPATCH_ROWS
    inflight_sm[0] = issued

    @pl.when(i == pl.num_programs(0) - 1)
    def _():
        wait_copies(inflight_sm[0], DISPATCH_ROWS)
        stage_sc[0:DISPATCH_ROWS, :] = jnp.zeros((DISPATCH_ROWS, stage_sc.shape[1]), F32)
        for e in range(n_experts):
            zero_fill(fill_ref[e], fill_ref[n_experts + e], SUBLANES)
        zero_fill(fill_ref[2 * n_experts], fill_ref[2 * n_experts + 1], DISPATCH_ROWS)


def _dispatch(h4, route, starts, runs, segs, fill, n_rows, s, n_experts):
    t, d = h4.shape
    tm = _token_tile(s)
    r_stage = TOP_K * tm + SUBLANES * n_experts
    grid_spec = pltpu.PrefetchScalarGridSpec(
        num_scalar_prefetch=4,
        grid=(t // tm,),
        in_specs=[pl.BlockSpec((tm, d), lambda i, *_: (i, 0)),
                  pl.BlockSpec((tm, LANES), lambda i, *_: (i, 0))],
        out_specs=pl.BlockSpec(memory_space=pl.ANY),
        scratch_shapes=[pltpu.VMEM((r_stage + DISPATCH_ROWS, d), F32), pltpu.SMEM((1,), jnp.int32),
                        pltpu.SemaphoreType.DMA(())],
    )
    return pl.pallas_call(
        functools.partial(_dispatch_kernel, n_experts=n_experts),
        grid_spec=grid_spec,
        out_shape=jax.ShapeDtypeStruct((n_rows, d), F32),
        compiler_params=_cparams("arbitrary"),
        name="moe_dispatch",
    )(starts, runs, segs, fill, h4, route)


def _expert_ffn_kernel(te_ref, nreal_ref, xs_ref, wg_ref, wu_ref, wd_ref, ys_ref, xb_sc, acc_sc):
    i = pl.program_id(0)
    j = pl.program_id(1)
    last = pl.num_programs(1) - 1
    real = i < nreal_ref[0]

    @pl.when(real & (j == 0))
    def _():
        xb_sc[...] = xs_ref[...].astype(BF16)
        acc_sc[...] = jnp.zeros(acc_sc.shape, F32)

    @pl.when(real)
    def _():
        acc_sc[...] += _swiglu_partial(xb_sc[...], wg_ref.shape[2], lambda c: wg_ref[0, :, c],
                                       lambda c: wu_ref[0, :, c], lambda c: wd_ref[0, c, :])

    @pl.when(real & (j == last))
    def _():
        ys_ref[...] = acc_sc[...]

    @pl.when(jnp.logical_not(real) & (j == last))
    def _():
        ys_ref[...] = jnp.zeros(ys_ref.shape, F32)


def _expert_ffn(xs, tile_expert, n_real, wg, wu, wd):
    p, d = xs.shape
    f = wg.shape[2]
    tm = EXPERT_TILE
    tf = _ffn_chunk(f)
    nj = f // tf

    def row_idx(i, j, te, nr):
        return (jnp.minimum(i, nr[0] - 1), 0)

    def col_j(i, j, nr):
        return jnp.where(i < nr[0], j, nj - 1)

    grid_spec = pltpu.PrefetchScalarGridSpec(
        num_scalar_prefetch=2,
        grid=(p // tm, nj),
        in_specs=[pl.BlockSpec((tm, d), row_idx),
                  pl.BlockSpec((1, d, tf), lambda i, j, te, nr: (te[i], 0, col_j(i, j, nr))),
                  pl.BlockSpec((1, d, tf), lambda i, j, te, nr: (te[i], 0, col_j(i, j, nr))),
                  pl.BlockSpec((1, tf, d), lambda i, j, te, nr: (te[i], col_j(i, j, nr), 0))],
        out_specs=pl.BlockSpec((tm, d), lambda i, j, te, nr: (i, 0)),
        scratch_shapes=[pltpu.VMEM((tm, d), BF16), pltpu.VMEM((tm, d), F32)],
    )
    return pl.pallas_call(
        _expert_ffn_kernel,
        grid_spec=grid_spec,
        out_shape=jax.ShapeDtypeStruct((p, d), F32),
        compiler_params=_cparams("arbitrary", "arbitrary"),
        name="moe_expert_ffn",
    )(tile_expert, n_real, xs, wg, wu, wd)


def _combine_kernel(start_ref, run_ref, seg_ref, ys_ref, route_ref, x_ref, mod_ref, g_ref, o_ref, buf_sc, y_sc,
                    col_sc, src_sm, exp_sm, off_sm, count_sm, sems, *, n_experts):
    i = pl.program_id(0)
    tm = x_ref.shape[0]
    d = x_ref.shape[1]
    max_chunks = buf_sc.shape[1]

    def chunk_copy(slot, ci):
        return pltpu.make_async_copy(
            ys_ref.at[pl.ds(pl.multiple_of(src_sm[slot * max_chunks + ci], SUBLANES), DISPATCH_ROWS)],
            buf_sc.at[slot, ci], sems.at[slot, ci])

    def fetch(tile, slot):
        n_chunks = 0
        for e in range(n_experts):
            start = start_ref[tile * n_experts + e]
            rows = run_ref[tile * n_experts + e] * SUBLANES
            seg = seg_ref[tile * n_experts + e]
            for k in range(tm // DISPATCH_ROWS):
                @pl.when(rows > k * DISPATCH_ROWS)
                def _(start=start, seg=seg, k=k, ci=n_chunks + k):
                    src_sm[slot * max_chunks + ci] = start + k * DISPATCH_ROWS
                    exp_sm[slot * max_chunks + ci] = e
                    off_sm[slot * max_chunks + ci] = seg + k * DISPATCH_ROWS
                    chunk_copy(slot, ci).start()
            n_chunks = n_chunks + (rows + DISPATCH_ROWS - 1) // DISPATCH_ROWS
        count_sm[slot] = n_chunks

    slot = i % 2

    @pl.when(i == 0)
    def _():
        fetch(i, slot)

    @pl.when(i + 1 < pl.num_programs(0))
    def _():
        fetch(i + 1, 1 - slot)

    route = route_ref[...]
    for idx in range(col_sc.shape[0]):
        col_sc[idx] = jnp.broadcast_to(route[:, idx:idx + 1], col_sc.shape[1:])
    lane = lax.broadcasted_iota(jnp.int32, (tm, DISPATCH_ROWS), 1).astype(F32)
    y_sc[...] = jnp.zeros(y_sc.shape, F32)

    def body(ci, carry):
        chunk_copy(slot, ci).wait()
        ef = exp_sm[slot * max_chunks + ci].astype(F32)
        target = lane + off_sm[slot * max_chunks + ci].astype(F32)
        first = col_sc[2] == ef
        spread = jnp.where((first & (col_sc[4] == target)) | ((col_sc[3] == ef) & (col_sc[5] == target)),
                           1.0, 0.0).astype(BF16)
        rows_out = jnp.dot(spread, buf_sc[slot, ci].astype(BF16), preferred_element_type=F32)
        weight = jnp.where(first, col_sc[0], col_sc[1])
        y_sc[...] += jnp.tile(weight, (1, d // LANES)) * rows_out
        return carry

    lax.fori_loop(0, count_sm[slot], body, 0)
    o_ref[...] = x_ref[...] + mod_ref[0, 5:6, :] * _rms(y_sc[...], g_ref[3:4, :])


def _combine(ys, starts, runs, segs, route, x2d, mod, g, s, n_experts):
    t, d = x2d.shape
    tm = _token_tile(s)
    tpb = s // tm
    max_chunks = TOP_K * tm // DISPATCH_ROWS + n_experts
    grid_spec = pltpu.PrefetchScalarGridSpec(
        num_scalar_prefetch=3,
        grid=(t // tm,),
        in_specs=[pl.BlockSpec(memory_space=pl.ANY),
                  pl.BlockSpec((tm, LANES), lambda i, *_: (i, 0)),
                  pl.BlockSpec((tm, d), lambda i, *_: (i, 0)),
                  pl.BlockSpec((1, 6, d), lambda i, *_: (i // tpb, 0, 0)),
                  pl.BlockSpec((4, d), lambda i, *_: (0, 0))],
        out_specs=pl.BlockSpec((tm, d), lambda i, *_: (i, 0)),
        scratch_shapes=[pltpu.VMEM((2, max_chunks, DISPATCH_ROWS, d), F32), pltpu.VMEM((tm, d), F32),
                        pltpu.VMEM((6, tm, LANES), F32),
                        pltpu.SMEM((2 * max_chunks,), jnp.int32), pltpu.SMEM((2 * max_chunks,), jnp.int32),
                        pltpu.SMEM((2 * max_chunks,), jnp.int32), pltpu.SMEM((2,), jnp.int32),
                        pltpu.SemaphoreType.DMA((2, max_chunks))],
    )
    return pl.pallas_call(
        functools.partial(_combine_kernel, n_experts=n_experts),
        grid_spec=grid_spec,
        out_shape=jax.ShapeDtypeStruct((t, d), F32),
        compiler_params=_cparams("arbitrary"),
        name="moe_combine",
    )(starts, runs, segs, ys, route, x2d, mod, g)


def _moe(h4, route, meta, x2d, mod, g, wg, wu, wd, s):
    t, d = x2d.shape
    e = wg.shape[0]
    tm = EXPERT_TILE
    n_tok_tiles = t // _token_tile(s)
    n_tiles = (TOP_K * t + (SUBLANES * n_tok_tiles + DISPATCH_ROWS) * e) // tm + e
    runs = meta[:, 0, :e]
    segs = meta[:, 0, e:2 * e]
    used = SUBLANES * jnp.sum(runs, axis=0)
    tiles_e = jnp.where(used > 0, (used + DISPATCH_ROWS + tm - 1) // tm, 0)
    ends = jnp.cumsum(tiles_e)
    off = (ends - tiles_e) * tm
    starts = off[None, :] + SUBLANES * (jnp.cumsum(runs, axis=0) - runs)
    n_real = ends[e - 1:e].astype(jnp.int32)
    fill = jnp.concatenate([off + used, (tiles_e * tm - used) // SUBLANES,
                            n_real * tm, (n_tiles - n_real) * (tm // DISPATCH_ROWS)]).astype(jnp.int32)
    tile_ids = jnp.arange(n_tiles, dtype=jnp.int32)
    tile_expert = jnp.minimum(jnp.sum(tile_ids[:, None] >= ends[None, :], axis=1), e - 1)
    last_expert = jnp.minimum(jnp.sum(n_real[0] - 1 >= ends), e - 1)
    tile_expert = jnp.where(tile_ids < n_real[0], tile_expert, last_expert).astype(jnp.int32)
    flat = lambda a: a.reshape(-1).astype(jnp.int32)
    xs = _dispatch(h4, route, flat(starts), flat(runs), flat(segs), fill, n_tiles * tm, s, e)
    ys = _expert_ffn(xs, tile_expert, n_real, wg, wu, wd)
    return _combine(ys, flat(starts), flat(runs), flat(segs), route, x2d, mod, g, s, e)


def kernel(x, c, mod_w, mod_b, norm_g, attn_w_in, attn_w_out, attn_lambda, attn_subln, ffn_w_gate, ffn_w_up, ffn_w_down, mlstm_w_in, mlstm_conv_w, mlstm_conv_b, mlstm_w_q, mlstm_w_k, mlstm_w_v, mlstm_w_gate, mlstm_b_gate, mlstm_skip, mlstm_norm, mlstm_w_out, moe_w_router, moe_b_router, moe_w_gate, moe_w_up, moe_w_down):
    b, s, d = x.shape
    depth = mod_w.shape[0]
    mod_all = _modulation(c, mod_w, mod_b)
    x2d = x.reshape(b * s, d)
    da = d // (2 * ATTN_HEADS)
    for i in range(depth):
        mod = mod_all[i]
        g = norm_g[i]
        j = i // N_MIXERS
        if i % N_MIXERS == 0:
            lambda_init = 0.8 - 0.6 * math.exp(-0.3 * i)
            q, k, v = _in_proj(x2d, mod, g, attn_w_in[j].astype(BF16), 3, (LOG2E * da ** -0.5, 1.0, 1.0), s,
                               "attn_in_proj")
            o = _diff_attention(q, k, v, attn_lambda[j], attn_subln[j], lambda_init, b, s)
            x2d, h2 = _attn_out(o, attn_w_out[j].astype(BF16), x2d, mod, g, s)
            x2d = _dense_ffn(h2, ffn_w_gate[j].astype(BF16), ffn_w_up[j].astype(BF16),
                             ffn_w_down[j].astype(BF16), x2d, mod, g, s)
        else:
            xm, z = _in_proj(x2d, mod, g, mlstm_w_in[j].astype(BF16), 2, (1.0, 1.0), s, "mlstm_in_proj")
            xc, q, kt, v, gates = _mlstm_qkv(xm, mlstm_conv_w[j], mlstm_conv_b[j], mlstm_w_q[j], mlstm_w_k[j],
                                             mlstm_w_v[j], mlstm_w_gate[j], mlstm_b_gate[j], b, s)
            hn = _mlstm_cell(q, kt, v, gates, mlstm_norm[j], b, s)
            di = hn.shape[2]
            x2d, h4, route, meta = _mlstm_out(hn.reshape(b * s, di), xc.reshape(b * s, di), z, mlstm_skip[j],
                                              mlstm_w_out[j].astype(BF16), x2d, mod, g,
                                              moe_w_router[j], moe_b_router[j], s)
            x2d = _moe(h4, route, meta, x2d, mod, g, moe_w_gate[j].astype(BF16), moe_w_up[j].astype(BF16),
                       moe_w_down[j].astype(BF16), s)
    return x2d.reshape(b, s, d)
```

```python
import functools
import math

import jax
import jax.numpy as jnp
from jax import lax
from jax.experimental import pallas as pl
from jax.experimental.pallas import tpu as pltpu

F32 = jnp.float32
BF16 = jnp.bfloat16
HIGHEST = lax.Precision.HIGHEST

CHUNK = 64
ATTN_HEADS = 8
ALIBI_MAX_BIAS = 8.0
MLSTM_HEADS = 4
MLSTM_CONV = 4
QKV_BLOCK = 4
TOP_K = 2
NORM_EPS = 1e-6
N_MIXERS = 2

LANES = 128
SUBLANES = 8
VMEM_LIMIT_BYTES = 56 * 1024 * 1024
FFN_SUBCHUNK = 256
EXPERT_TILE = 512
DISPATCH_ROWS = 128
NEG_BIG = -1e30
LOG2E = math.log2(math.e)


def _cparams(*sem):
    return pltpu.CompilerParams(dimension_semantics=sem, vmem_limit_bytes=VMEM_LIMIT_BYTES)


def _rms(x, g):
    return x * lax.rsqrt(jnp.mean(x * x, axis=-1, keepdims=True) + NORM_EPS) * g


def _silu(x):
    return x * jax.nn.sigmoid(x)


def _token_tile(s):
    return min(512, s)


def _mod_kernel(c_ref, w_ref, b_ref, o_ref):
    cond = _silu(c_ref[...])
    o_ref[0] = jnp.dot(cond, w_ref[0], preferred_element_type=F32, precision=HIGHEST) + b_ref[0]


def _modulation(c, mod_w, mod_b):
    depth, d, n = mod_w.shape
    b = c.shape[0]
    tn = 1536 if n % 1536 == 0 else n
    out = pl.pallas_call(
        _mod_kernel,
        grid=(depth, n // tn),
        in_specs=[pl.BlockSpec((b, d), lambda i, j: (0, 0)),
                  pl.BlockSpec((1, d, tn), lambda i, j: (i, 0, j)),
                  pl.BlockSpec((1, 1, tn), lambda i, j: (i, 0, j))],
        out_specs=pl.BlockSpec((1, b, tn), lambda i, j: (i, 0, j)),
        out_shape=jax.ShapeDtypeStruct((depth, b, n), F32),
        compiler_params=_cparams("arbitrary", "arbitrary"),
        name="modulation",
    )(c, mod_w, mod_b.reshape(depth, 1, n))
    return out.reshape(depth, b, 6, d)


def _in_proj_kernel(x_ref, mod_ref, g_ref, w_ref, *o_refs, scales):
    x = x_ref[...]
    h = _rms(x, g_ref[0:1, :]) * (1.0 + mod_ref[0, 1:2, :]) + mod_ref[0, 0:1, :]
    hb = h.astype(BF16)
    n = o_refs[0].shape[1]
    for idx, o_ref in enumerate(o_refs):
        r = jnp.dot(hb, w_ref[:, idx * n:(idx + 1) * n], preferred_element_type=F32)
        if scales[idx] != 1.0:
            r = r * scales[idx]
        o_ref[...] = r.astype(BF16)


def _in_proj(x2d, mod, g, w_bf16, n_out, scales, s, name):
    t, d = x2d.shape
    tm = _token_tile(s)
    tpb = s // tm
    n = w_bf16.shape[1] // n_out
    return pl.pallas_call(
        functools.partial(_in_proj_kernel, scales=scales),
        grid=(t // tm,),
        in_specs=[pl.BlockSpec((tm, d), lambda i: (i, 0)),
                  pl.BlockSpec((1, 6, d), lambda i: (i // tpb, 0, 0)),
                  pl.BlockSpec((4, d), lambda i: (0, 0)),
                  pl.BlockSpec(w_bf16.shape, lambda i: (0, 0))],
        out_specs=[pl.BlockSpec((tm, n), lambda i: (i, 0))] * n_out,
        out_shape=[jax.ShapeDtypeStruct((t, n), BF16)] * n_out,
        compiler_params=_cparams("arbitrary"),
        name=name,
    )(x2d, mod, g, w_bf16)


def _attn_kernel(slopes_ref, lam_ref, subln_ref, q_ref, k_ref, v_ref, o_ref, m_sc, acc_sc, bias_sc, vext_sc,
                 s_sc, smax_sc, *, tq, tk, lambda_init):
    slope = slopes_ref[pl.program_id(1)]
    hd = q_ref.shape[2]
    half = hd // 2
    n_q = q_ref.shape[1] // tq

    vext_sc[:, :hd] = v_ref[0]
    vext_sc[:, hd:] = jnp.ones((vext_sc.shape[0], hd), BF16)
    row = lax.broadcasted_iota(jnp.int32, (tq, tk), 0)
    col = lax.broadcasted_iota(jnp.int32, (tq, tk), 1)
    dist = (row - col).astype(F32)
    bias_sc[0] = -slope * dist
    bias_sc[1] = jnp.where(col // CHUNK <= row // CHUNK, -slope * jnp.abs(dist), NEG_BIG)

    lam = lam_ref[...]
    lam_full = (jnp.exp(jnp.sum(lam[0:1, :] * lam[1:2, :], axis=1, keepdims=True))
                - jnp.exp(jnp.sum(lam[2:3, :] * lam[3:4, :], axis=1, keepdims=True)) + lambda_init)

    def scores(qi, j, slot):
        q = q_ref[0, qi * tq:(qi + 1) * tq, :]
        lane = lax.broadcasted_iota(jnp.int32, q.shape, 1)
        zero = jnp.zeros_like(q)
        q2 = jnp.concatenate([jnp.where(lane < half, q, zero), jnp.where(lane >= half, q, zero)], axis=0)
        kj = k_ref[0, j * tk:(j + 1) * tk, :]
        s = lax.dot_general(q2, kj, (((1,), (1,)), ((), ())), preferred_element_type=F32)
        bias = bias_sc[1 if j == qi else 0]
        s = s + jnp.concatenate([bias, bias], axis=0)
        s_sc[slot] = s
        smax_sc[slot] = jnp.broadcast_to(jnp.max(s, axis=1, keepdims=True), smax_sc.shape[1:])

    def accumulate(qi, j, slot):
        const = 0.0 if j == qi else -slope * float((qi - j) * tq)
        m_prev = m_sc[...]
        m_next = jnp.maximum(m_prev, smax_sc[slot] + const)
        alpha = jnp.exp2(m_prev - m_next)
        p = jnp.exp2(s_sc[slot] - jnp.tile(m_next - const, (1, tk // LANES)))
        pv = jnp.dot(p.astype(BF16), vext_sc[j * tk:(j + 1) * tk, :], preferred_element_type=F32)
        acc_sc[...] = jnp.tile(alpha, (1, 2)) * acc_sc[...] + pv
        m_sc[...] = m_next

    def finalize(qi):
        acc = acc_sc[...]
        o_all = acc[:, :hd] / acc[:, hd:]
        o = o_all[:tq] - lam_full * o_all[tq:]
        o = _rms(o, subln_ref[...]) * (1.0 - lambda_init)
        o_ref[0, qi * tq:(qi + 1) * tq, :] = o.astype(BF16)

    blocks = [(qi, j) for qi in range(n_q) for j in range(qi + 1)]
    scores(*blocks[0], 0)
    for n, (qi, j) in enumerate(blocks):
        if n + 1 < len(blocks):
            scores(*blocks[n + 1], (n + 1) % 2)
        if j == 0:
            m_sc[...] = jnp.full(m_sc.shape, NEG_BIG, F32)
            acc_sc[...] = jnp.zeros(acc_sc.shape, F32)
        accumulate(qi, j, n % 2)
        if j == qi:
            finalize(qi)


def _diff_attention(q, k, v, lam, subln, lambda_init, b, s):
    d = q.shape[1]
    hd = d // ATTN_HEADS
    tq = tk = min(512, s)
    slopes = LOG2E * jnp.exp2(-ALIBI_MAX_BIAS * jnp.arange(1, ATTN_HEADS + 1, dtype=F32) / ATTN_HEADS)
    q3, k3, v3 = (a.reshape(b, s, d) for a in (q, k, v))
    grid_spec = pltpu.PrefetchScalarGridSpec(
        num_scalar_prefetch=1,
        grid=(b, ATTN_HEADS),
        in_specs=[pl.BlockSpec(lam.shape, lambda bi, h, sl: (0, 0)),
                  pl.BlockSpec((1, hd), lambda bi, h, sl: (0, 0)),
                  pl.BlockSpec((1, s, hd), lambda bi, h, sl: (bi, 0, h)),
                  pl.BlockSpec((1, s, hd), lambda bi, h, sl: (bi, 0, h)),
                  pl.BlockSpec((1, s, hd), lambda bi, h, sl: (bi, 0, h))],
        out_specs=pl.BlockSpec((1, s, hd), lambda bi, h, sl: (bi, 0, h)),
        scratch_shapes=[pltpu.VMEM((2 * tq, LANES), F32), pltpu.VMEM((2 * tq, 2 * hd), F32),
                        pltpu.VMEM((2, tq, tk), F32), pltpu.VMEM((s, 2 * hd), BF16),
                        pltpu.VMEM((2, 2 * tq, tk), F32), pltpu.VMEM((2, 2 * tq, LANES), F32)],
    )
    o = pl.pallas_call(
        functools.partial(_attn_kernel, tq=tq, tk=tk, lambda_init=lambda_init),
        grid_spec=grid_spec,
        out_shape=jax.ShapeDtypeStruct((b, s, d), BF16),
        compiler_params=_cparams("arbitrary", "arbitrary"),
        name="diff_attention",
    )(slopes, lam, subln.reshape(1, hd), q3, k3, v3)
    return o.reshape(b * s, d)


def _mixer_out_core(inp_bf16, w_ref, x_ref, mod_ref, g_ref, x_out_ref):
    y = jnp.dot(inp_bf16, w_ref[...], preferred_element_type=F32)
    x1 = x_ref[...] + mod_ref[0, 2:3, :] * _rms(y, g_ref[1:2, :])
    x_out_ref[...] = x1
    return _rms(x1, g_ref[2:3, :]) * (1.0 + mod_ref[0, 4:5, :]) + mod_ref[0, 3:4, :]


def _swiglu_partial(h, width, wg, wu, wd):
    out = None
    for c0 in range(0, width, FFN_SUBCHUNK):
        c = slice(c0, min(c0 + FFN_SUBCHUNK, width))
        gate = jnp.dot(h, wg(c), preferred_element_type=F32)
        up = jnp.dot(h, wu(c), preferred_element_type=F32)
        part = jnp.dot((_silu(gate) * up).astype(BF16), wd(c), preferred_element_type=F32)
        out = part if out is None else out + part
    return out


def _attn_out_ffn_kernel(o_ref, wo_ref, x_ref, mod_ref, g_ref, wg_ref, wu_ref, wd_ref, out_ref,
                         x1_sc, h2_sc, acc_sc):
    j = pl.program_id(1)

    @pl.when(j == 0)
    def _():
        h2_sc[...] = _mixer_out_core(o_ref[...], wo_ref, x_ref, mod_ref, g_ref, x1_sc).astype(BF16)
        acc_sc[...] = jnp.zeros(acc_sc.shape, F32)

    acc_sc[...] += _swiglu_partial(h2_sc[...], wg_ref.shape[1], lambda c: wg_ref[:, c], lambda c: wu_ref[:, c],
                                   lambda c: wd_ref[c, :])

    @pl.when(j == pl.num_programs(1) - 1)
    def _():
        out_ref[...] = x1_sc[...] + mod_ref[0, 5:6, :] * _rms(acc_sc[...], g_ref[3:4, :])


def _ffn_chunk(f):
    for tf in (1792, 1408, 1024, 896, 512, 256, 128):
        if f % tf == 0:
            return tf
    return f


def _attn_out_ffn(o, wo, wg, wu, wd, x2d, mod, g, s):
    t, d = x2d.shape
    f = wg.shape[1]
    tm = _token_tile(s)
    tpb = s // tm
    tf = _ffn_chunk(f)
    tok = pl.BlockSpec((tm, d), lambda i, j: (i, 0))
    return pl.pallas_call(
        _attn_out_ffn_kernel,
        grid=(t // tm, f // tf),
        in_specs=[tok,
                  pl.BlockSpec(wo.shape, lambda i, j: (0, 0)),
                  tok,
                  pl.BlockSpec((1, 6, d), lambda i, j: (i // tpb, 0, 0)),
                  pl.BlockSpec((4, d), lambda i, j: (0, 0)),
                  pl.BlockSpec((d, tf), lambda i, j: (0, j)),
                  pl.BlockSpec((d, tf), lambda i, j: (0, j)),
                  pl.BlockSpec((tf, d), lambda i, j: (j, 0))],
        out_specs=tok,
        out_shape=jax.ShapeDtypeStruct((t, d), F32),
        scratch_shapes=[pltpu.VMEM((tm, d), F32), pltpu.VMEM((tm, d), BF16), pltpu.VMEM((tm, d), F32)],
        compiler_params=_cparams("arbitrary", "arbitrary"),
        name="attn_out_dense_ffn",
    )(o, wo, x2d, mod, g, wg, wu, wd)


def _mlstm_qkv_kernel(xm_ref, cw_ref, cb_ref, wqk_ref, wv_ref, wgq_ref, wgk_ref, wgv_ref, bg_ref,
                      xc_ref, q_ref, kt_ref, v_ref, gates_ref, pad_sc, *, ts, k_scale):
    c = pl.program_id(1)
    s, cw = xm_ref.shape[1], xm_ref.shape[2]
    front = SUBLANES
    pad_sc[0:front, :] = jnp.zeros((front, cw), F32)
    pad_sc[front:front + s, :] = xm_ref[0].astype(F32)

    @pl.when(c == 0)
    def _():
        gates_ref[0] = jnp.broadcast_to(bg_ref[...], gates_ref.shape[1:])

    for r in range(s // ts):
        r0 = r * ts
        conv = cb_ref[0]
        for j in reversed(range(MLSTM_CONV)):
            start = r0 + front - (MLSTM_CONV - 1) + j
            conv = conv + pad_sc[start:start + ts, :] * cw_ref[0, j:j + 1, :]
        xc = _silu(conv)
        xcb = xc.astype(BF16)
        qk = jnp.dot(xcb, wqk_ref[0], preferred_element_type=F32)
        qb = qk[:, :cw].astype(BF16)
        kb = qk[:, cw:].astype(BF16)
        vb = jnp.dot(xm_ref[0, r0:r0 + ts, :], wv_ref[0], preferred_element_type=F32).astype(BF16)
        gates_ref[0, r0:r0 + ts, :] += (jnp.dot(qb, wgq_ref[...], preferred_element_type=F32)
                                        + jnp.dot(kb, wgk_ref[...], preferred_element_type=F32)
                                        + jnp.dot(vb, wgv_ref[...], preferred_element_type=F32))
        xc_ref[0, r0:r0 + ts, :] = xcb
        q_ref[0, r0:r0 + ts, :] = qb
        kt_ref[0, :, r0:r0 + ts] = jnp.transpose(qk[:, cw:] * k_scale).astype(BF16)
        v_ref[0, r0:r0 + ts, :] = vb


def _block_diag(w, cw):
    g, qb, _ = w.shape
    per = cw // qb
    wr = w.reshape(g // per, per, qb, qb)
    eye = jnp.eye(per, dtype=w.dtype)
    return jnp.einsum("cgio,gh->cgiho", wr, eye).reshape(g // per, cw, cw)


def _mlstm_qkv(xm, conv_w, conv_b, w_q, w_k, w_v, w_gate, b_gate, b, s):
    di = xm.shape[1]
    cw = 256
    nchunk = di // cw
    nh = MLSTM_HEADS
    dh = di // nh
    ts = min(256, s)
    wqk = jnp.concatenate([_block_diag(w_q, cw), _block_diag(w_k, cw)], axis=2).astype(BF16)
    wv = _block_diag(w_v, cw).astype(BF16)
    wg = jnp.zeros((3 * di, LANES), F32).at[:, :2 * nh].set(w_gate).astype(BF16)
    bg = jnp.zeros((1, LANES), F32).at[0, :2 * nh].set(b_gate)
    xm3 = xm.reshape(b, s, di)
    blk = pl.BlockSpec((1, s, cw), lambda bi, c: (bi, 0, c))
    outs = pl.pallas_call(
        functools.partial(_mlstm_qkv_kernel, ts=ts, k_scale=dh ** -0.5),
        grid=(b, nchunk),
        in_specs=[blk,
                  pl.BlockSpec((1, MLSTM_CONV, cw), lambda bi, c: (c, 0, 0)),
                  pl.BlockSpec((1, 1, cw), lambda bi, c: (c, 0, 0)),
                  pl.BlockSpec((1, cw, 2 * cw), lambda bi, c: (c, 0, 0)),
                  pl.BlockSpec((1, cw, cw), lambda bi, c: (c, 0, 0)),
                  pl.BlockSpec((cw, LANES), lambda bi, c: (c, 0)),
                  pl.BlockSpec((cw, LANES), lambda bi, c: (nchunk + c, 0)),
                  pl.BlockSpec((cw, LANES), lambda bi, c: (2 * nchunk + c, 0)),
                  pl.BlockSpec((1, LANES), lambda bi, c: (0, 0))],
        out_specs=[blk, blk, pl.BlockSpec((1, cw, s), lambda bi, c: (bi, c, 0)), blk,
                   pl.BlockSpec((1, s, LANES), lambda bi, c: (bi, 0, 0))],
        out_shape=[jax.ShapeDtypeStruct((b, s, di), BF16)] * 2 + [jax.ShapeDtypeStruct((b, di, s), BF16)]
        + [jax.ShapeDtypeStruct((b, s, di), BF16), jax.ShapeDtypeStruct((b, s, LANES), F32)],
        scratch_shapes=[pltpu.VMEM((s + SUBLANES, cw), F32)],
        compiler_params=_cparams("arbitrary", "arbitrary"),
        name="mlstm_qkv",
    )(xm3, conv_w.reshape(MLSTM_CONV, nchunk, cw).transpose(1, 0, 2), conv_b.reshape(nchunk, 1, cw),
      wqk, wv, wg, wg, wg, bg)
    return outs


def _log_sigmoid(x):
    return jnp.minimum(x, 0.0) - jnp.log1p(jnp.exp(-jnp.abs(x)))


def _split3(x):
    hi = x.astype(BF16)
    r = x - hi.astype(F32)
    mid = r.astype(BF16)
    lo = (r - mid.astype(F32)).astype(BF16)
    return hi, mid, lo


def _mlstm_cell_kernel(q_ref, kt_ref, v_ref, gc_ref, gr_ref, ng_ref, o_ref, ct_sc, n_sc, m_sc, *, nh):
    c = pl.program_id(1)
    L = q_ref.shape[1]
    dh = q_ref.shape[2] // nh

    @pl.when(c == 0)
    def _():
        ct_sc[...] = jnp.zeros(ct_sc.shape, F32)
        n_sc[...] = jnp.zeros(n_sc.shape, F32)
        m_sc[...] = jnp.zeros(m_sc.shape, F32)

    t_idx = lax.broadcasted_iota(jnp.int32, (L, L), 0)
    s_idx = lax.broadcasted_iota(jnp.int32, (L, L), 1)
    causal = s_idx <= t_idx
    tri = causal.astype(BF16)
    tri_t = (t_idx <= s_idx).astype(F32)
    gc = gc_ref[0]
    gr = gr_ref[0]
    b_cols = sum(jnp.dot(tri, part, preferred_element_type=F32) for part in _split3(_log_sigmoid(gc)))
    b_rows = jnp.dot(_log_sigmoid(gr), tri_t, preferred_element_type=F32, precision=HIGHEST)
    lane = lax.broadcasted_iota(jnp.int32, gc.shape, 1)

    for hl in range(nh):
        h = hl
        cols = slice(hl * dh, (hl + 1) * dh)
        i_col = jnp.sum(jnp.where(lane == h, gc, 0.0), axis=1, keepdims=True)
        b_col = jnp.sum(jnp.where(lane == h + nh, b_cols, 0.0), axis=1, keepdims=True)
        i_row = gr[h:h + 1, :]
        b_row = b_rows[nh + h:nh + h + 1, :]
        u_row = i_row - b_row
        u_col = i_col - b_col

        m_prev = m_sc[hl, 0:1, 0:1]
        dlog = jnp.where(causal, b_col + u_row, NEG_BIG)
        g = b_col + m_prev
        m_t = jnp.maximum(g, jnp.max(dlog, axis=1, keepdims=True))
        qb = q_ref[0, :, cols]
        ktb = kt_ref[0, cols, :]
        vb = v_ref[0, :, cols]
        w = jnp.exp(dlog - m_t) * jnp.dot(qb, ktb, preferred_element_type=F32)
        inter = jnp.exp(g - m_t)
        ct = ct_sc[hl]
        num = (jnp.dot(w.astype(BF16), vb, preferred_element_type=F32)
               + inter * jnp.dot(qb, ct.astype(BF16), preferred_element_type=F32))
        den = (jnp.sum(w, axis=1, keepdims=True)
               + inter * jnp.dot(qb, n_sc[hl].astype(BF16), preferred_element_type=F32))
        scale = 1.0 / jnp.maximum(jnp.abs(den), jnp.exp(-m_t))
        hh = num * jnp.tile(scale, (1, dh // LANES))

        mu = jnp.mean(hh, axis=1, keepdims=True)
        cen = hh - mu
        var = jnp.mean(cen * cen, axis=1, keepdims=True)
        o_ref[0, :, cols] = (cen * lax.rsqrt(var + NORM_EPS) * ng_ref[h]).astype(BF16)

        b_last = b_row[:, L - 1:L]
        m_new = jnp.maximum(b_last + m_prev, jnp.max(b_last + u_row, axis=1, keepdims=True))
        decay = jnp.exp(b_last + m_prev - m_new)
        ws_col = jnp.exp(b_last - m_new + u_col).astype(BF16)
        ct_sc[hl] = decay * ct + jnp.dot(ktb, vb * ws_col, preferred_element_type=F32)
        n_sc[hl] = decay * n_sc[hl] + jnp.dot(ktb, jnp.broadcast_to(ws_col, (L, LANES)),
                                              preferred_element_type=F32)
        m_sc[hl] = jnp.broadcast_to(m_new, m_sc.shape[1:])


def _mlstm_cell(q, kt, v, gates, norm_g, b, s):
    di = q.shape[2]
    nh = MLSTM_HEADS
    dh = di // nh
    L = min(256, s)
    gates_t = jnp.transpose(gates[:, :, :SUBLANES], (0, 2, 1))
    blk = pl.BlockSpec((1, L, di), lambda bi, c: (bi, c, 0))
    return pl.pallas_call(
        functools.partial(_mlstm_cell_kernel, nh=nh),
        grid=(b, s // L),
        in_specs=[blk,
                  pl.BlockSpec((1, di, L), lambda bi, c: (bi, 0, c)),
                  blk,
                  pl.BlockSpec((1, L, LANES), lambda bi, c: (bi, c, 0)),
                  pl.BlockSpec((1, SUBLANES, L), lambda bi, c: (bi, 0, c)),
                  pl.BlockSpec((nh, 1, dh), lambda bi, c: (0, 0, 0))],
        out_specs=blk,
        out_shape=jax.ShapeDtypeStruct((b, s, di), BF16),
        scratch_shapes=[pltpu.VMEM((nh, dh, dh), F32), pltpu.VMEM((nh, dh, LANES), F32),
                        pltpu.VMEM((nh, SUBLANES, LANES), F32)],
        compiler_params=_cparams("arbitrary", "arbitrary"),
        name="mlstm_cell",
    )(q, kt, v, gates, gates_t, norm_g.reshape(nh, 1, dh))


def _mlstm_out_kernel(hn_ref, xc_ref, z_ref, skip_ref, w_ref, x_ref, mod_ref, g_ref, wr_ref, br_ref,
                      x_out_ref, h_out_ref, route_ref, meta_ref, *, n_experts):
    inner = ((hn_ref[...].astype(F32) + skip_ref[...] * xc_ref[...].astype(F32))
             * _silu(z_ref[...].astype(F32)))
    h4 = _mixer_out_core(inner.astype(BF16), w_ref, x_ref, mod_ref, g_ref, x_out_ref)
    h_hi = h4.astype(BF16)
    h_out_ref[...] = h_hi

    tm = h4.shape[0]
    lane = lax.broadcasted_iota(jnp.int32, (tm, LANES), 1)
    h_lo = (h4 - h_hi.astype(F32)).astype(BF16)
    w_hi = wr_ref[...].astype(BF16)
    w_lo = (wr_ref[...] - w_hi.astype(F32)).astype(BF16)
    hi_terms = jnp.dot(h_hi, jnp.concatenate([w_hi, w_lo], axis=1), preferred_element_type=F32)
    logits = (hi_terms[:, :LANES] + hi_terms[:, LANES:] + jnp.dot(h_lo, w_hi, preferred_element_type=F32)
              + br_ref[...])
    logits = jnp.where(lane < n_experts, logits, NEG_BIG)
    ex = jnp.exp(logits - jnp.max(logits, axis=1, keepdims=True))
    probs = ex / jnp.sum(ex, axis=1, keepdims=True)
    probs = jnp.where(lane < n_experts, probs, -1.0)
    lane_f = lane.astype(F32)
    p0 = jnp.max(probs, axis=1, keepdims=True)
    e0 = jnp.min(jnp.where(probs == p0, lane_f, float(LANES)), axis=1, keepdims=True)
    rest = jnp.where(lane_f == e0, -1.0, probs)
    p1 = jnp.max(rest, axis=1, keepdims=True)
    e1 = jnp.min(jnp.where(rest == p1, lane_f, float(LANES)), axis=1, keepdims=True)
    tot = p0 + p1
    sel0 = lane_f == e0
    sel1 = lane_f == e1
    sel = jnp.where(sel0 | sel1, 1.0, 0.0)
    r_idx = lax.broadcasted_iota(jnp.int32, (tm, tm), 0)
    c_idx = lax.broadcasted_iota(jnp.int32, (tm, tm), 1)
    before = (c_idx < r_idx).astype(BF16)
    cum = jnp.dot(before, sel.astype(BF16), preferred_element_type=F32)
    run8 = jnp.floor((jnp.sum(sel, axis=0, keepdims=True) + 7.0) * 0.125)
    e_r = lax.broadcasted_iota(jnp.int32, (LANES, LANES), 0)
    e_c = lax.broadcasted_iota(jnp.int32, (LANES, LANES), 1)
    seg8 = jnp.dot(jnp.broadcast_to(run8, (SUBLANES, LANES)).astype(BF16), (e_r < e_c).astype(BF16),
                   preferred_element_type=F32)[0:1, :]
    pos = cum + 8.0 * seg8
    pos0 = jnp.sum(jnp.where(sel0, pos, 0.0), axis=1, keepdims=True)
    pos1 = jnp.sum(jnp.where(sel1, pos, 0.0), axis=1, keepdims=True)
    vals = (p0 / tot, p1 / tot, e0, e1, pos0, pos1)
    route = jnp.zeros((tm, LANES), F32)
    for idx, val in enumerate(vals):
        route = jnp.where(lane == idx, val, route)
    route_ref[...] = route
    meta = jnp.where(lane[0:1, :] < n_experts, run8, 0.0)
    meta = jnp.where((lane[0:1, :] >= n_experts) & (lane[0:1, :] < 2 * n_experts),
                     pltpu.roll(8.0 * seg8, n_experts, 1), meta)
    meta_ref[0] = jnp.broadcast_to(meta, meta_ref.shape[1:]).astype(jnp.int32)


def _mlstm_out(hn, xc, z, skip, w_bf16, x2d, mod, g, w_router, b_router, s):
    t, d = x2d.shape
    di = hn.shape[1]
    e = w_router.shape[1]
    tm = _token_tile(s)
    tpb = s // tm
    n_tok_tiles = t // tm
    wr = jnp.zeros((d, LANES), F32).at[:, :e].set(w_router)
    br = jnp.zeros((1, LANES), F32).at[0, :e].set(b_router)
    tok = lambda n: pl.BlockSpec((tm, n), lambda i: (i, 0))
    return pl.pallas_call(
        functools.partial(_mlstm_out_kernel, n_experts=e),
        grid=(n_tok_tiles,),
        in_specs=[tok(di), tok(di), tok(di),
                  pl.BlockSpec((1, di), lambda i: (0, 0)),
                  pl.BlockSpec(w_bf16.shape, lambda i: (0, 0)),
                  tok(d),
                  pl.BlockSpec((1, 6, d), lambda i: (i // tpb, 0, 0)),
                  pl.BlockSpec((4, d), lambda i: (0, 0)),
                  pl.BlockSpec((d, LANES), lambda i: (0, 0)),
                  pl.BlockSpec((1, LANES), lambda i: (0, 0))],
        out_specs=[tok(d), tok(d), tok(LANES), pl.BlockSpec((1, SUBLANES, LANES), lambda i: (i, 0, 0))],
        out_shape=[jax.ShapeDtypeStruct((t, d), F32), jax.ShapeDtypeStruct((t, d), BF16),
                   jax.ShapeDtypeStruct((t, LANES), F32),
                   jax.ShapeDtypeStruct((n_tok_tiles, SUBLANES, LANES), jnp.int32)],
        compiler_params=_cparams("arbitrary"),
        name="mlstm_out_router",
    )(hn, xc, z, skip.reshape(1, di), w_bf16, x2d, mod, g, wr, br)


def _dispatch_kernel(start_ref, run_ref, seg_ref, fill_ref, h_ref, route_ref, xs_ref, stage_sc, inflight_sm, sem,
                     *, n_experts):
    i = pl.program_id(0)
    tm = h_ref.shape[0]
    r_stage = stage_sc.shape[0] - DISPATCH_ROWS

    def copy(src_row, dst_row, rows):
        return pltpu.make_async_copy(stage_sc.at[pl.ds(src_row, rows)], xs_ref.at[pl.ds(dst_row, rows)], sem)

    def wait_copies(n, rows):
        def body(_, carry):
            copy(0, 0, rows).wait()
            return carry
        lax.fori_loop(0, n, body, 0)

    def zero_fill(first_row, n, rows):
        def body(j, carry):
            copy(0, pl.multiple_of(first_row + j * rows, SUBLANES), rows).start()
            return carry
        lax.fori_loop(0, n, body, 0)
        wait_copies(n, rows)

    @pl.when(i == 0)
    def _():
        inflight_sm[0] = 0
        stage_sc[r_stage:, :] = jnp.zeros((DISPATCH_ROWS, stage_sc.shape[1]), F32)

    route_t = jnp.transpose(route_ref[...])
    slot_row = lax.broadcasted_iota(jnp.int32, (r_stage, tm), 0).astype(F32)
    onehot = jnp.where((route_t[4:5, :] == slot_row) | (route_t[5:6, :] == slot_row), 1.0, 0.0).astype(BF16)
    sorted_rows = jnp.dot(onehot, h_ref[...], preferred_element_type=F32)

    wait_copies(inflight_sm[0], DISPATCH_ROWS)
    stage_sc[0:r_stage, :] = sorted_rows
    issued = 0
    for e in range(n_experts):
        rows = run_ref[i * n_experts + e] * SUBLANES
        start = start_ref[i * n_experts + e]
        seg = seg_ref[i * n_experts + e]
        for k in range(tm // DISPATCH_ROWS):
            @pl.when(rows > k * DISPATCH_ROWS)
            def _(seg=seg, start=start, k=k):
                copy(pl.multiple_of(seg + k * DISPATCH_ROWS, SUBLANES),
                     pl.multiple_of(start + k * DISPATCH_ROWS, SUBLANES), DISPATCH_ROWS).start()
        issued = issued + (rows + DISPATCH_ROWS - 1) // DISPATCH_ROWS
    inflight_sm[0] = issued

    @pl.when(i == pl.num_programs(0) - 1)
    def _():
        wait_copies(inflight_sm[0], DISPATCH_ROWS)
        stage_sc[0:DISPATCH_ROWS, :] = jnp.zeros((DISPATCH_ROWS, stage_sc.shape[1]), F32)
        for e in range(n_experts):
            zero_fill(fill_ref[e], fill_ref[n_experts + e], SUBLANES)
        zero_fill(fill_ref[2 * n_experts], fill_ref[2 * n_experts + 1], DISPATCH_ROWS)


def _dispatch(h4, route, starts, runs, segs, fill, n_rows, s, n_experts):
    t, d = h4.shape
    tm = _token_tile(s)
    r_stage = TOP_K * tm + SUBLANES * n_experts
    grid_spec = pltpu.PrefetchScalarGridSpec(
        num_scalar_prefetch=4,
        grid=(t // tm,),
        in_specs=[pl.BlockSpec((tm, d), lambda i, *_: (i, 0)),
                  pl.BlockSpec((tm, LANES), lambda i, *_: (i, 0))],
        out_specs=pl.BlockSpec(memory_space=pl.ANY),
        scratch_shapes=[pltpu.VMEM((r_stage + DISPATCH_ROWS, d), F32), pltpu.SMEM((1,), jnp.int32),
                        pltpu.SemaphoreType.DMA(())],
    )
    return pl.pallas_call(
        functools.partial(_dispatch_kernel, n_experts=n_experts),
        grid_spec=grid_spec,
        out_shape=jax.ShapeDtypeStruct((n_rows, d), F32),
        compiler_params=_cparams("arbitrary"),
        name="moe_dispatch",
    )(starts, runs, segs, fill, h4, route)


def _expert_ffn_kernel(te_ref, nreal_ref, xs_ref, wg_ref, wu_ref, wd_ref, ys_ref, xb_sc, acc_sc):
    i = pl.program_id(0)
    j = pl.program_id(1)
    last = pl.num_programs(1) - 1
    real = i < nreal_ref[0]

    @pl.when(real & (j == 0))
    def _():
        xb_sc[...] = xs_ref[...].astype(BF16)
        acc_sc[...] = jnp.zeros(acc_sc.shape, F32)

    @pl.when(real)
    def _():
        acc_sc[...] += _swiglu_partial(xb_sc[...], wg_ref.shape[2], lambda c: wg_ref[0, :, c],
                                       lambda c: wu_ref[0, :, c], lambda c: wd_ref[0, c, :])

    @pl.when(real & (j == last))
    def _():
        ys_ref[...] = acc_sc[...]

    @pl.when(jnp.logical_not(real) & (j == last))
    def _():
        ys_ref[...] = jnp.zeros(ys_ref.shape, F32)


def _expert_ffn(xs, tile_expert, n_real, wg, wu, wd):
    p, d = xs.shape
    f = wg.shape[2]
    tm = EXPERT_TILE
    tf = _ffn_chunk(f)
    nj = f // tf

    def row_idx(i, j, te, nr):
        return (jnp.minimum(i, nr[0] - 1), 0)

    def col_j(i, j, nr):
        return jnp.where(i < nr[0], j, nj - 1)

    grid_spec = pltpu.PrefetchScalarGridSpec(
        num_scalar_prefetch=2,
        grid=(p // tm, nj),
        in_specs=[pl.BlockSpec((tm, d), row_idx),
                  pl.BlockSpec((1, d, tf), lambda i, j, te, nr: (te[i], 0, col_j(i, j, nr))),
                  pl.BlockSpec((1, d, tf), lambda i, j, te, nr: (te[i], 0, col_j(i, j, nr))),
                  pl.BlockSpec((1, tf, d), lambda i, j, te, nr: (te[i], col_j(i, j, nr), 0))],
        out_specs=pl.BlockSpec((tm, d), lambda i, j, te, nr: (i, 0)),
        scratch_shapes=[pltpu.VMEM((tm, d), BF16), pltpu.VMEM((tm, d), F32)],
    )
    return pl.pallas_call(
        _expert_ffn_kernel,
        grid_spec=grid_spec,
        out_shape=jax.ShapeDtypeStruct((p, d), F32),
        compiler_params=_cparams("arbitrary", "arbitrary"),
        name="moe_expert_ffn",
    )(tile_expert, n_real, xs, wg, wu, wd)


def _combine_kernel(start_ref, run_ref, seg_ref, ys_ref, route_ref, x_ref, mod_ref, g_ref, o_ref, buf_sc, y_sc,
                    col_sc, src_sm, exp_sm, off_sm, count_sm, sems, *, n_experts):
    i = pl.program_id(0)
    tm = x_ref.shape[0]
    d = x_ref.shape[1]
    max_chunks = buf_sc.shape[1]

    def chunk_copy(slot, ci):
        return pltpu.make_async_copy(
            ys_ref.at[pl.ds(pl.multiple_of(src_sm[slot * max_chunks + ci], SUBLANES), DISPATCH_ROWS)],
            buf_sc.at[slot, ci], sems.at[slot, ci])

    def fetch(tile, slot):
        n_chunks = 0
        for e in range(n_experts):
            start = start_ref[tile * n_experts + e]
            rows = run_ref[tile * n_experts + e] * SUBLANES
            seg = seg_ref[tile * n_experts + e]
            for k in range(tm // DISPATCH_ROWS):
                @pl.when(rows > k * DISPATCH_ROWS)
                def _(start=start, seg=seg, k=k, ci=n_chunks + k):
                    src_sm[slot * max_chunks + ci] = start + k * DISPATCH_ROWS
                    exp_sm[slot * max_chunks + ci] = e
                    off_sm[slot * max_chunks + ci] = seg + k * DISPATCH_ROWS
                    chunk_copy(slot, ci).start()
            n_chunks = n_chunks + (rows + DISPATCH_ROWS - 1) // DISPATCH_ROWS
        count_sm[slot] = n_chunks

    slot = i % 2

    @pl.when(i == 0)
    def _():
        fetch(i, slot)

    @pl.when(i + 1 < pl.num_programs(0))
    def _():
        fetch(i + 1, 1 - slot)

    route = route_ref[...]
    for idx in range(col_sc.shape[0]):
        col_sc[idx] = jnp.broadcast_to(route[:, idx:idx + 1], col_sc.shape[1:])
    lane = lax.broadcasted_iota(jnp.int32, (tm, DISPATCH_ROWS), 1).astype(F32)
    y_sc[...] = jnp.zeros(y_sc.shape, F32)

    def body(ci, carry):
        chunk_copy(slot, ci).wait()
        ef = exp_sm[slot * max_chunks + ci].astype(F32)
        target = lane + off_sm[slot * max_chunks + ci].astype(F32)
        first = col_sc[2] == ef
        spread = jnp.where((first & (col_sc[4] == target)) | ((col_sc[3] == ef) & (col_sc[5] == target)),
                           1.0, 0.0).astype(BF16)
        rows_out = jnp.dot(spread, buf_sc[slot, ci].astype(BF16), preferred_element_type=F32)
        weight = jnp.where(first, col_sc[0], col_sc[1])
        y_sc[...] += jnp.tile(weight, (1, d // LANES)) * rows_out
        return carry

    lax.fori_loop(0, count_sm[slot], body, 0)
    o_ref[...] = x_ref[...] + mod_ref[0, 5:6, :] * _rms(y_sc[...], g_ref[3:4, :])


def _combine(ys, starts, runs, segs, route, x2d, mod, g, s, n_experts):
    t, d = x2d.shape
    tm = _token_tile(s)
    tpb = s // tm
    max_chunks = TOP_K * tm // DISPATCH_ROWS + n_experts
    grid_spec = pltpu.PrefetchScalarGridSpec(
        num_scalar_prefetch=3,
        grid=(t // tm,),
        in_specs=[pl.BlockSpec(memory_space=pl.ANY),
                  pl.BlockSpec((tm, LANES), lambda i, *_: (i, 0)),
                  pl.BlockSpec((tm, d), lambda i, *_: (i, 0)),
                  pl.BlockSpec((1, 6, d), lambda i, *_: (i // tpb, 0, 0)),
                  pl.BlockSpec((4, d), lambda i, *_: (0, 0))],
        out_specs=pl.BlockSpec((tm, d), lambda i, *_: (i, 0)),
        scratch_shapes=[pltpu.VMEM((2, max_chunks, DISPATCH_ROWS, d), F32), pltpu.VMEM((tm, d), F32),
                        pltpu.VMEM((6, tm, LANES), F32),
                        pltpu.SMEM((2 * max_chunks,), jnp.int32), pltpu.SMEM((2 * max_chunks,), jnp.int32),
                        pltpu.SMEM((2 * max_chunks,), jnp.int32), pltpu.SMEM((2,), jnp.int32),
                        pltpu.SemaphoreType.DMA((2, max_chunks))],
    )
    return pl.pallas_call(
        functools.partial(_combine_kernel, n_experts=n_experts),
        grid_spec=grid_spec,
        out_shape=jax.ShapeDtypeStruct((t, d), F32),
        compiler_params=_cparams("arbitrary"),
        name="moe_combine",
    )(starts, runs, segs, ys, route, x2d, mod, g)


def _moe(h4, route, meta, x2d, mod, g, wg, wu, wd, s):
    t, d = x2d.shape
    e = wg.shape[0]
    tm = EXPERT_TILE
    n_tok_tiles = t // _token_tile(s)
    n_tiles = (TOP_K * t + (SUBLANES * n_tok_tiles + DISPATCH_ROWS) * e) // tm + e
    runs = meta[:, 0, :e]
    segs = meta[:, 0, e:2 * e]
    used = SUBLANES * jnp.sum(runs, axis=0)
    tiles_e = jnp.where(used > 0, (used + DISPATCH_ROWS + tm - 1) // tm, 0)
    ends = jnp.cumsum(tiles_e)
    off = (ends - tiles_e) * tm
    starts = off[None, :] + SUBLANES * (jnp.cumsum(runs, axis=0) - runs)
    n_real = ends[e - 1:e].astype(jnp.int32)
    fill = jnp.concatenate([off + used, (tiles_e * tm - used) // SUBLANES,
                            n_real * tm, (n_tiles - n_real) * (tm // DISPATCH_ROWS)]).astype(jnp.int32)
    tile_ids = jnp.arange(n_tiles, dtype=jnp.int32)
    tile_expert = jnp.minimum(jnp.sum(tile_ids[:, None] >= ends[None, :], axis=1), e - 1)
    last_expert = jnp.minimum(jnp.sum(n_real[0] - 1 >= ends), e - 1)
    tile_expert = jnp.where(tile_ids < n_real[0], tile_expert, last_expert).astype(jnp.int32)
    flat = lambda a: a.reshape(-1).astype(jnp.int32)
    xs = _dispatch(h4, route, flat(starts), flat(runs), flat(segs), fill, n_tiles * tm, s, e)
    ys = _expert_ffn(xs, tile_expert, n_real, wg, wu, wd)
    return _combine(ys, flat(starts), flat(runs), flat(segs), route, x2d, mod, g, s, e)


def kernel(x, c, mod_w, mod_b, norm_g, attn_w_in, attn_w_out, attn_lambda, attn_subln, ffn_w_gate, ffn_w_up, ffn_w_down, mlstm_w_in, mlstm_conv_w, mlstm_conv_b, mlstm_w_q, mlstm_w_k, mlstm_w_v, mlstm_w_gate, mlstm_b_gate, mlstm_skip, mlstm_norm, mlstm_w_out, moe_w_router, moe_b_router, moe_w_gate, moe_w_up, moe_w_down):
    b, s, d = x.shape
    depth = mod_w.shape[0]
    mod_all = _modulation(c, mod_w, mod_b)
    x2d = x.reshape(b * s, d)
    da = d // (2 * ATTN_HEADS)
    for i in range(depth):
        mod = mod_all[i]
        g = norm_g[i]
        j = i // N_MIXERS
        if i % N_MIXERS == 0:
            lambda_init = 0.8 - 0.6 * math.exp(-0.3 * i)
            q, k, v = _in_proj(x2d, mod, g, attn_w_in[j].astype(BF16), 3, (LOG2E * da ** -0.5, 1.0, 1.0), s,
                               "attn_in_proj")
            o = _diff_attention(q, k, v, attn_lambda[j], attn_subln[j], lambda_init, b, s)
            x2d = _attn_out_ffn(o, attn_w_out[j].astype(BF16), ffn_w_gate[j].astype(BF16),
                                ffn_w_up[j].astype(BF16), ffn_w_down[j].astype(BF16), x2d, mod, g, s)
        else:
            xm, z = _in_proj(x2d, mod, g, mlstm_w_in[j].astype(BF16), 2, (1.0, 1.0), s, "mlstm_in_proj")
            xc, q, kt, v, gates = _mlstm_qkv(xm, mlstm_conv_w[j], mlstm_conv_b[j], mlstm_w_q[j], mlstm_w_k[j],
                                             mlstm_w_v[j], mlstm_w_gate[j], mlstm_b_gate[j], b, s)
            hn = _mlstm_cell(q, kt, v, gates, mlstm_norm[j], b, s)
            di = hn.shape[2]
            x2d, h4, route, meta = _mlstm_out(hn.reshape(b * s, di), xc.reshape(b * s, di), z, mlstm_skip[j],
                                              mlstm_w_out[j].astype(BF16), x2d, mod, g,
                                              moe_w_router[j], moe_b_router[j], s)
            x2d = _moe(h4, route, meta, x2d, mod, g, moe_w_gate[j].astype(BF16), moe_w_up[j].astype(BF16),
                       moe_w_down[j].astype(BF16), s)
    return x2d.reshape(b, s, d)
```

```python
import functools
import math

import jax
import jax.numpy as jnp
from jax import lax
from jax.experimental import pallas as pl
from jax.experimental.pallas import tpu as pltpu

F32 = jnp.float32
BF16 = jnp.bfloat16
HIGHEST = lax.Precision.HIGHEST

CHUNK = 64
ATTN_HEADS = 8
ALIBI_MAX_BIAS = 8.0
MLSTM_HEADS = 4
MLSTM_CONV = 4
QKV_BLOCK = 4
TOP_K = 2
NORM_EPS = 1e-6
N_MIXERS = 2

LANES = 128
SUBLANES = 8
VMEM_LIMIT_BYTES = 56 * 1024 * 1024
FFN_SUBCHUNK = 256
EXPERT_TILE = 512
DISPATCH_ROWS = 128
COMBINE_ROWS = 256
NEG_BIG = -1e30
LOG2E = math.log2(math.e)


def _cparams(*sem):
    return pltpu.CompilerParams(dimension_semantics=sem, vmem_limit_bytes=VMEM_LIMIT_BYTES)


def _rms(x, g):
    return x * lax.rsqrt(jnp.mean(x * x, axis=-1, keepdims=True) + NORM_EPS) * g


def _silu(x):
    return x * jax.nn.sigmoid(x)


def _token_tile(s):
    return min(512, s)


def _mod_kernel(c_ref, w_ref, b_ref, o_ref):
    cond = _silu(c_ref[...])
    o_ref[0] = jnp.dot(cond, w_ref[0], preferred_element_type=F32, precision=HIGHEST) + b_ref[0]


def _modulation(c, mod_w, mod_b):
    depth, d, n = mod_w.shape
    b = c.shape[0]
    tn = 1536 if n % 1536 == 0 else n
    out = pl.pallas_call(
        _mod_kernel,
        grid=(depth, n // tn),
        in_specs=[pl.BlockSpec((b, d), lambda i, j: (0, 0)),
                  pl.BlockSpec((1, d, tn), lambda i, j: (i, 0, j)),
                  pl.BlockSpec((1, 1, tn), lambda i, j: (i, 0, j))],
        out_specs=pl.BlockSpec((1, b, tn), lambda i, j: (i, 0, j)),
        out_shape=jax.ShapeDtypeStruct((depth, b, n), F32),
        compiler_params=_cparams("arbitrary", "arbitrary"),
        name="modulation",
    )(c, mod_w, mod_b.reshape(depth, 1, n))
    return out.reshape(depth, b, 6, d)


def _in_proj_kernel(x_ref, mod_ref, g_ref, w_ref, *o_refs, scales):
    x = x_ref[...]
    h = _rms(x, g_ref[0:1, :]) * (1.0 + mod_ref[0, 1:2, :]) + mod_ref[0, 0:1, :]
    hb = h.astype(BF16)
    n = o_refs[0].shape[1]
    for idx, o_ref in enumerate(o_refs):
        r = jnp.dot(hb, w_ref[:, idx * n:(idx + 1) * n], preferred_element_type=F32)
        if scales[idx] != 1.0:
            r = r * scales[idx]
        o_ref[...] = r.astype(BF16)


def _in_proj(x2d, mod, g, w_bf16, n_out, scales, s, name):
    t, d = x2d.shape
    tm = _token_tile(s)
    tpb = s // tm
    n = w_bf16.shape[1] // n_out
    return pl.pallas_call(
        functools.partial(_in_proj_kernel, scales=scales),
        grid=(t // tm,),
        in_specs=[pl.BlockSpec((tm, d), lambda i: (i, 0)),
                  pl.BlockSpec((1, 6, d), lambda i: (i // tpb, 0, 0)),
                  pl.BlockSpec((4, d), lambda i: (0, 0)),
                  pl.BlockSpec(w_bf16.shape, lambda i: (0, 0))],
        out_specs=[pl.BlockSpec((tm, n), lambda i: (i, 0))] * n_out,
        out_shape=[jax.ShapeDtypeStruct((t, n), BF16)] * n_out,
        compiler_params=_cparams("arbitrary"),
        name=name,
    )(x2d, mod, g, w_bf16)


def _attn_kernel(slopes_ref, lam_ref, subln_ref, q_ref, k_ref, v_ref, o_ref, m_sc, acc_sc, bias_sc, vext_sc,
                 s_sc, smax_sc, *, tq, tk, lambda_init):
    slope = slopes_ref[pl.program_id(1)]
    hd = q_ref.shape[2]
    half = hd // 2
    n_q = q_ref.shape[1] // tq

    vext_sc[:, :hd] = v_ref[0]
    vext_sc[:, hd:] = jnp.ones((vext_sc.shape[0], hd), BF16)
    row = lax.broadcasted_iota(jnp.int32, (tq, tk), 0)
    col = lax.broadcasted_iota(jnp.int32, (tq, tk), 1)
    dist = (row - col).astype(F32)
    bias_sc[0] = -slope * dist
    bias_sc[1] = jnp.where(col // CHUNK <= row // CHUNK, -slope * jnp.abs(dist), NEG_BIG)

    lam = lam_ref[...]
    lam_full = (jnp.exp(jnp.sum(lam[0:1, :] * lam[1:2, :], axis=1, keepdims=True))
                - jnp.exp(jnp.sum(lam[2:3, :] * lam[3:4, :], axis=1, keepdims=True)) + lambda_init)

    def scores(qi, j, slot):
        q = q_ref[0, qi * tq:(qi + 1) * tq, :]
        lane = lax.broadcasted_iota(jnp.int32, q.shape, 1)
        zero = jnp.zeros_like(q)
        q2 = jnp.concatenate([jnp.where(lane < half, q, zero), jnp.where(lane >= half, q, zero)], axis=0)
        kj = k_ref[0, j * tk:(j + 1) * tk, :]
        s = lax.dot_general(q2, kj, (((1,), (1,)), ((), ())), preferred_element_type=F32)
        bias = bias_sc[1 if j == qi else 0]
        s = s + jnp.concatenate([bias, bias], axis=0)
        s_sc[slot] = s
        smax_sc[slot] = jnp.broadcast_to(jnp.max(s, axis=1, keepdims=True), smax_sc.shape[1:])

    def accumulate(qi, j, slot):
        const = 0.0 if j == qi else -slope * float((qi - j) * tq)
        m_prev = m_sc[...]
        m_next = jnp.maximum(m_prev, smax_sc[slot] + const)
        alpha = jnp.exp2(m_prev - m_next)
        p = jnp.exp2(s_sc[slot] - jnp.tile(m_next - const, (1, tk // LANES)))
        pv = jnp.dot(p.astype(BF16), vext_sc[j * tk:(j + 1) * tk, :], preferred_element_type=F32)
        acc_sc[...] = jnp.tile(alpha, (1, 2)) * acc_sc[...] + pv
        m_sc[...] = m_next

    def finalize(qi):
        acc = acc_sc[...]
        o_all = acc[:, :hd] / acc[:, hd:]
        o = o_all[:tq] - lam_full * o_all[tq:]
        o = _rms(o, subln_ref[...]) * (1.0 - lambda_init)
        o_ref[0, qi * tq:(qi + 1) * tq, :] = o.astype(BF16)

    blocks = [(qi, j) for qi in range(n_q) for j in range(qi + 1)]
    scores(*blocks[0], 0)
    for n, (qi, j) in enumerate(blocks):
        if n + 1 < len(blocks):
            scores(*blocks[n + 1], (n + 1) % 2)
        if j == 0:
            m_sc[...] = jnp.full(m_sc.shape, NEG_BIG, F32)
            acc_sc[...] = jnp.zeros(acc_sc.shape, F32)
        accumulate(qi, j, n % 2)
        if j == qi:
            finalize(qi)


def _diff_attention(q, k, v, lam, subln, lambda_init, b, s):
    d = q.shape[1]
    hd = d // ATTN_HEADS
    tq = tk = min(512, s)
    slopes = LOG2E * jnp.exp2(-ALIBI_MAX_BIAS * jnp.arange(1, ATTN_HEADS + 1, dtype=F32) / ATTN_HEADS)
    q3, k3, v3 = (a.reshape(b, s, d) for a in (q, k, v))
    grid_spec = pltpu.PrefetchScalarGridSpec(
        num_scalar_prefetch=1,
        grid=(b, ATTN_HEADS),
        in_specs=[pl.BlockSpec(lam.shape, lambda bi, h, sl: (0, 0)),
                  pl.BlockSpec((1, hd), lambda bi, h, sl: (0, 0)),
                  pl.BlockSpec((1, s, hd), lambda bi, h, sl: (bi, 0, h)),
                  pl.BlockSpec((1, s, hd), lambda bi, h, sl: (bi, 0, h)),
                  pl.BlockSpec((1, s, hd), lambda bi, h, sl: (bi, 0, h))],
        out_specs=pl.BlockSpec((1, s, hd), lambda bi, h, sl: (bi, 0, h)),
        scratch_shapes=[pltpu.VMEM((2 * tq, LANES), F32), pltpu.VMEM((2 * tq, 2 * hd), F32),
                        pltpu.VMEM((2, tq, tk), F32), pltpu.VMEM((s, 2 * hd), BF16),
                        pltpu.VMEM((2, 2 * tq, tk), F32), pltpu.VMEM((2, 2 * tq, LANES), F32)],
    )
    o = pl.pallas_call(
        functools.partial(_attn_kernel, tq=tq, tk=tk, lambda_init=lambda_init),
        grid_spec=grid_spec,
        out_shape=jax.ShapeDtypeStruct((b, s, d), BF16),
        compiler_params=_cparams("arbitrary", "arbitrary"),
        name="diff_attention",
    )(slopes, lam, subln.reshape(1, hd), q3, k3, v3)
    return o.reshape(b * s, d)


def _mixer_out_core(inp_bf16, w_ref, x_ref, mod_ref, g_ref, x_out_ref):
    y = jnp.dot(inp_bf16, w_ref[...], preferred_element_type=F32)
    x1 = x_ref[...] + mod_ref[0, 2:3, :] * _rms(y, g_ref[1:2, :])
    x_out_ref[...] = x1
    return _rms(x1, g_ref[2:3, :]) * (1.0 + mod_ref[0, 4:5, :]) + mod_ref[0, 3:4, :]


def _swiglu_partial(h, width, wg, wu, wd):
    out = None
    for c0 in range(0, width, FFN_SUBCHUNK):
        c = slice(c0, min(c0 + FFN_SUBCHUNK, width))
        gate = jnp.dot(h, wg(c), preferred_element_type=F32)
        up = jnp.dot(h, wu(c), preferred_element_type=F32)
        part = jnp.dot((_silu(gate) * up).astype(BF16), wd(c), preferred_element_type=F32)
        out = part if out is None else out + part
    return out


def _attn_out_ffn_kernel(o_ref, wo_ref, x_ref, mod_ref, g_ref, wg_ref, wu_ref, wd_ref, out_ref,
                         x1_sc, h2_sc, acc_sc):
    j = pl.program_id(1)

    @pl.when(j == 0)
    def _():
        h2_sc[...] = _mixer_out_core(o_ref[...], wo_ref, x_ref, mod_ref, g_ref, x1_sc).astype(BF16)
        acc_sc[...] = jnp.zeros(acc_sc.shape, F32)

    acc_sc[...] += _swiglu_partial(h2_sc[...], wg_ref.shape[1], lambda c: wg_ref[:, c], lambda c: wu_ref[:, c],
                                   lambda c: wd_ref[c, :])

    @pl.when(j == pl.num_programs(1) - 1)
    def _():
        out_ref[...] = x1_sc[...] + mod_ref[0, 5:6, :] * _rms(acc_sc[...], g_ref[3:4, :])


def _ffn_chunk(f):
    for tf in (1792, 1408, 1024, 896, 512, 256, 128):
        if f % tf == 0:
            return tf
    return f


def _attn_out_ffn(o, wo, wg, wu, wd, x2d, mod, g, s):
    t, d = x2d.shape
    f = wg.shape[1]
    tm = _token_tile(s)
    tpb = s // tm
    tf = _ffn_chunk(f)
    tok = pl.BlockSpec((tm, d), lambda i, j: (i, 0))
    return pl.pallas_call(
        _attn_out_ffn_kernel,
        grid=(t // tm, f // tf),
        in_specs=[tok,
                  pl.BlockSpec(wo.shape, lambda i, j: (0, 0)),
                  tok,
                  pl.BlockSpec((1, 6, d), lambda i, j: (i // tpb, 0, 0)),
                  pl.BlockSpec((4, d), lambda i, j: (0, 0)),
                  pl.BlockSpec((d, tf), lambda i, j: (0, j)),
                  pl.BlockSpec((d, tf), lambda i, j: (0, j)),
                  pl.BlockSpec((tf, d), lambda i, j: (j, 0))],
        out_specs=tok,
        out_shape=jax.ShapeDtypeStruct((t, d), F32),
        scratch_shapes=[pltpu.VMEM((tm, d), F32), pltpu.VMEM((tm, d), BF16), pltpu.VMEM((tm, d), F32)],
        compiler_params=_cparams("arbitrary", "arbitrary"),
        name="attn_out_dense_ffn",
    )(o, wo, x2d, mod, g, wg, wu, wd)


def _mlstm_qkv_kernel(xm_ref, cw_ref, cb_ref, wqk_ref, wv_ref, wgq_ref, wgk_ref, wgv_ref, bg_ref,
                      xc_ref, q_ref, kt_ref, v_ref, gates_ref, pad_sc, *, ts, k_scale):
    c = pl.program_id(1)
    s, cw = xm_ref.shape[1], xm_ref.shape[2]
    front = SUBLANES
    pad_sc[0:front, :] = jnp.zeros((front, cw), F32)
    pad_sc[front:front + s, :] = xm_ref[0].astype(F32)

    @pl.when(c == 0)
    def _():
        gates_ref[0] = jnp.broadcast_to(bg_ref[...], gates_ref.shape[1:])

    for r in range(s // ts):
        r0 = r * ts
        conv = cb_ref[0]
        for j in reversed(range(MLSTM_CONV)):
            start = r0 + front - (MLSTM_CONV - 1) + j
            conv = conv + pad_sc[start:start + ts, :] * cw_ref[0, j:j + 1, :]
        xc = _silu(conv)
        xcb = xc.astype(BF16)
        qk = jnp.dot(xcb, wqk_ref[0], preferred_element_type=F32)
        qb = qk[:, :cw].astype(BF16)
        kb = qk[:, cw:].astype(BF16)
        vb = jnp.dot(xm_ref[0, r0:r0 + ts, :], wv_ref[0], preferred_element_type=F32).astype(BF16)
        gates_ref[0, r0:r0 + ts, :] += (jnp.dot(qb, wgq_ref[...], preferred_element_type=F32)
                                        + jnp.dot(kb, wgk_ref[...], preferred_element_type=F32)
                                        + jnp.dot(vb, wgv_ref[...], preferred_element_type=F32))
        xc_ref[0, r0:r0 + ts, :] = xcb
        q_ref[0, r0:r0 + ts, :] = qb
        kt_ref[0, :, r0:r0 + ts] = jnp.transpose(qk[:, cw:] * k_scale).astype(BF16)
        v_ref[0, r0:r0 + ts, :] = vb


def _block_diag(w, cw):
    g, qb, _ = w.shape
    per = cw // qb
    wr = w.reshape(g // per, per, qb, qb)
    eye = jnp.eye(per, dtype=w.dtype)
    return jnp.einsum("cgio,gh->cgiho", wr, eye).reshape(g // per, cw, cw)


def _mlstm_qkv(xm, conv_w, conv_b, w_q, w_k, w_v, w_gate, b_gate, b, s):
    di = xm.shape[1]
    cw = 256
    nchunk = di // cw
    nh = MLSTM_HEADS
    dh = di // nh
    ts = min(256, s)
    wqk = jnp.concatenate([_block_diag(w_q, cw), _block_diag(w_k, cw)], axis=2).astype(BF16)
    wv = _block_diag(w_v, cw).astype(BF16)
    wg = jnp.zeros((3 * di, LANES), F32).at[:, :2 * nh].set(w_gate).astype(BF16)
    bg = jnp.zeros((1, LANES), F32).at[0, :2 * nh].set(b_gate)
    xm3 = xm.reshape(b, s, di)
    blk = pl.BlockSpec((1, s, cw), lambda bi, c: (bi, 0, c))
    outs = pl.pallas_call(
        functools.partial(_mlstm_qkv_kernel, ts=ts, k_scale=dh ** -0.5),
        grid=(b, nchunk),
        in_specs=[blk,
                  pl.BlockSpec((1, MLSTM_CONV, cw), lambda bi, c: (c, 0, 0)),
                  pl.BlockSpec((1, 1, cw), lambda bi, c: (c, 0, 0)),
                  pl.BlockSpec((1, cw, 2 * cw), lambda bi, c: (c, 0, 0)),
                  pl.BlockSpec((1, cw, cw), lambda bi, c: (c, 0, 0)),
                  pl.BlockSpec((cw, LANES), lambda bi, c: (c, 0)),
                  pl.BlockSpec((cw, LANES), lambda bi, c: (nchunk + c, 0)),
                  pl.BlockSpec((cw, LANES), lambda bi, c: (2 * nchunk + c, 0)),
                  pl.BlockSpec((1, LANES), lambda bi, c: (0, 0))],
        out_specs=[blk, blk, pl.BlockSpec((1, cw, s), lambda bi, c: (bi, c, 0)), blk,
                   pl.BlockSpec((1, s, LANES), lambda bi, c: (bi, 0, 0))],
        out_shape=[jax.ShapeDtypeStruct((b, s, di), BF16)] * 2 + [jax.ShapeDtypeStruct((b, di, s), BF16)]
        + [jax.ShapeDtypeStruct((b, s, di), BF16), jax.ShapeDtypeStruct((b, s, LANES), F32)],
        scratch_shapes=[pltpu.VMEM((s + SUBLANES, cw), F32)],
        compiler_params=_cparams("arbitrary", "arbitrary"),
        name="mlstm_qkv",
    )(xm3, conv_w.reshape(MLSTM_CONV, nchunk, cw).transpose(1, 0, 2), conv_b.reshape(nchunk, 1, cw),
      wqk, wv, wg, wg, wg, bg)
    return outs


def _log_sigmoid(x):
    return jnp.minimum(x, 0.0) - jnp.log1p(jnp.exp(-jnp.abs(x)))


def _split3(x):
    hi = x.astype(BF16)
    r = x - hi.astype(F32)
    mid = r.astype(BF16)
    lo = (r - mid.astype(F32)).astype(BF16)
    return hi, mid, lo


def _mlstm_cell_kernel(q_ref, kt_ref, v_ref, gc_ref, gr_ref, ng_ref, o_ref, ct_sc, n_sc, m_sc, *, nh):
    c = pl.program_id(1)
    L = q_ref.shape[1]
    dh = q_ref.shape[2] // nh

    @pl.when(c == 0)
    def _():
        ct_sc[...] = jnp.zeros(ct_sc.shape, F32)
        n_sc[...] = jnp.zeros(n_sc.shape, F32)
        m_sc[...] = jnp.zeros(m_sc.shape, F32)

    t_idx = lax.broadcasted_iota(jnp.int32, (L, L), 0)
    s_idx = lax.broadcasted_iota(jnp.int32, (L, L), 1)
    causal = s_idx <= t_idx
    tri = causal.astype(BF16)
    tri_t = (t_idx <= s_idx).astype(F32)
    gc = gc_ref[0]
    gr = gr_ref[0]
    b_cols = sum(jnp.dot(tri, part, preferred_element_type=F32) for part in _split3(_log_sigmoid(gc)))
    b_rows = jnp.dot(_log_sigmoid(gr), tri_t, preferred_element_type=F32, precision=HIGHEST)
    lane = lax.broadcasted_iota(jnp.int32, gc.shape, 1)

    for hl in range(nh):
        h = hl
        cols = slice(hl * dh, (hl + 1) * dh)
        i_col = jnp.sum(jnp.where(lane == h, gc, 0.0), axis=1, keepdims=True)
        b_col = jnp.sum(jnp.where(lane == h + nh, b_cols, 0.0), axis=1, keepdims=True)
        i_row = gr[h:h + 1, :]
        b_row = b_rows[nh + h:nh + h + 1, :]
        u_row = i_row - b_row
        u_col = i_col - b_col

        m_prev = m_sc[hl, 0:1, 0:1]
        dlog = jnp.where(causal, b_col + u_row, NEG_BIG)
        g = b_col + m_prev
        m_t = jnp.maximum(g, jnp.max(dlog, axis=1, keepdims=True))
        qb = q_ref[0, :, cols]
        ktb = kt_ref[0, cols, :]
        vb = v_ref[0, :, cols]
        w = jnp.exp(dlog - m_t) * jnp.dot(qb, ktb, preferred_element_type=F32)
        inter = jnp.exp(g - m_t)
        num = (jnp.dot(w.astype(BF16), vb, preferred_element_type=F32)
               + inter * jnp.dot(qb, ct_sc[hl].astype(BF16), preferred_element_type=F32))
        den = (jnp.sum(w, axis=1, keepdims=True)
               + inter * jnp.dot(qb, n_sc[hl].astype(BF16), preferred_element_type=F32))
        scale = 1.0 / jnp.maximum(jnp.abs(den), jnp.exp(-m_t))
        hh = num * jnp.tile(scale, (1, dh // LANES))

        mu = jnp.mean(hh, axis=1, keepdims=True)
        cen = hh - mu
        var = jnp.mean(cen * cen, axis=1, keepdims=True)
        o_ref[0, :, cols] = (cen * lax.rsqrt(var + NORM_EPS) * ng_ref[h]).astype(BF16)

        b_last = b_row[:, L - 1:L]
        m_new = jnp.maximum(b_last + m_prev, jnp.max(b_last + u_row, axis=1, keepdims=True))
        decay = jnp.exp(b_last + m_prev - m_new)
        ws_col = jnp.exp(b_last - m_new + u_col).astype(BF16)
        ct_sc[hl] = decay * ct_sc[hl] + jnp.dot(ktb, vb * ws_col, preferred_element_type=F32)
        n_sc[hl] = decay * n_sc[hl] + jnp.dot(ktb, jnp.broadcast_to(ws_col, (L, LANES)),
                                              preferred_element_type=F32)
        m_sc[hl] = jnp.broadcast_to(m_new, m_sc.shape[1:])


def _mlstm_cell(q, kt, v, gates, norm_g, b, s):
    di = q.shape[2]
    nh = MLSTM_HEADS
    dh = di // nh
    L = min(256, s)
    gates_t = jnp.transpose(gates[:, :, :SUBLANES], (0, 2, 1))
    blk = pl.BlockSpec((1, L, di), lambda bi, c: (bi, c, 0))
    return pl.pallas_call(
        functools.partial(_mlstm_cell_kernel, nh=nh),
        grid=(b, s // L),
        in_specs=[blk,
                  pl.BlockSpec((1, di, L), lambda bi, c: (bi, 0, c)),
                  blk,
                  pl.BlockSpec((1, L, LANES), lambda bi, c: (bi, c, 0)),
                  pl.BlockSpec((1, SUBLANES, L), lambda bi, c: (bi, 0, c)),
                  pl.BlockSpec((nh, 1, dh), lambda bi, c: (0, 0, 0))],
        out_specs=blk,
        out_shape=jax.ShapeDtypeStruct((b, s, di), BF16),
        scratch_shapes=[pltpu.VMEM((nh, dh, dh), F32), pltpu.VMEM((nh, dh, LANES), F32),
                        pltpu.VMEM((nh, SUBLANES, LANES), F32)],
        compiler_params=_cparams("arbitrary", "arbitrary"),
        name="mlstm_cell",
    )(q, kt, v, gates, gates_t, norm_g.reshape(nh, 1, dh))


def _mlstm_out_kernel(hn_ref, xc_ref, z_ref, skip_ref, w_ref, x_ref, mod_ref, g_ref, wr_ref, br_ref,
                      x_out_ref, h_out_ref, route_ref, meta_ref, *, n_experts):
    inner = ((hn_ref[...].astype(F32) + skip_ref[...] * xc_ref[...].astype(F32))
             * _silu(z_ref[...].astype(F32)))
    h4 = _mixer_out_core(inner.astype(BF16), w_ref, x_ref, mod_ref, g_ref, x_out_ref)
    h_hi = h4.astype(BF16)
    h_out_ref[...] = h_hi

    tm = h4.shape[0]
    lane = lax.broadcasted_iota(jnp.int32, (tm, LANES), 1)
    h_lo = (h4 - h_hi.astype(F32)).astype(BF16)
    w_hi = wr_ref[...].astype(BF16)
    w_lo = (wr_ref[...] - w_hi.astype(F32)).astype(BF16)
    hi_terms = jnp.dot(h_hi, jnp.concatenate([w_hi, w_lo], axis=1), preferred_element_type=F32)
    logits = (hi_terms[:, :LANES] + hi_terms[:, LANES:] + jnp.dot(h_lo, w_hi, preferred_element_type=F32)
              + br_ref[...])
    logits = jnp.where(lane < n_experts, logits, NEG_BIG)
    ex = jnp.exp(logits - jnp.max(logits, axis=1, keepdims=True))
    probs = ex / jnp.sum(ex, axis=1, keepdims=True)
    probs = jnp.where(lane < n_experts, probs, -1.0)
    lane_f = lane.astype(F32)
    p0 = jnp.max(probs, axis=1, keepdims=True)
    e0 = jnp.min(jnp.where(probs == p0, lane_f, float(LANES)), axis=1, keepdims=True)
    rest = jnp.where(lane_f == e0, -1.0, probs)
    p1 = jnp.max(rest, axis=1, keepdims=True)
    e1 = jnp.min(jnp.where(rest == p1, lane_f, float(LANES)), axis=1, keepdims=True)
    tot = p0 + p1
    sel0 = lane_f == e0
    sel1 = lane_f == e1
    sel = jnp.where(sel0 | sel1, 1.0, 0.0)
    r_idx = lax.broadcasted_iota(jnp.int32, (tm, tm), 0)
    c_idx = lax.broadcasted_iota(jnp.int32, (tm, tm), 1)
    before = (c_idx < r_idx).astype(BF16)
    cum = jnp.dot(before, sel.astype(BF16), preferred_element_type=F32)
    run8 = jnp.floor((jnp.sum(sel, axis=0, keepdims=True) + 7.0) * 0.125)
    e_r = lax.broadcasted_iota(jnp.int32, (LANES, LANES), 0)
    e_c = lax.broadcasted_iota(jnp.int32, (LANES, LANES), 1)
    seg8 = jnp.dot(jnp.broadcast_to(run8, (SUBLANES, LANES)).astype(BF16), (e_r < e_c).astype(BF16),
                   preferred_element_type=F32)[0:1, :]
    pos = cum + 8.0 * seg8
    pos0 = jnp.sum(jnp.where(sel0, pos, 0.0), axis=1, keepdims=True)
    pos1 = jnp.sum(jnp.where(sel1, pos, 0.0), axis=1, keepdims=True)
    vals = (p0 / tot, p1 / tot, e0, e1, pos0, pos1)
    route = jnp.zeros((tm, LANES), F32)
    for idx, val in enumerate(vals):
        route = jnp.where(lane == idx, val, route)
    route_ref[...] = route
    meta = jnp.where(lane[0:1, :] < n_experts, run8, 0.0)
    meta = jnp.where((lane[0:1, :] >= n_experts) & (lane[0:1, :] < 2 * n_experts),
                     pltpu.roll(8.0 * seg8, n_experts, 1), meta)
    meta_ref[0] = jnp.broadcast_to(meta, meta_ref.shape[1:]).astype(jnp.int32)


def _mlstm_out(hn, xc, z, skip, w_bf16, x2d, mod, g, w_router, b_router, s):
    t, d = x2d.shape
    di = hn.shape[1]
    e = w_router.shape[1]
    tm = _token_tile(s)
    tpb = s // tm
    n_tok_tiles = t // tm
    wr = jnp.zeros((d, LANES), F32).at[:, :e].set(w_router)
    br = jnp.zeros((1, LANES), F32).at[0, :e].set(b_router)
    tok = lambda n: pl.BlockSpec((tm, n), lambda i: (i, 0))
    return pl.pallas_call(
        functools.partial(_mlstm_out_kernel, n_experts=e),
        grid=(n_tok_tiles,),
        in_specs=[tok(di), tok(di), tok(di),
                  pl.BlockSpec((1, di), lambda i: (0, 0)),
                  pl.BlockSpec(w_bf16.shape, lambda i: (0, 0)),
                  tok(d),
                  pl.BlockSpec((1, 6, d), lambda i: (i // tpb, 0, 0)),
                  pl.BlockSpec((4, d), lambda i: (0, 0)),
                  pl.BlockSpec((d, LANES), lambda i: (0, 0)),
                  pl.BlockSpec((1, LANES), lambda i: (0, 0))],
        out_specs=[tok(d), tok(d), tok(LANES), pl.BlockSpec((1, SUBLANES, LANES), lambda i: (i, 0, 0))],
        out_shape=[jax.ShapeDtypeStruct((t, d), F32), jax.ShapeDtypeStruct((t, d), BF16),
                   jax.ShapeDtypeStruct((t, LANES), F32),
                   jax.ShapeDtypeStruct((n_tok_tiles, SUBLANES, LANES), jnp.int32)],
        compiler_params=_cparams("arbitrary"),
        name="mlstm_out_router",
    )(hn, xc, z, skip.reshape(1, di), w_bf16, x2d, mod, g, wr, br)


def _dispatch_kernel(start_ref, run_ref, seg_ref, fill_ref, h_ref, route_ref, xs_ref, stage_sc, zero_sc,
                     inflight_sm, sems, *, n_experts):
    i = pl.program_id(0)
    n_steps = pl.num_programs(0)
    tm = h_ref.shape[0]
    slot = i % 2
    sizes = (DISPATCH_ROWS, SUBLANES)

    def copy(src, src_row, dst_row, rows, sem_slot):
        return pltpu.make_async_copy(src.at[pl.ds(src_row, rows)], xs_ref.at[pl.ds(dst_row, rows)],
                                     sems.at[sem_slot])

    def wait_copies(n, rows, sem_slot):
        def body(_, carry):
            copy(zero_sc, 0, 0, rows, sem_slot).wait()
            return carry
        lax.fori_loop(0, n, body, 0)

    def wait_slot(sl):
        for which, rows in enumerate(sizes):
            wait_copies(inflight_sm[sl * 2 + which], rows, sl)

    @pl.when(i == 0)
    def _():
        for idx in range(4):
            inflight_sm[idx] = 0
        zero_sc[...] = jnp.zeros(zero_sc.shape, F32)

    route_t = jnp.transpose(route_ref[...])
    r_stage = stage_sc.shape[1]
    slot_row = lax.broadcasted_iota(jnp.int32, (r_stage, tm), 0).astype(F32)
    onehot = jnp.where((route_t[4:5, :] == slot_row) | (route_t[5:6, :] == slot_row), 1.0, 0.0).astype(BF16)
    sorted_rows = jnp.dot(onehot, h_ref[...], preferred_element_type=F32)

    wait_slot(slot)
    stage_sc[slot] = sorted_rows
    n_big = 0
    n_small = 0
    for e in range(n_experts):
        rows = run_ref[i * n_experts + e] * SUBLANES
        start = start_ref[i * n_experts + e]
        seg = seg_ref[i * n_experts + e]
        full = rows // DISPATCH_ROWS
        for k in range(tm // DISPATCH_ROWS):
            @pl.when(k < full)
            def _(seg=seg, start=start, k=k):
                copy(stage_sc.at[slot], pl.multiple_of(seg + k * DISPATCH_ROWS, SUBLANES),
                     pl.multiple_of(start + k * DISPATCH_ROWS, SUBLANES), DISPATCH_ROWS, slot).start()
        rest = (rows - full * DISPATCH_ROWS) // SUBLANES

        def small(r, carry, seg=seg, start=start, full=full):
            off = full * DISPATCH_ROWS + r * SUBLANES
            copy(stage_sc.at[slot], pl.multiple_of(seg + off, SUBLANES), pl.multiple_of(start + off, SUBLANES),
                 SUBLANES, slot).start()
            return carry

        lax.fori_loop(0, rest, small, 0)
        n_big = n_big + full
        n_small = n_small + rest
    inflight_sm[slot * 2] = n_big
    inflight_sm[slot * 2 + 1] = n_small

    @pl.when(i == n_steps - 1)
    def _():
        def fill(first_row, n, rows):
            def body(j, carry):
                copy(zero_sc, 0, pl.multiple_of(first_row + j * rows, SUBLANES), rows, 2).start()
                return carry
            lax.fori_loop(0, n, body, 0)

        for e in range(n_experts):
            fill(fill_ref[e], fill_ref[n_experts + e], SUBLANES)
        fill(fill_ref[2 * n_experts], fill_ref[2 * n_experts + 1], DISPATCH_ROWS)
        wait_slot(0)
        wait_slot(1)
        for e in range(n_experts):
            wait_copies(fill_ref[n_experts + e], SUBLANES, 2)
        wait_copies(fill_ref[2 * n_experts + 1], DISPATCH_ROWS, 2)


def _dispatch(h4, route, starts, runs, segs, fill, n_rows, s, n_experts):
    t, d = h4.shape
    tm = _token_tile(s)
    r_stage = TOP_K * tm + SUBLANES * n_experts
    grid_spec = pltpu.PrefetchScalarGridSpec(
        num_scalar_prefetch=4,
        grid=(t // tm,),
        in_specs=[pl.BlockSpec((tm, d), lambda i, *_: (i, 0)),
                  pl.BlockSpec((tm, LANES), lambda i, *_: (i, 0))],
        out_specs=pl.BlockSpec(memory_space=pl.ANY),
        scratch_shapes=[pltpu.VMEM((2, r_stage, d), F32), pltpu.VMEM((DISPATCH_ROWS, d), F32),
                        pltpu.SMEM((4,), jnp.int32), pltpu.SemaphoreType.DMA((3,))],
    )
    return pl.pallas_call(
        functools.partial(_dispatch_kernel, n_experts=n_experts),
        grid_spec=grid_spec,
        out_shape=jax.ShapeDtypeStruct((n_rows, d), F32),
        compiler_params=_cparams("arbitrary"),
        name="moe_dispatch",
    )(starts, runs, segs, fill, h4, route)


def _expert_ffn_kernel(te_ref, nreal_ref, xs_ref, wg_ref, wu_ref, wd_ref, ys_ref, xb_sc, acc_sc):
    i = pl.program_id(0)
    j = pl.program_id(1)
    last = pl.num_programs(1) - 1
    real = i < nreal_ref[0]

    @pl.when(real & (j == 0))
    def _():
        xb_sc[...] = xs_ref[...].astype(BF16)
        acc_sc[...] = jnp.zeros(acc_sc.shape, F32)

    @pl.when(real)
    def _():
        acc_sc[...] += _swiglu_partial(xb_sc[...], wg_ref.shape[2], lambda c: wg_ref[0, :, c],
                                       lambda c: wu_ref[0, :, c], lambda c: wd_ref[0, c, :])

    @pl.when(real & (j == last))
    def _():
        ys_ref[...] = acc_sc[...]

    @pl.when(jnp.logical_not(real) & (j == last))
    def _():
        ys_ref[...] = jnp.zeros(ys_ref.shape, F32)


def _expert_ffn(xs, tile_expert, n_real, wg, wu, wd):
    p, d = xs.shape
    f = wg.shape[2]
    tm = EXPERT_TILE
    tf = _ffn_chunk(f)
    nj = f // tf

    def row_idx(i, j, te, nr):
        return (jnp.minimum(i, nr[0] - 1), 0)

    def col_j(i, j, nr):
        return jnp.where(i < nr[0], j, nj - 1)

    grid_spec = pltpu.PrefetchScalarGridSpec(
        num_scalar_prefetch=2,
        grid=(p // tm, nj),
        in_specs=[pl.BlockSpec((tm, d), row_idx),
                  pl.BlockSpec((1, d, tf), lambda i, j, te, nr: (te[i], 0, col_j(i, j, nr))),
                  pl.BlockSpec((1, d, tf), lambda i, j, te, nr: (te[i], 0, col_j(i, j, nr))),
                  pl.BlockSpec((1, tf, d), lambda i, j, te, nr: (te[i], col_j(i, j, nr), 0))],
        out_specs=pl.BlockSpec((tm, d), lambda i, j, te, nr: (i, 0)),
        scratch_shapes=[pltpu.VMEM((tm, d), BF16), pltpu.VMEM((tm, d), F32)],
    )
    return pl.pallas_call(
        _expert_ffn_kernel,
        grid_spec=grid_spec,
        out_shape=jax.ShapeDtypeStruct((p, d), F32),
        compiler_params=_cparams("arbitrary", "arbitrary"),
        name="moe_expert_ffn",
    )(tile_expert, n_real, xs, wg, wu, wd)


def _combine_kernel(start_ref, run_ref, seg_ref, ys_ref, route_ref, x_ref, mod_ref, g_ref, o_ref, buf_sc, y_sc,
                    col_sc, src_sm, exp_sm, off_sm, count_sm, sems, *, n_experts):
    i = pl.program_id(0)
    tm = x_ref.shape[0]
    d = x_ref.shape[1]
    max_chunks = buf_sc.shape[1]

    def chunk_copy(slot, ci):
        return pltpu.make_async_copy(
            ys_ref.at[pl.ds(pl.multiple_of(src_sm[slot * max_chunks + ci], SUBLANES), COMBINE_ROWS)],
            buf_sc.at[slot, ci], sems.at[slot, ci])

    def fetch(tile, slot):
        n_chunks = 0
        for e in range(n_experts):
            start = start_ref[tile * n_experts + e]
            rows = run_ref[tile * n_experts + e] * SUBLANES
            seg = seg_ref[tile * n_experts + e]
            for k in range(tm // COMBINE_ROWS):
                @pl.when(rows > k * COMBINE_ROWS)
                def _(start=start, seg=seg, k=k, ci=n_chunks + k):
                    src_sm[slot * max_chunks + ci] = start + k * COMBINE_ROWS
                    exp_sm[slot * max_chunks + ci] = e
                    off_sm[slot * max_chunks + ci] = seg + k * COMBINE_ROWS
                    chunk_copy(slot, ci).start()
            n_chunks = n_chunks + (rows + COMBINE_ROWS - 1) // COMBINE_ROWS
        count_sm[slot] = n_chunks

    slot = i % 2

    @pl.when(i == 0)
    def _():
        fetch(i, slot)

    @pl.when(i + 1 < pl.num_programs(0))
    def _():
        fetch(i + 1, 1 - slot)

    route = route_ref[...]
    for idx in range(col_sc.shape[0]):
        col_sc[idx] = jnp.broadcast_to(route[:, idx:idx + 1], col_sc.shape[1:])
    lane = lax.broadcasted_iota(jnp.int32, (tm, COMBINE_ROWS), 1).astype(F32)
    y_sc[...] = jnp.zeros(y_sc.shape, F32)

    def body(ci, carry):
        chunk_copy(slot, ci).wait()
        ef = exp_sm[slot * max_chunks + ci].astype(F32)
        target = lane + off_sm[slot * max_chunks + ci].astype(F32)
        wide = lambda idx: jnp.tile(col_sc[idx], (1, COMBINE_ROWS // LANES))
        first = col_sc[2] == ef
        hit = (wide(2) == ef) & (wide(4) == target) | (wide(3) == ef) & (wide(5) == target)
        spread = jnp.where(hit, 1.0, 0.0).astype(BF16)
        rows_out = jnp.dot(spread, buf_sc[slot, ci].astype(BF16), preferred_element_type=F32)
        weight = jnp.where(first, col_sc[0], col_sc[1])
        y_sc[...] += jnp.tile(weight, (1, d // LANES)) * rows_out
        return carry

    lax.fori_loop(0, count_sm[slot], body, 0)
    o_ref[...] = x_ref[...] + mod_ref[0, 5:6, :] * _rms(y_sc[...], g_ref[3:4, :])


def _combine(ys, starts, runs, segs, route, x2d, mod, g, s, n_experts):
    t, d = x2d.shape
    tm = _token_tile(s)
    tpb = s // tm
    max_chunks = TOP_K * tm // COMBINE_ROWS + n_experts
    grid_spec = pltpu.PrefetchScalarGridSpec(
        num_scalar_prefetch=3,
        grid=(t // tm,),
        in_specs=[pl.BlockSpec(memory_space=pl.ANY),
                  pl.BlockSpec((tm, LANES), lambda i, *_: (i, 0)),
                  pl.BlockSpec((tm, d), lambda i, *_: (i, 0)),
                  pl.BlockSpec((1, 6, d), lambda i, *_: (i // tpb, 0, 0)),
                  pl.BlockSpec((4, d), lambda i, *_: (0, 0))],
        out_specs=pl.BlockSpec((tm, d), lambda i, *_: (i, 0)),
        scratch_shapes=[pltpu.VMEM((2, max_chunks, COMBINE_ROWS, d), F32), pltpu.VMEM((tm, d), F32),
                        pltpu.VMEM((6, tm, LANES), F32),
                        pltpu.SMEM((2 * max_chunks,), jnp.int32), pltpu.SMEM((2 * max_chunks,), jnp.int32),
                        pltpu.SMEM((2 * max_chunks,), jnp.int32), pltpu.SMEM((2,), jnp.int32),
                        pltpu.SemaphoreType.DMA((2, max_chunks))],
    )
    return pl.pallas_call(
        functools.partial(_combine_kernel, n_experts=n_experts),
        grid_spec=grid_spec,
        out_shape=jax.ShapeDtypeStruct((t, d), F32),
        compiler_params=_cparams("arbitrary"),
        name="moe_combine",
    )(starts, runs, segs, ys, route, x2d, mod, g)


def _moe(h4, route, meta, x2d, mod, g, wg, wu, wd, s):
    t, d = x2d.shape
    e = wg.shape[0]
    tm = EXPERT_TILE
    n_tok_tiles = t // _token_tile(s)
    n_tiles = (TOP_K * t + SUBLANES * n_tok_tiles * e) // tm + e + 1
    runs = meta[:, 0, :e]
    segs = meta[:, 0, e:2 * e]
    used = SUBLANES * jnp.sum(runs, axis=0)
    tiles_e = (used + tm - 1) // tm
    ends = jnp.cumsum(tiles_e)
    off = (ends - tiles_e) * tm
    starts = off[None, :] + SUBLANES * (jnp.cumsum(runs, axis=0) - runs)
    n_real = ends[e - 1:e].astype(jnp.int32)
    fill = jnp.concatenate([off + used, (tiles_e * tm - used) // SUBLANES,
                            n_real * tm, (n_tiles - n_real) * (tm // DISPATCH_ROWS)]).astype(jnp.int32)
    tile_ids = jnp.arange(n_tiles, dtype=jnp.int32)
    tile_expert = jnp.minimum(jnp.sum(tile_ids[:, None] >= ends[None, :], axis=1), e - 1)
    last_expert = jnp.minimum(jnp.sum(n_real[0] - 1 >= ends), e - 1)
    tile_expert = jnp.where(tile_ids < n_real[0], tile_expert, last_expert).astype(jnp.int32)
    flat = lambda a: a.reshape(-1).astype(jnp.int32)
    xs = _dispatch(h4, route, flat(starts), flat(runs), flat(segs), fill, n_tiles * tm, s, e)
    ys = _expert_ffn(xs, tile_expert, n_real, wg, wu, wd)
    return _combine(ys, flat(starts), flat(runs), flat(segs), route, x2d, mod, g, s, e)


def kernel(x, c, mod_w, mod_b, norm_g, attn_w_in, attn_w_out, attn_lambda, attn_subln, ffn_w_gate, ffn_w_up, ffn_w_down, mlstm_w_in, mlstm_conv_w, mlstm_conv_b, mlstm_w_q, mlstm_w_k, mlstm_w_v, mlstm_w_gate, mlstm_b_gate, mlstm_skip, mlstm_norm, mlstm_w_out, moe_w_router, moe_b_router, moe_w_gate, moe_w_up, moe_w_down):
    b, s, d = x.shape
    depth = mod_w.shape[0]
    mod_all = _modulation(c, mod_w, mod_b)
    x2d = x.reshape(b * s, d)
    da = d // (2 * ATTN_HEADS)
    for i in range(depth):
        mod = mod_all[i]
        g = norm_g[i]
        j = i // N_MIXERS
        if i % N_MIXERS == 0:
            lambda_init = 0.8 - 0.6 * math.exp(-0.3 * i)
            q, k, v = _in_proj(x2d, mod, g, attn_w_in[j].astype(BF16), 3, (LOG2E * da ** -0.5, 1.0, 1.0), s,
                               "attn_in_proj")
            o = _diff_attention(q, k, v, attn_lambda[j], attn_subln[j], lambda_init, b, s)
            x2d = _attn_out_ffn(o, attn_w_out[j].astype(BF16), ffn_w_gate[j].astype(BF16),
                                ffn_w_up[j].astype(BF16), ffn_w_down[j].astype(BF16), x2d, mod, g, s)
        else:
            xm, z = _in_proj(x2d, mod, g, mlstm_w_in[j].astype(BF16), 2, (1.0, 1.0), s, "mlstm_in_proj")
            xc, q, kt, v, gates = _mlstm_qkv(xm, mlstm_conv_w[j], mlstm_conv_b[j], mlstm_w_q[j], mlstm_w_k[j],
                                             mlstm_w_v[j], mlstm_w_gate[j], mlstm_b_gate[j], b, s)
            hn = _mlstm_cell(q, kt, v, gates, mlstm_norm[j], b, s)
            di = hn.shape[2]
            x2d, h4, route, meta = _mlstm_out(hn.reshape(b * s, di), xc.reshape(b * s, di), z, mlstm_skip[j],
                                              mlstm_w_out[j].astype(BF16), x2d, mod, g,
                                              moe_w_router[j], moe_b_router[j], s)
            x2d = _moe(h4, route, meta, x2d, mod, g, moe_w_gate[j].astype(BF16), moe_w_up[j].astype(BF16),
                       moe_w_down[j].astype(BF16), s)
    return x2d.reshape(b, s, d)
```

```python
import functools
import math

import jax
import jax.numpy as jnp
from jax import lax
from jax.experimental import pallas as pl
from jax.experimental.pallas import tpu as pltpu

F32 = jnp.float32
BF16 = jnp.bfloat16
HIGHEST = lax.Precision.HIGHEST

CHUNK = 64
ATTN_HEADS = 8
ALIBI_MAX_BIAS = 8.0
MLSTM_HEADS = 4
MLSTM_CONV = 4
QKV_BLOCK = 4
TOP_K = 2
NORM_EPS = 1e-6
N_MIXERS = 2

LANES = 128
SUBLANES = 8
VMEM_LIMIT_BYTES = 56 * 1024 * 1024
FFN_SUBCHUNK = 256
EXPERT_TILE = 512
DISPATCH_ROWS = 128
COMBINE_ROWS = 256
NEG_BIG = -1e30
LOG2E = math.log2(math.e)


def _cparams(*sem):
    return pltpu.CompilerParams(dimension_semantics=sem, vmem_limit_bytes=VMEM_LIMIT_BYTES)


def _rms(x, g):
    return x * lax.rsqrt(jnp.mean(x * x, axis=-1, keepdims=True) + NORM_EPS) * g


def _silu(x):
    return x * jax.nn.sigmoid(x)


def _token_tile(s):
    return min(512, s)


def _mod_kernel(c_ref, w_ref, b_ref, o_ref):
    cond = _silu(c_ref[...])
    o_ref[0] = jnp.dot(cond, w_ref[0], preferred_element_type=F32, precision=HIGHEST) + b_ref[0]


def _modulation(c, mod_w, mod_b):
    depth, d, n = mod_w.shape
    b = c.shape[0]
    tn = 1536 if n % 1536 == 0 else n
    out = pl.pallas_call(
        _mod_kernel,
        grid=(depth, n // tn),
        in_specs=[pl.BlockSpec((b, d), lambda i, j: (0, 0)),
                  pl.BlockSpec((1, d, tn), lambda i, j: (i, 0, j)),
                  pl.BlockSpec((1, 1, tn), lambda i, j: (i, 0, j))],
        out_specs=pl.BlockSpec((1, b, tn), lambda i, j: (i, 0, j)),
        out_shape=jax.ShapeDtypeStruct((depth, b, n), F32),
        compiler_params=_cparams("arbitrary", "arbitrary"),
        name="modulation",
    )(c, mod_w, mod_b.reshape(depth, 1, n))
    return out.reshape(depth, b, 6, d)


def _in_proj_kernel(x_ref, mod_ref, g_ref, w_ref, *o_refs, scales):
    x = x_ref[...]
    h = _rms(x, g_ref[0:1, :]) * (1.0 + mod_ref[0, 1:2, :]) + mod_ref[0, 0:1, :]
    hb = h.astype(BF16)
    n = o_refs[0].shape[1]
    for idx, o_ref in enumerate(o_refs):
        r = jnp.dot(hb, w_ref[:, idx * n:(idx + 1) * n], preferred_element_type=F32)
        if scales[idx] != 1.0:
            r = r * scales[idx]
        o_ref[...] = r.astype(BF16)


def _in_proj(x2d, mod, g, w_bf16, n_out, scales, s, name):
    t, d = x2d.shape
    tm = _token_tile(s)
    tpb = s // tm
    n = w_bf16.shape[1] // n_out
    return pl.pallas_call(
        functools.partial(_in_proj_kernel, scales=scales),
        grid=(t // tm,),
        in_specs=[pl.BlockSpec((tm, d), lambda i: (i, 0)),
                  pl.BlockSpec((1, 6, d), lambda i: (i // tpb, 0, 0)),
                  pl.BlockSpec((4, d), lambda i: (0, 0)),
                  pl.BlockSpec(w_bf16.shape, lambda i: (0, 0))],
        out_specs=[pl.BlockSpec((tm, n), lambda i: (i, 0))] * n_out,
        out_shape=[jax.ShapeDtypeStruct((t, n), BF16)] * n_out,
        compiler_params=_cparams("arbitrary"),
        name=name,
    )(x2d, mod, g, w_bf16)


def _attn_kernel(slopes_ref, lam_ref, subln_ref, q_ref, k_ref, v_ref, o_ref, m_sc, acc_sc, bias_sc, vext_sc,
                 s_sc, smax_sc, *, tq, tk, lambda_init):
    slope = slopes_ref[pl.program_id(0)]
    hd = q_ref.shape[2]
    half = hd // 2
    n_q = q_ref.shape[1] // tq

    vext_sc[:, :hd] = v_ref[0]

    @pl.when(pl.program_id(1) == 0)
    def _():
        vext_sc[:, hd:] = jnp.ones((vext_sc.shape[0], hd), BF16)
        row = lax.broadcasted_iota(jnp.int32, (tq, tk), 0)
        col = lax.broadcasted_iota(jnp.int32, (tq, tk), 1)
        dist = (row - col).astype(F32)
        bias_sc[0] = -slope * dist
        bias_sc[1] = jnp.where(col // CHUNK <= row // CHUNK, -slope * jnp.abs(dist), NEG_BIG)

    lam = lam_ref[...]
    lam_full = (jnp.exp(jnp.sum(lam[0:1, :] * lam[1:2, :], axis=1, keepdims=True))
                - jnp.exp(jnp.sum(lam[2:3, :] * lam[3:4, :], axis=1, keepdims=True)) + lambda_init)

    def scores(qi, j, slot):
        q = q_ref[0, qi * tq:(qi + 1) * tq, :]
        lane = lax.broadcasted_iota(jnp.int32, q.shape, 1)
        zero = jnp.zeros_like(q)
        q2 = jnp.concatenate([jnp.where(lane < half, q, zero), jnp.where(lane >= half, q, zero)], axis=0)
        kj = k_ref[0, j * tk:(j + 1) * tk, :]
        s = lax.dot_general(q2, kj, (((1,), (1,)), ((), ())), preferred_element_type=F32)
        bias = bias_sc[1 if j == qi else 0]
        s = s + jnp.concatenate([bias, bias], axis=0)
        s_sc[slot] = s
        smax_sc[slot] = jnp.broadcast_to(jnp.max(s, axis=1, keepdims=True), smax_sc.shape[1:])

    def accumulate(qi, j, slot):
        const = 0.0 if j == qi else -slope * float((qi - j) * tq)
        m_prev = m_sc[...]
        m_next = jnp.maximum(m_prev, smax_sc[slot] + const)
        alpha = jnp.exp2(m_prev - m_next)
        p = jnp.exp2(s_sc[slot] - jnp.tile(m_next - const, (1, tk // LANES)))
        pv = jnp.dot(p.astype(BF16), vext_sc[j * tk:(j + 1) * tk, :], preferred_element_type=F32)
        acc_sc[...] = jnp.tile(alpha, (1, 2)) * acc_sc[...] + pv
        m_sc[...] = m_next

    def finalize(qi):
        acc = acc_sc[...]
        o_all = acc[:, :hd] / acc[:, hd:]
        o = o_all[:tq] - lam_full * o_all[tq:]
        o = _rms(o, subln_ref[...]) * (1.0 - lambda_init)
        o_ref[0, qi * tq:(qi + 1) * tq, :] = o.astype(BF16)

    blocks = [(qi, j) for qi in range(n_q) for j in range(qi + 1)]
    scores(*blocks[0], 0)
    for n, (qi, j) in enumerate(blocks):
        if n + 1 < len(blocks):
            scores(*blocks[n + 1], (n + 1) % 2)
        if j == 0:
            m_sc[...] = jnp.full(m_sc.shape, NEG_BIG, F32)
            acc_sc[...] = jnp.zeros(acc_sc.shape, F32)
        accumulate(qi, j, n % 2)
        if j == qi:
            finalize(qi)


def _diff_attention(q, k, v, lam, subln, lambda_init, b, s):
    d = q.shape[1]
    hd = d // ATTN_HEADS
    tq = tk = min(512, s)
    slopes = LOG2E * jnp.exp2(-ALIBI_MAX_BIAS * jnp.arange(1, ATTN_HEADS + 1, dtype=F32) / ATTN_HEADS)
    q3, k3, v3 = (a.reshape(b, s, d) for a in (q, k, v))
    grid_spec = pltpu.PrefetchScalarGridSpec(
        num_scalar_prefetch=1,
        grid=(ATTN_HEADS, b),
        in_specs=[pl.BlockSpec(lam.shape, lambda h, bi, sl: (0, 0)),
                  pl.BlockSpec((1, hd), lambda h, bi, sl: (0, 0)),
                  pl.BlockSpec((1, s, hd), lambda h, bi, sl: (bi, 0, h)),
                  pl.BlockSpec((1, s, hd), lambda h, bi, sl: (bi, 0, h)),
                  pl.BlockSpec((1, s, hd), lambda h, bi, sl: (bi, 0, h))],
        out_specs=pl.BlockSpec((1, s, hd), lambda h, bi, sl: (bi, 0, h)),
        scratch_shapes=[pltpu.VMEM((2 * tq, LANES), F32), pltpu.VMEM((2 * tq, 2 * hd), F32),
                        pltpu.VMEM((2, tq, tk), F32), pltpu.VMEM((s, 2 * hd), BF16),
                        pltpu.VMEM((2, 2 * tq, tk), F32), pltpu.VMEM((2, 2 * tq, LANES), F32)],
    )
    o = pl.pallas_call(
        functools.partial(_attn_kernel, tq=tq, tk=tk, lambda_init=lambda_init),
        grid_spec=grid_spec,
        out_shape=jax.ShapeDtypeStruct((b, s, d), BF16),
        compiler_params=_cparams("arbitrary", "arbitrary"),
        name="diff_attention",
    )(slopes, lam, subln.reshape(1, hd), q3, k3, v3)
    return o.reshape(b * s, d)


def _mixer_out_core(inp_bf16, w_ref, x_ref, mod_ref, g_ref, x_out_ref):
    y = jnp.dot(inp_bf16, w_ref[...], preferred_element_type=F32)
    x1 = x_ref[...] + mod_ref[0, 2:3, :] * _rms(y, g_ref[1:2, :])
    x_out_ref[...] = x1
    return _rms(x1, g_ref[2:3, :]) * (1.0 + mod_ref[0, 4:5, :]) + mod_ref[0, 3:4, :]


def _swiglu_partial(h, width, wg, wu, wd):
    out = None
    for c0 in range(0, width, FFN_SUBCHUNK):
        c = slice(c0, min(c0 + FFN_SUBCHUNK, width))
        gate = jnp.dot(h, wg(c), preferred_element_type=F32)
        up = jnp.dot(h, wu(c), preferred_element_type=F32)
        part = jnp.dot((_silu(gate) * up).astype(BF16), wd(c), preferred_element_type=F32)
        out = part if out is None else out + part
    return out


def _attn_out_ffn_kernel(o_ref, wo_ref, x_ref, mod_ref, g_ref, wg_ref, wu_ref, wd_ref, out_ref, x1_sc):
    h2 = _mixer_out_core(o_ref[...], wo_ref, x_ref, mod_ref, g_ref, x1_sc).astype(BF16)
    y = _swiglu_partial(h2, wg_ref.shape[1], lambda c: wg_ref[:, c], lambda c: wu_ref[:, c],
                        lambda c: wd_ref[c, :])
    out_ref[...] = x1_sc[...] + mod_ref[0, 5:6, :] * _rms(y, g_ref[3:4, :])


def _ffn_chunk(f):
    for tf in (1792, 1408, 1024, 896, 512, 256, 128):
        if f % tf == 0:
            return tf
    return f


def _attn_out_ffn(o, wo, wg, wu, wd, x2d, mod, g, s):
    t, d = x2d.shape
    f = wg.shape[1]
    tm = _token_tile(s)
    tpb = s // tm
    tok = pl.BlockSpec((tm, d), lambda i: (i, 0))
    resident = lambda w: pl.BlockSpec(w.shape, lambda i: (0, 0), pipeline_mode=pl.Buffered(1))
    return pl.pallas_call(
        _attn_out_ffn_kernel,
        grid=(t // tm,),
        in_specs=[tok,
                  resident(wo),
                  tok,
                  pl.BlockSpec((1, 6, d), lambda i: (i // tpb, 0, 0)),
                  pl.BlockSpec((4, d), lambda i: (0, 0)),
                  resident(wg), resident(wu), resident(wd)],
        out_specs=tok,
        out_shape=jax.ShapeDtypeStruct((t, d), F32),
        scratch_shapes=[pltpu.VMEM((tm, d), F32)],
        compiler_params=_cparams("arbitrary"),
        name="attn_out_dense_ffn",
    )(o, wo, x2d, mod, g, wg, wu, wd)


def _mlstm_qkv_kernel(xm_ref, cw_ref, cb_ref, wqk_ref, wv_ref, wgq_ref, wgk_ref, wgv_ref, bg_ref,
                      xc_ref, q_ref, kt_ref, v_ref, gates_ref, pad_sc, *, ts, k_scale):
    c = pl.program_id(1)
    s, cw = xm_ref.shape[1], xm_ref.shape[2]
    front = SUBLANES
    pad_sc[0:front, :] = jnp.zeros((front, cw), F32)
    pad_sc[front:front + s, :] = xm_ref[0].astype(F32)

    @pl.when(c == 0)
    def _():
        gates_ref[0] = jnp.broadcast_to(bg_ref[...], gates_ref.shape[1:])

    for r in range(s // ts):
        r0 = r * ts
        conv = cb_ref[0]
        for j in reversed(range(MLSTM_CONV)):
            start = r0 + front - (MLSTM_CONV - 1) + j
            conv = conv + pad_sc[start:start + ts, :] * cw_ref[0, j:j + 1, :]
        xc = _silu(conv)
        xcb = xc.astype(BF16)
        qk = jnp.dot(xcb, wqk_ref[0], preferred_element_type=F32)
        qb = qk[:, :cw].astype(BF16)
        kb = qk[:, cw:].astype(BF16)
        vb = jnp.dot(xm_ref[0, r0:r0 + ts, :], wv_ref[0], preferred_element_type=F32).astype(BF16)
        gates_ref[0, r0:r0 + ts, :] += (jnp.dot(qb, wgq_ref[...], preferred_element_type=F32)
                                        + jnp.dot(kb, wgk_ref[...], preferred_element_type=F32)
                                        + jnp.dot(vb, wgv_ref[...], preferred_element_type=F32))
        xc_ref[0, r0:r0 + ts, :] = xcb
        q_ref[0, r0:r0 + ts, :] = qb
        kt_ref[0, :, r0:r0 + ts] = jnp.transpose(qk[:, cw:] * k_scale).astype(BF16)
        v_ref[0, r0:r0 + ts, :] = vb


def _block_diag(w, cw):
    g, qb, _ = w.shape
    per = cw // qb
    wr = w.reshape(g // per, per, qb, qb)
    eye = jnp.eye(per, dtype=w.dtype)
    return jnp.einsum("cgio,gh->cgiho", wr, eye).reshape(g // per, cw, cw)


def _mlstm_qkv(xm, conv_w, conv_b, w_q, w_k, w_v, w_gate, b_gate, b, s):
    di = xm.shape[1]
    cw = 256
    nchunk = di // cw
    nh = MLSTM_HEADS
    dh = di // nh
    ts = min(256, s)
    wqk = jnp.concatenate([_block_diag(w_q, cw), _block_diag(w_k, cw)], axis=2).astype(BF16)
    wv = _block_diag(w_v, cw).astype(BF16)
    wg = jnp.zeros((3 * di, LANES), F32).at[:, :2 * nh].set(w_gate).astype(BF16)
    bg = jnp.zeros((1, LANES), F32).at[0, :2 * nh].set(b_gate)
    xm3 = xm.reshape(b, s, di)
    blk = pl.BlockSpec((1, s, cw), lambda bi, c: (bi, 0, c))
    outs = pl.pallas_call(
        functools.partial(_mlstm_qkv_kernel, ts=ts, k_scale=dh ** -0.5),
        grid=(b, nchunk),
        in_specs=[blk,
                  pl.BlockSpec((1, MLSTM_CONV, cw), lambda bi, c: (c, 0, 0)),
                  pl.BlockSpec((1, 1, cw), lambda bi, c: (c, 0, 0)),
                  pl.BlockSpec((1, cw, 2 * cw), lambda bi, c: (c, 0, 0)),
                  pl.BlockSpec((1, cw, cw), lambda bi, c: (c, 0, 0)),
                  pl.BlockSpec((cw, LANES), lambda bi, c: (c, 0)),
                  pl.BlockSpec((cw, LANES), lambda bi, c: (nchunk + c, 0)),
                  pl.BlockSpec((cw, LANES), lambda bi, c: (2 * nchunk + c, 0)),
                  pl.BlockSpec((1, LANES), lambda bi, c: (0, 0))],
        out_specs=[blk, blk, pl.BlockSpec((1, cw, s), lambda bi, c: (bi, c, 0)), blk,
                   pl.BlockSpec((1, s, LANES), lambda bi, c: (bi, 0, 0))],
        out_shape=[jax.ShapeDtypeStruct((b, s, di), BF16)] * 2 + [jax.ShapeDtypeStruct((b, di, s), BF16)]
        + [jax.ShapeDtypeStruct((b, s, di), BF16), jax.ShapeDtypeStruct((b, s, LANES), F32)],
        scratch_shapes=[pltpu.VMEM((s + SUBLANES, cw), F32)],
        compiler_params=_cparams("arbitrary", "arbitrary"),
        name="mlstm_qkv",
    )(xm3, conv_w.reshape(MLSTM_CONV, nchunk, cw).transpose(1, 0, 2), conv_b.reshape(nchunk, 1, cw),
      wqk, wv, wg, wg, wg, bg)
    return outs


def _log_sigmoid(x):
    return jnp.minimum(x, 0.0) - jnp.log1p(jnp.exp(-jnp.abs(x)))


def _split3(x):
    hi = x.astype(BF16)
    r = x - hi.astype(F32)
    mid = r.astype(BF16)
    lo = (r - mid.astype(F32)).astype(BF16)
    return hi, mid, lo


def _mlstm_cell_kernel(q_ref, kt_ref, v_ref, gc_ref, gr_ref, ng_ref, o_ref, ct_sc, n_sc, m_sc, *, nh):
    c = pl.program_id(1)
    L = q_ref.shape[1]
    dh = q_ref.shape[2] // nh

    @pl.when(c == 0)
    def _():
        ct_sc[...] = jnp.zeros(ct_sc.shape, F32)
        n_sc[...] = jnp.zeros(n_sc.shape, F32)
        m_sc[...] = jnp.zeros(m_sc.shape, F32)

    t_idx = lax.broadcasted_iota(jnp.int32, (L, L), 0)
    s_idx = lax.broadcasted_iota(jnp.int32, (L, L), 1)
    causal = s_idx <= t_idx
    tri = causal.astype(BF16)
    tri_t = (t_idx <= s_idx).astype(F32)
    gc = gc_ref[0]
    gr = gr_ref[0]
    b_cols = sum(jnp.dot(tri, part, preferred_element_type=F32) for part in _split3(_log_sigmoid(gc)))
    b_rows = jnp.dot(_log_sigmoid(gr), tri_t, preferred_element_type=F32, precision=HIGHEST)
    lane = lax.broadcasted_iota(jnp.int32, gc.shape, 1)

    for hl in range(nh):
        h = hl
        cols = slice(hl * dh, (hl + 1) * dh)
        i_col = jnp.sum(jnp.where(lane == h, gc, 0.0), axis=1, keepdims=True)
        b_col = jnp.sum(jnp.where(lane == h + nh, b_cols, 0.0), axis=1, keepdims=True)
        i_row = gr[h:h + 1, :]
        b_row = b_rows[nh + h:nh + h + 1, :]
        u_row = i_row - b_row
        u_col = i_col - b_col

        m_prev = m_sc[hl, 0:1, 0:1]
        dlog = jnp.where(causal, b_col + u_row, NEG_BIG)
        g = b_col + m_prev
        m_t = jnp.maximum(g, jnp.max(dlog, axis=1, keepdims=True))
        qb = q_ref[0, :, cols]
        ktb = kt_ref[0, cols, :]
        vb = v_ref[0, :, cols]
        w = jnp.exp(dlog - m_t) * jnp.dot(qb, ktb, preferred_element_type=F32)
        inter = jnp.exp(g - m_t)
        num = (jnp.dot(w.astype(BF16), vb, preferred_element_type=F32)
               + inter * jnp.dot(qb, ct_sc[hl].astype(BF16), preferred_element_type=F32))
        den = (jnp.sum(w, axis=1, keepdims=True)
               + inter * jnp.dot(qb, n_sc[hl].astype(BF16), preferred_element_type=F32))
        scale = 1.0 / jnp.maximum(jnp.abs(den), jnp.exp(-m_t))
        hh = num * jnp.tile(scale, (1, dh // LANES))

        mu = jnp.mean(hh, axis=1, keepdims=True)
        cen = hh - mu
        var = jnp.mean(cen * cen, axis=1, keepdims=True)
        o_ref[0, :, cols] = (cen * lax.rsqrt(var + NORM_EPS) * ng_ref[h]).astype(BF16)

        b_last = b_row[:, L - 1:L]
        m_new = jnp.maximum(b_last + m_prev, jnp.max(b_last + u_row, axis=1, keepdims=True))
        decay = jnp.exp(b_last + m_prev - m_new)
        ws_col = jnp.exp(b_last - m_new + u_col).astype(BF16)
        ct_sc[hl] = decay * ct_sc[hl] + jnp.dot(ktb, vb * ws_col, preferred_element_type=F32)
        n_sc[hl] = decay * n_sc[hl] + jnp.dot(ktb, jnp.broadcast_to(ws_col, (L, LANES)),
                                              preferred_element_type=F32)
        m_sc[hl] = jnp.broadcast_to(m_new, m_sc.shape[1:])


def _mlstm_cell(q, kt, v, gates, norm_g, b, s):
    di = q.shape[2]
    nh = MLSTM_HEADS
    dh = di // nh
    L = min(256, s)
    gates_t = jnp.transpose(gates[:, :, :SUBLANES], (0, 2, 1))
    blk = pl.BlockSpec((1, L, di), lambda bi, c: (bi, c, 0))
    return pl.pallas_call(
        functools.partial(_mlstm_cell_kernel, nh=nh),
        grid=(b, s // L),
        in_specs=[blk,
                  pl.BlockSpec((1, di, L), lambda bi, c: (bi, 0, c)),
                  blk,
                  pl.BlockSpec((1, L, LANES), lambda bi, c: (bi, c, 0)),
                  pl.BlockSpec((1, SUBLANES, L), lambda bi, c: (bi, 0, c)),
                  pl.BlockSpec((nh, 1, dh), lambda bi, c: (0, 0, 0))],
        out_specs=blk,
        out_shape=jax.ShapeDtypeStruct((b, s, di), BF16),
        scratch_shapes=[pltpu.VMEM((nh, dh, dh), F32), pltpu.VMEM((nh, dh, LANES), F32),
                        pltpu.VMEM((nh, SUBLANES, LANES), F32)],
        compiler_params=_cparams("arbitrary", "arbitrary"),
        name="mlstm_cell",
    )(q, kt, v, gates, gates_t, norm_g.reshape(nh, 1, dh))


def _mlstm_out_kernel(hn_ref, xc_ref, z_ref, skip_ref, w_ref, x_ref, mod_ref, g_ref, wr_ref, br_ref,
                      x_out_ref, h_out_ref, route_ref, meta_ref, *, n_experts):
    inner = ((hn_ref[...].astype(F32) + skip_ref[...] * xc_ref[...].astype(F32))
             * _silu(z_ref[...].astype(F32)))
    h4 = _mixer_out_core(inner.astype(BF16), w_ref, x_ref, mod_ref, g_ref, x_out_ref)
    h_hi = h4.astype(BF16)
    h_out_ref[...] = h_hi

    tm = h4.shape[0]
    lane = lax.broadcasted_iota(jnp.int32, (tm, LANES), 1)
    h_lo = (h4 - h_hi.astype(F32)).astype(BF16)
    w_hi = wr_ref[...].astype(BF16)
    w_lo = (wr_ref[...] - w_hi.astype(F32)).astype(BF16)
    hi_terms = jnp.dot(h_hi, jnp.concatenate([w_hi, w_lo], axis=1), preferred_element_type=F32)
    logits = (hi_terms[:, :LANES] + hi_terms[:, LANES:] + jnp.dot(h_lo, w_hi, preferred_element_type=F32)
              + br_ref[...])
    logits = jnp.where(lane < n_experts, logits, NEG_BIG)
    ex = jnp.exp(logits - jnp.max(logits, axis=1, keepdims=True))
    probs = ex / jnp.sum(ex, axis=1, keepdims=True)
    probs = jnp.where(lane < n_experts, probs, -1.0)
    lane_f = lane.astype(F32)
    p0 = jnp.max(probs, axis=1, keepdims=True)
    e0 = jnp.min(jnp.where(probs == p0, lane_f, float(LANES)), axis=1, keepdims=True)
    rest = jnp.where(lane_f == e0, -1.0, probs)
    p1 = jnp.max(rest, axis=1, keepdims=True)
    e1 = jnp.min(jnp.where(rest == p1, lane_f, float(LANES)), axis=1, keepdims=True)
    tot = p0 + p1
    sel0 = lane_f == e0
    sel1 = lane_f == e1
    sel = jnp.where(sel0 | sel1, 1.0, 0.0)
    r_idx = lax.broadcasted_iota(jnp.int32, (tm, tm), 0)
    c_idx = lax.broadcasted_iota(jnp.int32, (tm, tm), 1)
    before = (c_idx < r_idx).astype(BF16)
    cum = jnp.dot(before, sel.astype(BF16), preferred_element_type=F32)
    run8 = jnp.floor((jnp.sum(sel, axis=0, keepdims=True) + 7.0) * 0.125)
    e_r = lax.broadcasted_iota(jnp.int32, (LANES, LANES), 0)
    e_c = lax.broadcasted_iota(jnp.int32, (LANES, LANES), 1)
    seg8 = jnp.dot(jnp.broadcast_to(run8, (SUBLANES, LANES)).astype(BF16), (e_r < e_c).astype(BF16),
                   preferred_element_type=F32)[0:1, :]
    pos = cum + 8.0 * seg8
    pos0 = jnp.sum(jnp.where(sel0, pos, 0.0), axis=1, keepdims=True)
    pos1 = jnp.sum(jnp.where(sel1, pos, 0.0), axis=1, keepdims=True)
    vals = (p0 / tot, p1 / tot, e0, e1, pos0, pos1)
    route = jnp.zeros((tm, LANES), F32)
    for idx, val in enumerate(vals):
        route = jnp.where(lane == idx, val, route)
    route_ref[...] = route
    meta = jnp.where(lane[0:1, :] < n_experts, run8, 0.0)
    meta = jnp.where((lane[0:1, :] >= n_experts) & (lane[0:1, :] < 2 * n_experts),
                     pltpu.roll(8.0 * seg8, n_experts, 1), meta)
    meta_ref[0] = jnp.broadcast_to(meta, meta_ref.shape[1:]).astype(jnp.int32)


def _mlstm_out(hn, xc, z, skip, w_bf16, x2d, mod, g, w_router, b_router, s):
    t, d = x2d.shape
    di = hn.shape[1]
    e = w_router.shape[1]
    tm = _token_tile(s)
    tpb = s // tm
    n_tok_tiles = t // tm
    wr = jnp.zeros((d, LANES), F32).at[:, :e].set(w_router)
    br = jnp.zeros((1, LANES), F32).at[0, :e].set(b_router)
    tok = lambda n: pl.BlockSpec((tm, n), lambda i: (i, 0))
    return pl.pallas_call(
        functools.partial(_mlstm_out_kernel, n_experts=e),
        grid=(n_tok_tiles,),
        in_specs=[tok(di), tok(di), tok(di),
                  pl.BlockSpec((1, di), lambda i: (0, 0)),
                  pl.BlockSpec(w_bf16.shape, lambda i: (0, 0)),
                  tok(d),
                  pl.BlockSpec((1, 6, d), lambda i: (i // tpb, 0, 0)),
                  pl.BlockSpec((4, d), lambda i: (0, 0)),
                  pl.BlockSpec((d, LANES), lambda i: (0, 0)),
                  pl.BlockSpec((1, LANES), lambda i: (0, 0))],
        out_specs=[tok(d), tok(d), tok(LANES), pl.BlockSpec((1, SUBLANES, LANES), lambda i: (i, 0, 0))],
        out_shape=[jax.ShapeDtypeStruct((t, d), F32), jax.ShapeDtypeStruct((t, d), BF16),
                   jax.ShapeDtypeStruct((t, LANES), F32),
                   jax.ShapeDtypeStruct((n_tok_tiles, SUBLANES, LANES), jnp.int32)],
        compiler_params=_cparams("arbitrary"),
        name="mlstm_out_router",
    )(hn, xc, z, skip.reshape(1, di), w_bf16, x2d, mod, g, wr, br)


def _dispatch_kernel(start_ref, run_ref, seg_ref, fill_ref, h_ref, route_ref, xs_ref, stage_sc, zero_sc,
                     inflight_sm, sems, *, n_experts):
    i = pl.program_id(0)
    n_steps = pl.num_programs(0)
    tm = h_ref.shape[0]
    slot = i % 2
    sizes = (DISPATCH_ROWS, SUBLANES)

    def copy(src, src_row, dst_row, rows, sem_slot):
        return pltpu.make_async_copy(src.at[pl.ds(src_row, rows)], xs_ref.at[pl.ds(dst_row, rows)],
                                     sems.at[sem_slot])

    def wait_copies(n, rows, sem_slot):
        def body(_, carry):
            copy(zero_sc, 0, 0, rows, sem_slot).wait()
            return carry
        lax.fori_loop(0, n, body, 0)

    def wait_slot(sl):
        for which, rows in enumerate(sizes):
            wait_copies(inflight_sm[sl * 2 + which], rows, sl)

    @pl.when(i == 0)
    def _():
        for idx in range(4):
            inflight_sm[idx] = 0
        zero_sc[...] = jnp.zeros(zero_sc.shape, F32)

    route_t = jnp.transpose(route_ref[...])
    r_stage = stage_sc.shape[1]
    slot_row = lax.broadcasted_iota(jnp.int32, (r_stage, tm), 0).astype(F32)
    onehot = jnp.where((route_t[4:5, :] == slot_row) | (route_t[5:6, :] == slot_row), 1.0, 0.0).astype(BF16)
    sorted_rows = jnp.dot(onehot, h_ref[...], preferred_element_type=F32)

    wait_slot(slot)
    stage_sc[slot] = sorted_rows
    n_big = 0
    n_small = 0
    for e in range(n_experts):
        rows = run_ref[i * n_experts + e] * SUBLANES
        start = start_ref[i * n_experts + e]
        seg = seg_ref[i * n_experts + e]
        full = rows // DISPATCH_ROWS
        for k in range(tm // DISPATCH_ROWS):
            @pl.when(k < full)
            def _(seg=seg, start=start, k=k):
                copy(stage_sc.at[slot], pl.multiple_of(seg + k * DISPATCH_ROWS, SUBLANES),
                     pl.multiple_of(start + k * DISPATCH_ROWS, SUBLANES), DISPATCH_ROWS, slot).start()
        rest = (rows - full * DISPATCH_ROWS) // SUBLANES

        def small(r, carry, seg=seg, start=start, full=full):
            off = full * DISPATCH_ROWS + r * SUBLANES
            copy(stage_sc.at[slot], pl.multiple_of(seg + off, SUBLANES), pl.multiple_of(start + off, SUBLANES),
                 SUBLANES, slot).start()
            return carry

        lax.fori_loop(0, rest, small, 0)
        n_big = n_big + full
        n_small = n_small + rest
    inflight_sm[slot * 2] = n_big
    inflight_sm[slot * 2 + 1] = n_small

    @pl.when(i == n_steps - 1)
    def _():
        def fill(first_row, n, rows):
            def body(j, carry):
                copy(zero_sc, 0, pl.multiple_of(first_row + j * rows, SUBLANES), rows, 2).start()
                return carry
            lax.fori_loop(0, n, body, 0)

        for e in range(n_experts):
            fill(fill_ref[e], fill_ref[n_experts + e], SUBLANES)
        fill(fill_ref[2 * n_experts], fill_ref[2 * n_experts + 1], DISPATCH_ROWS)
        wait_slot(0)
        wait_slot(1)
        for e in range(n_experts):
            wait_copies(fill_ref[n_experts + e], SUBLANES, 2)
        wait_copies(fill_ref[2 * n_experts + 1], DISPATCH_ROWS, 2)


def _dispatch(h4, route, starts, runs, segs, fill, n_rows, s, n_experts):
    t, d = h4.shape
    tm = _token_tile(s)
    r_stage = TOP_K * tm + SUBLANES * n_experts
    grid_spec = pltpu.PrefetchScalarGridSpec(
        num_scalar_prefetch=4,
        grid=(t // tm,),
        in_specs=[pl.BlockSpec((tm, d), lambda i, *_: (i, 0)),
                  pl.BlockSpec((tm, LANES), lambda i, *_: (i, 0))],
        out_specs=pl.BlockSpec(memory_space=pl.ANY),
        scratch_shapes=[pltpu.VMEM((2, r_stage, d), F32), pltpu.VMEM((DISPATCH_ROWS, d), F32),
                        pltpu.SMEM((4,), jnp.int32), pltpu.SemaphoreType.DMA((3,))],
    )
    return pl.pallas_call(
        functools.partial(_dispatch_kernel, n_experts=n_experts),
        grid_spec=grid_spec,
        out_shape=jax.ShapeDtypeStruct((n_rows, d), F32),
        compiler_params=_cparams("arbitrary"),
        name="moe_dispatch",
    )(starts, runs, segs, fill, h4, route)


def _expert_ffn_kernel(te_ref, nreal_ref, xs_ref, wg_ref, wu_ref, wd_ref, ys_ref, xb_sc, acc_sc):
    i = pl.program_id(0)
    j = pl.program_id(1)
    last = pl.num_programs(1) - 1
    real = i < nreal_ref[0]

    @pl.when(real & (j == 0))
    def _():
        xb_sc[...] = xs_ref[...].astype(BF16)
        acc_sc[...] = jnp.zeros(acc_sc.shape, F32)

    @pl.when(real)
    def _():
        acc_sc[...] += _swiglu_partial(xb_sc[...], wg_ref.shape[2], lambda c: wg_ref[0, :, c],
                                       lambda c: wu_ref[0, :, c], lambda c: wd_ref[0, c, :])

    @pl.when(real & (j == last))
    def _():
        ys_ref[...] = acc_sc[...]

    @pl.when(jnp.logical_not(real) & (j == last))
    def _():
        ys_ref[...] = jnp.zeros(ys_ref.shape, F32)


def _expert_ffn(xs, tile_expert, n_real, wg, wu, wd):
    p, d = xs.shape
    f = wg.shape[2]
    tm = EXPERT_TILE
    tf = _ffn_chunk(f)
    nj = f // tf

    def row_idx(i, j, te, nr):
        return (jnp.minimum(i, nr[0] - 1), 0)

    def col_j(i, j, nr):
        return jnp.where(i < nr[0], j, nj - 1)

    grid_spec = pltpu.PrefetchScalarGridSpec(
        num_scalar_prefetch=2,
        grid=(p // tm, nj),
        in_specs=[pl.BlockSpec((tm, d), row_idx),
                  pl.BlockSpec((1, d, tf), lambda i, j, te, nr: (te[i], 0, col_j(i, j, nr))),
                  pl.BlockSpec((1, d, tf), lambda i, j, te, nr: (te[i], 0, col_j(i, j, nr))),
                  pl.BlockSpec((1, tf, d), lambda i, j, te, nr: (te[i], col_j(i, j, nr), 0))],
        out_specs=pl.BlockSpec((tm, d), lambda i, j, te, nr: (i, 0)),
        scratch_shapes=[pltpu.VMEM((tm, d), BF16), pltpu.VMEM((tm, d), F32)],
    )
    return pl.pallas_call(
        _expert_ffn_kernel,
        grid_spec=grid_spec,
        out_shape=jax.ShapeDtypeStruct((p, d), F32),
        compiler_params=_cparams("arbitrary", "arbitrary"),
        name="moe_expert_ffn",
    )(tile_expert, n_real, xs, wg, wu, wd)


def _combine_kernel(start_ref, run_ref, seg_ref, ys_ref, route_ref, x_ref, mod_ref, g_ref, o_ref, buf_sc, y_sc,
                    col_sc, src_sm, exp_sm, off_sm, count_sm, sems, *, n_experts):
    i = pl.program_id(0)
    tm = x_ref.shape[0]
    d = x_ref.shape[1]
    max_chunks = buf_sc.shape[1]

    def chunk_copy(slot, ci):
        return pltpu.make_async_copy(
            ys_ref.at[pl.ds(pl.multiple_of(src_sm[slot * max_chunks + ci], SUBLANES), COMBINE_ROWS)],
            buf_sc.at[slot, ci], sems.at[slot, ci])

    def fetch(tile, slot):
        n_chunks = 0
        for e in range(n_experts):
            start = start_ref[tile * n_experts + e]
            rows = run_ref[tile * n_experts + e] * SUBLANES
            seg = seg_ref[tile * n_experts + e]
            for k in range(tm // COMBINE_ROWS):
                @pl.when(rows > k * COMBINE_ROWS)
                def _(start=start, seg=seg, k=k, ci=n_chunks + k):
                    src_sm[slot * max_chunks + ci] = start + k * COMBINE_ROWS
                    exp_sm[slot * max_chunks + ci] = e
                    off_sm[slot * max_chunks + ci] = seg + k * COMBINE_ROWS
                    chunk_copy(slot, ci).start()
            n_chunks = n_chunks + (rows + COMBINE_ROWS - 1) // COMBINE_ROWS
        count_sm[slot] = n_chunks

    slot = i % 2

    @pl.when(i == 0)
    def _():
        fetch(i, slot)

    @pl.when(i + 1 < pl.num_programs(0))
    def _():
        fetch(i + 1, 1 - slot)

    route = route_ref[...]
    for idx in range(col_sc.shape[0]):
        col_sc[idx] = jnp.broadcast_to(route[:, idx:idx + 1], col_sc.shape[1:])
    lane = lax.broadcasted_iota(jnp.int32, (tm, COMBINE_ROWS), 1).astype(F32)
    y_sc[...] = jnp.zeros(y_sc.shape, F32)

    def body(ci, carry):
        chunk_copy(slot, ci).wait()
        ef = exp_sm[slot * max_chunks + ci].astype(F32)
        target = lane + off_sm[slot * max_chunks + ci].astype(F32)
        wide = lambda idx: jnp.tile(col_sc[idx], (1, COMBINE_ROWS // LANES))
        first = col_sc[2] == ef
        hit = (wide(2) == ef) & (wide(4) == target) | (wide(3) == ef) & (wide(5) == target)
        spread = jnp.where(hit, 1.0, 0.0).astype(BF16)
        rows_out = jnp.dot(spread, buf_sc[slot, ci].astype(BF16), preferred_element_type=F32)
        weight = jnp.where(first, col_sc[0], col_sc[1])
        y_sc[...] += jnp.tile(weight, (1, d // LANES)) * rows_out
        return carry

    lax.fori_loop(0, count_sm[slot], body, 0)
    o_ref[...] = x_ref[...] + mod_ref[0, 5:6, :] * _rms(y_sc[...], g_ref[3:4, :])


def _combine(ys, starts, runs, segs, route, x2d, mod, g, s, n_experts):
    t, d = x2d.shape
    tm = _token_tile(s)
    tpb = s // tm
    max_chunks = TOP_K * tm // COMBINE_ROWS + n_experts
    grid_spec = pltpu.PrefetchScalarGridSpec(
        num_scalar_prefetch=3,
        grid=(t // tm,),
        in_specs=[pl.BlockSpec(memory_space=pl.ANY),
                  pl.BlockSpec((tm, LANES), lambda i, *_: (i, 0)),
                  pl.BlockSpec((tm, d), lambda i, *_: (i, 0)),
                  pl.BlockSpec((1, 6, d), lambda i, *_: (i // tpb, 0, 0)),
                  pl.BlockSpec((4, d), lambda i, *_: (0, 0))],
        out_specs=pl.BlockSpec((tm, d), lambda i, *_: (i, 0)),
        scratch_shapes=[pltpu.VMEM((2, max_chunks, COMBINE_ROWS, d), F32), pltpu.VMEM((tm, d), F32),
                        pltpu.VMEM((6, tm, LANES), F32),
                        pltpu.SMEM((2 * max_chunks,), jnp.int32), pltpu.SMEM((2 * max_chunks,), jnp.int32),
                        pltpu.SMEM((2 * max_chunks,), jnp.int32), pltpu.SMEM((2,), jnp.int32),
                        pltpu.SemaphoreType.DMA((2, max_chunks))],
    )
    return pl.pallas_call(
        functools.partial(_combine_kernel, n_experts=n_experts),
        grid_spec=grid_spec,
        out_shape=jax.ShapeDtypeStruct((t, d), F32),
        compiler_params=_cparams("arbitrary"),
        name="moe_combine",
    )(starts, runs, segs, ys, route, x2d, mod, g)


def _moe(h4, route, meta, x2d, mod, g, wg, wu, wd, s):
    t, d = x2d.shape
    e = wg.shape[0]
    tm = EXPERT_TILE
    n_tok_tiles = t // _token_tile(s)
    n_tiles = (TOP_K * t + SUBLANES * n_tok_tiles * e) // tm + e + 1
    runs = meta[:, 0, :e]
    segs = meta[:, 0, e:2 * e]
    used = SUBLANES * jnp.sum(runs, axis=0)
    tiles_e = (used + tm - 1) // tm
    ends = jnp.cumsum(tiles_e)
    off = (ends - tiles_e) * tm
    starts = off[None, :] + SUBLANES * (jnp.cumsum(runs, axis=0) - runs)
    n_real = ends[e - 1:e].astype(jnp.int32)
    fill = jnp.concatenate([off + used, (tiles_e * tm - used) // SUBLANES,
                            n_real * tm, (n_tiles - n_real) * (tm // DISPATCH_ROWS)]).astype(jnp.int32)
    tile_ids = jnp.arange(n_tiles, dtype=jnp.int32)
    tile_expert = jnp.minimum(jnp.sum(tile_ids[:, None] >= ends[None, :], axis=1), e - 1)
    last_expert = jnp.minimum(jnp.sum(n_real[0] - 1 >= ends), e - 1)
    tile_expert = jnp.where(tile_ids < n_real[0], tile_expert, last_expert).astype(jnp.int32)
    flat = lambda a: a.reshape(-1).astype(jnp.int32)
    xs = _dispatch(h4, route, flat(starts), flat(runs), flat(segs), fill, n_tiles * tm, s, e)
    ys = _expert_ffn(xs, tile_expert, n_real, wg, wu, wd)
    return _combine(ys, flat(starts), flat(runs), flat(segs), route, x2d, mod, g, s, e)


def kernel(x, c, mod_w, mod_b, norm_g, attn_w_in, attn_w_out, attn_lambda, attn_subln, ffn_w_gate, ffn_w_up, ffn_w_down, mlstm_w_in, mlstm_conv_w, mlstm_conv_b, mlstm_w_q, mlstm_w_k, mlstm_w_v, mlstm_w_gate, mlstm_b_gate, mlstm_skip, mlstm_norm, mlstm_w_out, moe_w_router, moe_b_router, moe_w_gate, moe_w_up, moe_w_down):
    b, s, d = x.shape
    depth = mod_w.shape[0]
    mod_all = _modulation(c, mod_w, mod_b)
    x2d = x.reshape(b * s, d)
    da = d // (2 * ATTN_HEADS)
    for i in range(depth):
        mod = mod_all[i]
        g = norm_g[i]
        j = i // N_MIXERS
        if i % N_MIXERS == 0:
            lambda_init = 0.8 - 0.6 * math.exp(-0.3 * i)
            q, k, v = _in_proj(x2d, mod, g, attn_w_in[j].astype(BF16), 3, (LOG2E * da ** -0.5, 1.0, 1.0), s,
                               "attn_in_proj")
            o = _diff_attention(q, k, v, attn_lambda[j], attn_subln[j], lambda_init, b, s)
            x2d = _attn_out_ffn(o, attn_w_out[j].astype(BF16), ffn_w_gate[j].astype(BF16),
                                ffn_w_up[j].astype(BF16), ffn_w_down[j].astype(BF16), x2d, mod, g, s)
        else:
            xm, z = _in_proj(x2d, mod, g, mlstm_w_in[j].astype(BF16), 2, (1.0, 1.0), s, "mlstm_in_proj")
            xc, q, kt, v, gates = _mlstm_qkv(xm, mlstm_conv_w[j], mlstm_conv_b[j], mlstm_w_q[j], mlstm_w_k[j],
                                             mlstm_w_v[j], mlstm_w_gate[j], mlstm_b_gate[j], b, s)
            hn = _mlstm_cell(q, kt, v, gates, mlstm_norm[j], b, s)
            di = hn.shape[2]
            x2d, h4, route, meta = _mlstm_out(hn.reshape(b * s, di), xc.reshape(b * s, di), z, mlstm_skip[j],
                                              mlstm_w_out[j].astype(BF16), x2d, mod, g,
                                              moe_w_router[j], moe_b_router[j], s)
            x2d = _moe(h4, route, meta, x2d, mod, g, moe_w_gate[j].astype(BF16), moe_w_up[j].astype(BF16),
                       moe_w_down[j].astype(BF16), s)
    return x2d.reshape(b, s, d)
```

```python
import functools
import math

import jax
import jax.numpy as jnp
from jax import lax
from jax.experimental import pallas as pl
from jax.experimental.pallas import tpu as pltpu

F32 = jnp.float32
BF16 = jnp.bfloat16
HIGHEST = lax.Precision.HIGHEST

CHUNK = 64
ATTN_HEADS = 8
ALIBI_MAX_BIAS = 8.0
MLSTM_HEADS = 4
MLSTM_CONV = 4
QKV_BLOCK = 4
TOP_K = 2
NORM_EPS = 1e-6
N_MIXERS = 2

LANES = 128
SUBLANES = 8
VMEM_LIMIT_BYTES = 56 * 1024 * 1024
FFN_SUBCHUNK = 256
EXPERT_TILE = 512
DISPATCH_ROWS = 128
COMBINE_ROWS = 256
NEG_BIG = -1e30
LOG2E = math.log2(math.e)


def _cparams(*sem):
    return pltpu.CompilerParams(dimension_semantics=sem, vmem_limit_bytes=VMEM_LIMIT_BYTES)


def _rms(x, g):
    return x * lax.rsqrt(jnp.mean(x * x, axis=-1, keepdims=True) + NORM_EPS) * g


def _silu(x):
    return x * jax.nn.sigmoid(x)


def _token_tile(s):
    return min(512, s)


def _mod_kernel(c_ref, w_ref, b_ref, o_ref):
    cond = _silu(c_ref[...])
    o_ref[0] = jnp.dot(cond, w_ref[0], preferred_element_type=F32, precision=HIGHEST) + b_ref[0]


def _modulation(c, mod_w, mod_b):
    depth, d, n = mod_w.shape
    b = c.shape[0]
    tn = 1536 if n % 1536 == 0 else n
    out = pl.pallas_call(
        _mod_kernel,
        grid=(depth, n // tn),
        in_specs=[pl.BlockSpec((b, d), lambda i, j: (0, 0)),
                  pl.BlockSpec((1, d, tn), lambda i, j: (i, 0, j)),
                  pl.BlockSpec((1, 1, tn), lambda i, j: (i, 0, j))],
        out_specs=pl.BlockSpec((1, b, tn), lambda i, j: (i, 0, j)),
        out_shape=jax.ShapeDtypeStruct((depth, b, n), F32),
        compiler_params=_cparams("arbitrary", "arbitrary"),
        name="modulation",
    )(c, mod_w, mod_b.reshape(depth, 1, n))
    return out.reshape(depth, b, 6, d)


def _in_proj_kernel(x_ref, mod_ref, g_ref, w_ref, *o_refs, scales):
    x = x_ref[...]
    h = _rms(x, g_ref[0:1, :]) * (1.0 + mod_ref[0, 1:2, :]) + mod_ref[0, 0:1, :]
    hb = h.astype(BF16)
    n = o_refs[0].shape[1]
    for idx, o_ref in enumerate(o_refs):
        r = jnp.dot(hb, w_ref[:, idx * n:(idx + 1) * n], preferred_element_type=F32)
        if scales[idx] != 1.0:
            r = r * scales[idx]
        o_ref[...] = r.astype(BF16)


def _in_proj(x2d, mod, g, w_bf16, n_out, scales, s, name):
    t, d = x2d.shape
    tm = _token_tile(s)
    tpb = s // tm
    n = w_bf16.shape[1] // n_out
    return pl.pallas_call(
        functools.partial(_in_proj_kernel, scales=scales),
        grid=(t // tm,),
        in_specs=[pl.BlockSpec((tm, d), lambda i: (i, 0)),
                  pl.BlockSpec((1, 6, d), lambda i: (i // tpb, 0, 0)),
                  pl.BlockSpec((4, d), lambda i: (0, 0)),
                  pl.BlockSpec(w_bf16.shape, lambda i: (0, 0))],
        out_specs=[pl.BlockSpec((tm, n), lambda i: (i, 0))] * n_out,
        out_shape=[jax.ShapeDtypeStruct((t, n), BF16)] * n_out,
        compiler_params=_cparams("arbitrary"),
        name=name,
    )(x2d, mod, g, w_bf16)


def _attn_kernel(slopes_ref, lam_ref, subln_ref, q_ref, k_ref, v_ref, o_ref, m_sc, acc_sc, bias_sc, vext_sc,
                 s_sc, smax_sc, *, tq, tk, lambda_init):
    slope = slopes_ref[pl.program_id(0)]
    hd = q_ref.shape[2]
    half = hd // 2
    n_q = q_ref.shape[1] // tq

    vext_sc[:, :hd] = v_ref[0]

    @pl.when(pl.program_id(1) == 0)
    def _():
        vext_sc[:, hd:] = jnp.ones((vext_sc.shape[0], hd), BF16)
        row = lax.broadcasted_iota(jnp.int32, (tq, tk), 0)
        col = lax.broadcasted_iota(jnp.int32, (tq, tk), 1)
        dist = (row - col).astype(F32)
        bias_sc[0] = -slope * dist
        bias_sc[1] = jnp.where(col // CHUNK <= row // CHUNK, -slope * jnp.abs(dist), NEG_BIG)

    lam = lam_ref[...]
    lam_full = (jnp.exp(jnp.sum(lam[0:1, :] * lam[1:2, :], axis=1, keepdims=True))
                - jnp.exp(jnp.sum(lam[2:3, :] * lam[3:4, :], axis=1, keepdims=True)) + lambda_init)

    def scores(qi, j, slot):
        q = q_ref[0, qi * tq:(qi + 1) * tq, :]
        lane = lax.broadcasted_iota(jnp.int32, q.shape, 1)
        zero = jnp.zeros_like(q)
        q2 = jnp.concatenate([jnp.where(lane < half, q, zero), jnp.where(lane >= half, q, zero)], axis=0)
        kj = k_ref[0, j * tk:(j + 1) * tk, :]
        s = lax.dot_general(q2, kj, (((1,), (1,)), ((), ())), preferred_element_type=F32)
        bias = bias_sc[1 if j == qi else 0]
        s = s + jnp.concatenate([bias, bias], axis=0)
        s_sc[slot] = s
        smax_sc[slot] = jnp.broadcast_to(jnp.max(s, axis=1, keepdims=True), smax_sc.shape[1:])

    def accumulate(qi, j, slot):
        const = 0.0 if j == qi else -slope * float((qi - j) * tq)
        m_prev = m_sc[...]
        m_next = jnp.maximum(m_prev, smax_sc[slot] + const)
        alpha = jnp.exp2(m_prev - m_next)
        p = jnp.exp2(s_sc[slot] - jnp.tile(m_next - const, (1, tk // LANES)))
        pv = jnp.dot(p.astype(BF16), vext_sc[j * tk:(j + 1) * tk, :], preferred_element_type=F32)
        acc_sc[...] = jnp.tile(alpha, (1, 2)) * acc_sc[...] + pv
        m_sc[...] = m_next

    def finalize(qi):
        acc = acc_sc[...]
        o_all = acc[:, :hd] / acc[:, hd:]
        o = o_all[:tq] - lam_full * o_all[tq:]
        o = _rms(o, subln_ref[...]) * (1.0 - lambda_init)
        o_ref[0, qi * tq:(qi + 1) * tq, :] = o.astype(BF16)

    blocks = [(qi, j) for qi in range(n_q) for j in range(qi + 1)]
    scores(*blocks[0], 0)
    for n, (qi, j) in enumerate(blocks):
        if n + 1 < len(blocks):
            scores(*blocks[n + 1], (n + 1) % 2)
        if j == 0:
            m_sc[...] = jnp.full(m_sc.shape, NEG_BIG, F32)
            acc_sc[...] = jnp.zeros(acc_sc.shape, F32)
        accumulate(qi, j, n % 2)
        if j == qi:
            finalize(qi)


def _diff_attention(q, k, v, lam, subln, lambda_init, b, s):
    d = q.shape[1]
    hd = d // ATTN_HEADS
    tq = tk = min(512, s)
    slopes = LOG2E * jnp.exp2(-ALIBI_MAX_BIAS * jnp.arange(1, ATTN_HEADS + 1, dtype=F32) / ATTN_HEADS)
    q3, k3, v3 = (a.reshape(b, s, d) for a in (q, k, v))
    grid_spec = pltpu.PrefetchScalarGridSpec(
        num_scalar_prefetch=1,
        grid=(ATTN_HEADS, b),
        in_specs=[pl.BlockSpec(lam.shape, lambda h, bi, sl: (0, 0)),
                  pl.BlockSpec((1, hd), lambda h, bi, sl: (0, 0)),
                  pl.BlockSpec((1, s, hd), lambda h, bi, sl: (bi, 0, h)),
                  pl.BlockSpec((1, s, hd), lambda h, bi, sl: (bi, 0, h)),
                  pl.BlockSpec((1, s, hd), lambda h, bi, sl: (bi, 0, h))],
        out_specs=pl.BlockSpec((1, s, hd), lambda h, bi, sl: (bi, 0, h)),
        scratch_shapes=[pltpu.VMEM((2 * tq, LANES), F32), pltpu.VMEM((2 * tq, 2 * hd), F32),
                        pltpu.VMEM((2, tq, tk), F32), pltpu.VMEM((s, 2 * hd), BF16),
                        pltpu.VMEM((2, 2 * tq, tk), F32), pltpu.VMEM((2, 2 * tq, LANES), F32)],
    )
    o = pl.pallas_call(
        functools.partial(_attn_kernel, tq=tq, tk=tk, lambda_init=lambda_init),
        grid_spec=grid_spec,
        out_shape=jax.ShapeDtypeStruct((b, s, d), BF16),
        compiler_params=_cparams("arbitrary", "arbitrary"),
        name="diff_attention",
    )(slopes, lam, subln.reshape(1, hd), q3, k3, v3)
    return o.reshape(b * s, d)


def _mixer_out_core(inp_bf16, w_ref, x_ref, mod_ref, g_ref, x_out_ref):
    y = jnp.dot(inp_bf16, w_ref[...], preferred_element_type=F32)
    x1 = x_ref[...] + mod_ref[0, 2:3, :] * _rms(y, g_ref[1:2, :])
    x_out_ref[...] = x1
    return _rms(x1, g_ref[2:3, :]) * (1.0 + mod_ref[0, 4:5, :]) + mod_ref[0, 3:4, :]


def _swiglu_partial(h, width, wg, wu, wd):
    out = None
    for c0 in range(0, width, FFN_SUBCHUNK):
        c = slice(c0, min(c0 + FFN_SUBCHUNK, width))
        gate = jnp.dot(h, wg(c), preferred_element_type=F32)
        up = jnp.dot(h, wu(c), preferred_element_type=F32)
        part = jnp.dot((_silu(gate) * up).astype(BF16), wd(c), preferred_element_type=F32)
        out = part if out is None else out + part
    return out


def _attn_out_ffn_kernel(o_ref, wo_ref, x_ref, mod_ref, g_ref, wg_ref, wu_ref, wd_ref, out_ref, x1_sc):
    h2 = _mixer_out_core(o_ref[...], wo_ref, x_ref, mod_ref, g_ref, x1_sc).astype(BF16)
    y = _swiglu_partial(h2, wg_ref.shape[1], lambda c: wg_ref[:, c], lambda c: wu_ref[:, c],
                        lambda c: wd_ref[c, :])
    out_ref[...] = x1_sc[...] + mod_ref[0, 5:6, :] * _rms(y, g_ref[3:4, :])


def _ffn_chunk(f):
    for tf in (1792, 1408, 1024, 896, 512, 256, 128):
        if f % tf == 0:
            return tf
    return f


def _attn_out_ffn(o, wo, wg, wu, wd, x2d, mod, g, s):
    t, d = x2d.shape
    f = wg.shape[1]
    tm = _token_tile(s)
    tpb = s // tm
    tok = pl.BlockSpec((tm, d), lambda i: (i, 0))
    resident = lambda w: pl.BlockSpec(w.shape, lambda i: (0, 0), pipeline_mode=pl.Buffered(1))
    return pl.pallas_call(
        _attn_out_ffn_kernel,
        grid=(t // tm,),
        in_specs=[tok,
                  resident(wo),
                  tok,
                  pl.BlockSpec((1, 6, d), lambda i: (i // tpb, 0, 0)),
                  pl.BlockSpec((4, d), lambda i: (0, 0)),
                  resident(wg), resident(wu), resident(wd)],
        out_specs=tok,
        out_shape=jax.ShapeDtypeStruct((t, d), F32),
        scratch_shapes=[pltpu.VMEM((tm, d), F32)],
        compiler_params=_cparams("arbitrary"),
        name="attn_out_dense_ffn",
    )(o, wo, x2d, mod, g, wg, wu, wd)


def _mlstm_qkv_kernel(xm_ref, cw_ref, cb_ref, wqk_ref, wv_ref, wgq_ref, wgk_ref, wgv_ref, bg_ref,
                      xc_ref, q_ref, kt_ref, v_ref, gates_ref, pad_sc, *, ts, k_scale):
    c = pl.program_id(1)
    s, cw = xm_ref.shape[1], xm_ref.shape[2]
    front = SUBLANES
    pad_sc[0:front, :] = jnp.zeros((front, cw), F32)
    pad_sc[front:front + s, :] = xm_ref[0].astype(F32)

    @pl.when(c == 0)
    def _():
        gates_ref[0] = jnp.broadcast_to(bg_ref[...], gates_ref.shape[1:])

    for r in range(s // ts):
        r0 = r * ts
        conv = cb_ref[0]
        for j in reversed(range(MLSTM_CONV)):
            start = r0 + front - (MLSTM_CONV - 1) + j
            conv = conv + pad_sc[start:start + ts, :] * cw_ref[0, j:j + 1, :]
        xc = _silu(conv)
        xcb = xc.astype(BF16)
        qk = jnp.dot(xcb, wqk_ref[0], preferred_element_type=F32)
        qb = qk[:, :cw].astype(BF16)
        kb = qk[:, cw:].astype(BF16)
        vb = jnp.dot(xm_ref[0, r0:r0 + ts, :], wv_ref[0], preferred_element_type=F32).astype(BF16)
        gates_ref[0, r0:r0 + ts, :] += (jnp.dot(qb, wgq_ref[...], preferred_element_type=F32)
                                        + jnp.dot(kb, wgk_ref[...], preferred_element_type=F32)
                                        + jnp.dot(vb, wgv_ref[...], preferred_element_type=F32))
        xc_ref[0, r0:r0 + ts, :] = xcb
        q_ref[0, r0:r0 + ts, :] = qb
        kt_ref[0, :, r0:r0 + ts] = jnp.transpose(qk[:, cw:] * k_scale).astype(BF16)
        v_ref[0, r0:r0 + ts, :] = vb


def _block_diag(w, cw):
    g, qb, _ = w.shape
    per = cw // qb
    wr = w.reshape(g // per, per, qb, qb)
    eye = jnp.eye(per, dtype=w.dtype)
    return jnp.einsum("cgio,gh->cgiho", wr, eye).reshape(g // per, cw, cw)


def _mlstm_qkv(xm, conv_w, conv_b, w_q, w_k, w_v, w_gate, b_gate, b, s):
    di = xm.shape[1]
    cw = 256
    nchunk = di // cw
    nh = MLSTM_HEADS
    dh = di // nh
    ts = min(256, s)
    wqk = jnp.concatenate([_block_diag(w_q, cw), _block_diag(w_k, cw)], axis=2).astype(BF16)
    wv = _block_diag(w_v, cw).astype(BF16)
    wg = jnp.zeros((3 * di, LANES), F32).at[:, :2 * nh].set(w_gate).astype(BF16)
    bg = jnp.zeros((1, LANES), F32).at[0, :2 * nh].set(b_gate)
    xm3 = xm.reshape(b, s, di)
    blk = pl.BlockSpec((1, s, cw), lambda bi, c: (bi, 0, c))
    outs = pl.pallas_call(
        functools.partial(_mlstm_qkv_kernel, ts=ts, k_scale=dh ** -0.5),
        grid=(b, nchunk),
        in_specs=[blk,
                  pl.BlockSpec((1, MLSTM_CONV, cw), lambda bi, c: (c, 0, 0)),
                  pl.BlockSpec((1, 1, cw), lambda bi, c: (c, 0, 0)),
                  pl.BlockSpec((1, cw, 2 * cw), lambda bi, c: (c, 0, 0)),
                  pl.BlockSpec((1, cw, cw), lambda bi, c: (c, 0, 0)),
                  pl.BlockSpec((cw, LANES), lambda bi, c: (c, 0)),
                  pl.BlockSpec((cw, LANES), lambda bi, c: (nchunk + c, 0)),
                  pl.BlockSpec((cw, LANES), lambda bi, c: (2 * nchunk + c, 0)),
                  pl.BlockSpec((1, LANES), lambda bi, c: (0, 0))],
        out_specs=[blk, blk, pl.BlockSpec((1, cw, s), lambda bi, c: (bi, c, 0)), blk,
                   pl.BlockSpec((1, s, LANES), lambda bi, c: (bi, 0, 0))],
        out_shape=[jax.ShapeDtypeStruct((b, s, di), BF16)] * 2 + [jax.ShapeDtypeStruct((b, di, s), BF16)]
        + [jax.ShapeDtypeStruct((b, s, di), BF16), jax.ShapeDtypeStruct((b, s, LANES), F32)],
        scratch_shapes=[pltpu.VMEM((s + SUBLANES, cw), F32)],
        compiler_params=_cparams("arbitrary", "arbitrary"),
        name="mlstm_qkv",
    )(xm3, conv_w.reshape(MLSTM_CONV, nchunk, cw).transpose(1, 0, 2), conv_b.reshape(nchunk, 1, cw),
      wqk, wv, wg, wg, wg, bg)
    return outs


def _log_sigmoid(x):
    return jnp.minimum(x, 0.0) - jnp.log1p(jnp.exp(-jnp.abs(x)))


def _split3(x):
    hi = x.astype(BF16)
    r = x - hi.astype(F32)
    mid = r.astype(BF16)
    lo = (r - mid.astype(F32)).astype(BF16)
    return hi, mid, lo


def _mlstm_cell_kernel(q_ref, kt_ref, v_ref, gc_ref, gr_ref, ng_ref, o_ref, ct_sc, n_sc, m_sc, *, nh):
    c = pl.program_id(1)
    L = q_ref.shape[1]
    dh = q_ref.shape[2] // nh

    @pl.when(c == 0)
    def _():
        ct_sc[...] = jnp.zeros(ct_sc.shape, F32)
        n_sc[...] = jnp.zeros(n_sc.shape, F32)
        m_sc[...] = jnp.zeros(m_sc.shape, F32)

    t_idx = lax.broadcasted_iota(jnp.int32, (L, L), 0)
    s_idx = lax.broadcasted_iota(jnp.int32, (L, L), 1)
    causal = s_idx <= t_idx
    tri = causal.astype(BF16)
    tri_t = (t_idx <= s_idx).astype(F32)
    gc = gc_ref[0]
    gr = gr_ref[0]
    b_cols = sum(jnp.dot(tri, part, preferred_element_type=F32) for part in _split3(_log_sigmoid(gc)))
    b_rows = jnp.dot(_log_sigmoid(gr), tri_t, preferred_element_type=F32, precision=HIGHEST)
    lane = lax.broadcasted_iota(jnp.int32, gc.shape, 1)

    for hl in range(nh):
        h = hl
        cols = slice(hl * dh, (hl + 1) * dh)
        i_col = jnp.sum(jnp.where(lane == h, gc, 0.0), axis=1, keepdims=True)
        b_col = jnp.sum(jnp.where(lane == h + nh, b_cols, 0.0), axis=1, keepdims=True)
        i_row = gr[h:h + 1, :]
        b_row = b_rows[nh + h:nh + h + 1, :]
        u_row = i_row - b_row
        u_col = i_col - b_col

        m_prev = m_sc[hl, 0:1, 0:1]
        dlog = jnp.where(causal, b_col + u_row, NEG_BIG)
        g = b_col + m_prev
        m_t = jnp.maximum(g, jnp.max(dlog, axis=1, keepdims=True))
        qb = q_ref[0, :, cols]
        ktb = kt_ref[0, cols, :]
        vb = v_ref[0, :, cols]
        w = jnp.exp(dlog - m_t) * jnp.dot(qb, ktb, preferred_element_type=F32)
        inter = jnp.exp(g - m_t)
        num = (jnp.dot(w.astype(BF16), vb, preferred_element_type=F32)
               + inter * jnp.dot(qb, ct_sc[hl].astype(BF16), preferred_element_type=F32))
        den = (jnp.sum(w, axis=1, keepdims=True)
               + inter * jnp.dot(qb, n_sc[hl].astype(BF16), preferred_element_type=F32))
        scale = 1.0 / jnp.maximum(jnp.abs(den), jnp.exp(-m_t))
        hh = num * jnp.tile(scale, (1, dh // LANES))

        mu = jnp.mean(hh, axis=1, keepdims=True)
        cen = hh - mu
        var = jnp.mean(cen * cen, axis=1, keepdims=True)
        o_ref[0, :, cols] = (cen * lax.rsqrt(var + NORM_EPS) * ng_ref[h]).astype(BF16)

        b_last = b_row[:, L - 1:L]
        m_new = jnp.maximum(b_last + m_prev, jnp.max(b_last + u_row, axis=1, keepdims=True))
        decay = jnp.exp(b_last + m_prev - m_new)
        ws_col = jnp.exp(b_last - m_new + u_col).astype(BF16)
        ct_sc[hl] = decay * ct_sc[hl] + jnp.dot(ktb, vb * ws_col, preferred_element_type=F32)
        n_sc[hl] = decay * n_sc[hl] + jnp.dot(ktb, jnp.broadcast_to(ws_col, (L, LANES)),
                                              preferred_element_type=F32)
        m_sc[hl] = jnp.broadcast_to(m_new, m_sc.shape[1:])


def _mlstm_cell(q, kt, v, gates, norm_g, b, s):
    di = q.shape[2]
    nh = MLSTM_HEADS
    dh = di // nh
    L = min(256, s)
    gates_t = jnp.transpose(gates[:, :, :SUBLANES], (0, 2, 1))
    blk = pl.BlockSpec((1, L, di), lambda bi, c: (bi, c, 0))
    return pl.pallas_call(
        functools.partial(_mlstm_cell_kernel, nh=nh),
        grid=(b, s // L),
        in_specs=[blk,
                  pl.BlockSpec((1, di, L), lambda bi, c: (bi, 0, c)),
                  blk,
                  pl.BlockSpec((1, L, LANES), lambda bi, c: (bi, c, 0)),
                  pl.BlockSpec((1, SUBLANES, L), lambda bi, c: (bi, 0, c)),
                  pl.BlockSpec((nh, 1, dh), lambda bi, c: (0, 0, 0))],
        out_specs=blk,
        out_shape=jax.ShapeDtypeStruct((b, s, di), BF16),
        scratch_shapes=[pltpu.VMEM((nh, dh, dh), F32), pltpu.VMEM((nh, dh, LANES), F32),
                        pltpu.VMEM((nh, SUBLANES, LANES), F32)],
        compiler_params=_cparams("arbitrary", "arbitrary"),
        name="mlstm_cell",
    )(q, kt, v, gates, gates_t, norm_g.reshape(nh, 1, dh))


def _mlstm_out_kernel(hn_ref, xc_ref, z_ref, skip_ref, w_ref, x_ref, mod_ref, g_ref, wr_ref, br_ref,
                      x_out_ref, h_out_ref, route_ref, meta_ref, *, n_experts):
    inner = ((hn_ref[...].astype(F32) + skip_ref[...] * xc_ref[...].astype(F32))
             * _silu(z_ref[...].astype(F32)))
    h4 = _mixer_out_core(inner.astype(BF16), w_ref, x_ref, mod_ref, g_ref, x_out_ref)
    h_hi = h4.astype(BF16)
    h_out_ref[...] = h_hi

    tm = h4.shape[0]
    lane = lax.broadcasted_iota(jnp.int32, (tm, LANES), 1)
    h_lo = (h4 - h_hi.astype(F32)).astype(BF16)
    w_hi = wr_ref[...].astype(BF16)
    w_lo = (wr_ref[...] - w_hi.astype(F32)).astype(BF16)
    hi_terms = jnp.dot(h_hi, jnp.concatenate([w_hi, w_lo], axis=1), preferred_element_type=F32)
    logits = (hi_terms[:, :LANES] + hi_terms[:, LANES:] + jnp.dot(h_lo, w_hi, preferred_element_type=F32)
              + br_ref[...])
    logits = jnp.where(lane < n_experts, logits, NEG_BIG)
    ex = jnp.exp(logits - jnp.max(logits, axis=1, keepdims=True))
    probs = ex / jnp.sum(ex, axis=1, keepdims=True)
    probs = jnp.where(lane < n_experts, probs, -1.0)
    lane_f = lane.astype(F32)
    p0 = jnp.max(probs, axis=1, keepdims=True)
    e0 = jnp.min(jnp.where(probs == p0, lane_f, float(LANES)), axis=1, keepdims=True)
    rest = jnp.where(lane_f == e0, -1.0, probs)
    p1 = jnp.max(rest, axis=1, keepdims=True)
    e1 = jnp.min(jnp.where(rest == p1, lane_f, float(LANES)), axis=1, keepdims=True)
    tot = p0 + p1
    sel0 = lane_f == e0
    sel1 = lane_f == e1
    sel = jnp.where(sel0 | sel1, 1.0, 0.0)
    r_idx = lax.broadcasted_iota(jnp.int32, (tm, tm), 0)
    c_idx = lax.broadcasted_iota(jnp.int32, (tm, tm), 1)
    before = (c_idx < r_idx).astype(BF16)
    cum = jnp.dot(before, sel.astype(BF16), preferred_element_type=F32)
    run8 = jnp.floor((jnp.sum(sel, axis=0, keepdims=True) + 7.0) * 0.125)
    e_r = lax.broadcasted_iota(jnp.int32, (LANES, LANES), 0)
    e_c = lax.broadcasted_iota(jnp.int32, (LANES, LANES), 1)
    seg8 = jnp.dot(jnp.broadcast_to(run8, (SUBLANES, LANES)).astype(BF16), (e_r < e_c).astype(BF16),
                   preferred_element_type=F32)[0:1, :]
    pos = cum + 8.0 * seg8
    pos0 = jnp.sum(jnp.where(sel0, pos, 0.0), axis=1, keepdims=True)
    pos1 = jnp.sum(jnp.where(sel1, pos, 0.0), axis=1, keepdims=True)
    vals = (p0 / tot, p1 / tot, e0, e1, pos0, pos1)
    route = jnp.zeros((tm, LANES), F32)
    for idx, val in enumerate(vals):
        route = jnp.where(lane == idx, val, route)
    route_ref[...] = route
    meta = jnp.where(lane[0:1, :] < n_experts, run8, 0.0)
    meta = jnp.where((lane[0:1, :] >= n_experts) & (lane[0:1, :] < 2 * n_experts),
                     pltpu.roll(8.0 * seg8, n_experts, 1), meta)
    meta_ref[0] = jnp.broadcast_to(meta, meta_ref.shape[1:]).astype(jnp.int32)


def _mlstm_out(hn, xc, z, skip, w_bf16, x2d, mod, g, w_router, b_router, s):
    t, d = x2d.shape
    di = hn.shape[1]
    e = w_router.shape[1]
    tm = _token_tile(s)
    tpb = s // tm
    n_tok_tiles = t // tm
    wr = jnp.zeros((d, LANES), F32).at[:, :e].set(w_router)
    br = jnp.zeros((1, LANES), F32).at[0, :e].set(b_router)
    tok = lambda n: pl.BlockSpec((tm, n), lambda i: (i, 0))
    return pl.pallas_call(
        functools.partial(_mlstm_out_kernel, n_experts=e),
        grid=(n_tok_tiles,),
        in_specs=[tok(di), tok(di), tok(di),
                  pl.BlockSpec((1, di), lambda i: (0, 0)),
                  pl.BlockSpec(w_bf16.shape, lambda i: (0, 0)),
                  tok(d),
                  pl.BlockSpec((1, 6, d), lambda i: (i // tpb, 0, 0)),
                  pl.BlockSpec((4, d), lambda i: (0, 0)),
                  pl.BlockSpec((d, LANES), lambda i: (0, 0)),
                  pl.BlockSpec((1, LANES), lambda i: (0, 0))],
        out_specs=[tok(d), tok(d), tok(LANES), pl.BlockSpec((1, SUBLANES, LANES), lambda i: (i, 0, 0))],
        out_shape=[jax.ShapeDtypeStruct((t, d), F32), jax.ShapeDtypeStruct((t, d), BF16),
                   jax.ShapeDtypeStruct((t, LANES), F32),
                   jax.ShapeDtypeStruct((n_tok_tiles, SUBLANES, LANES), jnp.int32)],
        compiler_params=_cparams("arbitrary"),
        name="mlstm_out_router",
    )(hn, xc, z, skip.reshape(1, di), w_bf16, x2d, mod, g, wr, br)


def _dispatch_kernel(start_ref, run_ref, seg_ref, fill_ref, h_ref, route_ref, xs_ref, stage_sc, zero_sc,
                     inflight_sm, sems, *, n_experts):
    i = pl.program_id(0)
    n_steps = pl.num_programs(0)
    tm = h_ref.shape[0]
    slot = i % 2
    sizes = (DISPATCH_ROWS, SUBLANES)

    def copy(src, src_row, dst_row, rows, sem_slot):
        return pltpu.make_async_copy(src.at[pl.ds(src_row, rows)], xs_ref.at[pl.ds(dst_row, rows)],
                                     sems.at[sem_slot])

    def wait_copies(n, rows, sem_slot):
        def body(_, carry):
            copy(zero_sc, 0, 0, rows, sem_slot).wait()
            return carry
        lax.fori_loop(0, n, body, 0)

    def wait_slot(sl):
        for which, rows in enumerate(sizes):
            wait_copies(inflight_sm[sl * 2 + which], rows, sl)

    @pl.when(i == 0)
    def _():
        for idx in range(4):
            inflight_sm[idx] = 0
        zero_sc[...] = jnp.zeros(zero_sc.shape, F32)

    route_t = jnp.transpose(route_ref[...])
    r_stage = stage_sc.shape[1]
    slot_row = lax.broadcasted_iota(jnp.int32, (r_stage, tm), 0).astype(F32)
    onehot = jnp.where((route_t[4:5, :] == slot_row) | (route_t[5:6, :] == slot_row), 1.0, 0.0).astype(BF16)
    sorted_rows = jnp.dot(onehot, h_ref[...], preferred_element_type=F32)

    wait_slot(slot)
    stage_sc[slot] = sorted_rows
    n_big = 0
    n_small = 0
    for e in range(n_experts):
        rows = run_ref[i * n_experts + e] * SUBLANES
        start = start_ref[i * n_experts + e]
        seg = seg_ref[i * n_experts + e]
        full = rows // DISPATCH_ROWS
        for k in range(tm // DISPATCH_ROWS):
            @pl.when(k < full)
            def _(seg=seg, start=start, k=k):
                copy(stage_sc.at[slot], pl.multiple_of(seg + k * DISPATCH_ROWS, SUBLANES),
                     pl.multiple_of(start + k * DISPATCH_ROWS, SUBLANES), DISPATCH_ROWS, slot).start()
        rest = (rows - full * DISPATCH_ROWS) // SUBLANES

        def small(r, carry, seg=seg, start=start, full=full):
            off = full * DISPATCH_ROWS + r * SUBLANES
            copy(stage_sc.at[slot], pl.multiple_of(seg + off, SUBLANES), pl.multiple_of(start + off, SUBLANES),
                 SUBLANES, slot).start()
            return carry

        lax.fori_loop(0, rest, small, 0)
        n_big = n_big + full
        n_small = n_small + rest
    inflight_sm[slot * 2] = n_big
    inflight_sm[slot * 2 + 1] = n_small

    @pl.when(i == n_steps - 1)
    def _():
        def fill(first_row, n, rows):
            def body(j, carry):
                copy(zero_sc, 0, pl.multiple_of(first_row + j * rows, SUBLANES), rows, 2).start()
                return carry
            lax.fori_loop(0, n, body, 0)

        for e in range(n_experts):
            fill(fill_ref[e], fill_ref[n_experts + e], SUBLANES)
        fill(fill_ref[2 * n_experts], fill_ref[2 * n_experts + 1], DISPATCH_ROWS)
        wait_slot(0)
        wait_slot(1)
        for e in range(n_experts):
            wait_copies(fill_ref[n_experts + e], SUBLANES, 2)
        wait_copies(fill_ref[2 * n_experts + 1], DISPATCH_ROWS, 2)


def _dispatch(h4, route, starts, runs, segs, fill, n_rows, s, n_experts):
    t, d = h4.shape
    tm = _token_tile(s)
    r_stage = TOP_K * tm + SUBLANES * n_experts
    grid_spec = pltpu.PrefetchScalarGridSpec(
        num_scalar_prefetch=4,
        grid=(t // tm,),
        in_specs=[pl.BlockSpec((tm, d), lambda i, *_: (i, 0)),
                  pl.BlockSpec((tm, LANES), lambda i, *_: (i, 0))],
        out_specs=pl.BlockSpec(memory_space=pl.ANY),
        scratch_shapes=[pltpu.VMEM((2, r_stage, d), F32), pltpu.VMEM((DISPATCH_ROWS, d), F32),
                        pltpu.SMEM((4,), jnp.int32), pltpu.SemaphoreType.DMA((3,))],
    )
    return pl.pallas_call(
        functools.partial(_dispatch_kernel, n_experts=n_experts),
        grid_spec=grid_spec,
        out_shape=jax.ShapeDtypeStruct((n_rows, d), F32),
        compiler_params=_cparams("arbitrary"),
        name="moe_dispatch",
    )(starts, runs, segs, fill, h4, route)


def _expert_ffn_kernel(te_ref, nreal_ref, xs_ref, wg_ref, wu_ref, wd_ref, ys_ref):
    real = pl.program_id(0) < nreal_ref[0]

    @pl.when(real)
    def _():
        ys_ref[...] = _swiglu_partial(xs_ref[...].astype(BF16), wg_ref.shape[2], lambda c: wg_ref[0, :, c],
                                      lambda c: wu_ref[0, :, c], lambda c: wd_ref[0, c, :])

    @pl.when(jnp.logical_not(real))
    def _():
        ys_ref[...] = jnp.zeros(ys_ref.shape, F32)


def _expert_ffn(xs, tile_expert, n_real, wg, wu, wd):
    p, d = xs.shape
    tm = EXPERT_TILE
    expert_block = lambda w: pl.BlockSpec((1,) + w.shape[1:], lambda i, te, nr: (te[i], 0, 0),
                                          pipeline_mode=pl.Buffered(1))
    grid_spec = pltpu.PrefetchScalarGridSpec(
        num_scalar_prefetch=2,
        grid=(p // tm,),
        in_specs=[pl.BlockSpec((tm, d), lambda i, te, nr: (jnp.minimum(i, nr[0] - 1), 0)),
                  expert_block(wg), expert_block(wu), expert_block(wd)],
        out_specs=pl.BlockSpec((tm, d), lambda i, te, nr: (i, 0)),
    )
    return pl.pallas_call(
        _expert_ffn_kernel,
        grid_spec=grid_spec,
        out_shape=jax.ShapeDtypeStruct((p, d), F32),
        compiler_params=_cparams("arbitrary"),
        name="moe_expert_ffn",
    )(tile_expert, n_real, xs, wg, wu, wd)


def _combine_kernel(start_ref, run_ref, seg_ref, ys_ref, route_ref, x_ref, mod_ref, g_ref, o_ref, buf_sc, y_sc,
                    col_sc, src_sm, exp_sm, off_sm, count_sm, sems, *, n_experts):
    i = pl.program_id(0)
    tm = x_ref.shape[0]
    d = x_ref.shape[1]
    max_chunks = buf_sc.shape[1]

    def chunk_copy(slot, ci):
        return pltpu.make_async_copy(
            ys_ref.at[pl.ds(pl.multiple_of(src_sm[slot * max_chunks + ci], SUBLANES), COMBINE_ROWS)],
            buf_sc.at[slot, ci], sems.at[slot, ci])

    def fetch(tile, slot):
        n_chunks = 0
        for e in range(n_experts):
            start = start_ref[tile * n_experts + e]
            rows = run_ref[tile * n_experts + e] * SUBLANES
            seg = seg_ref[tile * n_experts + e]
            for k in range(tm // COMBINE_ROWS):
                @pl.when(rows > k * COMBINE_ROWS)
                def _(start=start, seg=seg, k=k, ci=n_chunks + k):
                    src_sm[slot * max_chunks + ci] = start + k * COMBINE_ROWS
                    exp_sm[slot * max_chunks + ci] = e
                    off_sm[slot * max_chunks + ci] = seg + k * COMBINE_ROWS
                    chunk_copy(slot, ci).start()
            n_chunks = n_chunks + (rows + COMBINE_ROWS - 1) // COMBINE_ROWS
        count_sm[slot] = n_chunks

    slot = i % 2

    @pl.when(i == 0)
    def _():
        fetch(i, slot)

    @pl.when(i + 1 < pl.num_programs(0))
    def _():
        fetch(i + 1, 1 - slot)

    route = route_ref[...]
    for idx in range(col_sc.shape[0]):
        col_sc[idx] = jnp.broadcast_to(route[:, idx:idx + 1], col_sc.shape[1:])
    lane = lax.broadcasted_iota(jnp.int32, (tm, COMBINE_ROWS), 1).astype(F32)
    y_sc[...] = jnp.zeros(y_sc.shape, F32)

    def body(ci, carry):
        chunk_copy(slot, ci).wait()
        ef = exp_sm[slot * max_chunks + ci].astype(F32)
        target = lane + off_sm[slot * max_chunks + ci].astype(F32)
        wide = lambda idx: jnp.tile(col_sc[idx], (1, COMBINE_ROWS // LANES))
        first = col_sc[2] == ef
        hit = (wide(2) == ef) & (wide(4) == target) | (wide(3) == ef) & (wide(5) == target)
        spread = jnp.where(hit, 1.0, 0.0).astype(BF16)
        rows_out = jnp.dot(spread, buf_sc[slot, ci].astype(BF16), preferred_element_type=F32)
        weight = jnp.where(first, col_sc[0], col_sc[1])
        y_sc[...] += jnp.tile(weight, (1, d // LANES)) * rows_out
        return carry

    lax.fori_loop(0, count_sm[slot], body, 0)
    o_ref[...] = x_ref[...] + mod_ref[0, 5:6, :] * _rms(y_sc[...], g_ref[3:4, :])


def _combine(ys, starts, runs, segs, route, x2d, mod, g, s, n_experts):
    t, d = x2d.shape
    tm = _token_tile(s)
    tpb = s // tm
    max_chunks = TOP_K * tm // COMBINE_ROWS + n_experts
    grid_spec = pltpu.PrefetchScalarGridSpec(
        num_scalar_prefetch=3,
        grid=(t // tm,),
        in_specs=[pl.BlockSpec(memory_space=pl.ANY),
                  pl.BlockSpec((tm, LANES), lambda i, *_: (i, 0)),
                  pl.BlockSpec((tm, d), lambda i, *_: (i, 0)),
                  pl.BlockSpec((1, 6, d), lambda i, *_: (i // tpb, 0, 0)),
                  pl.BlockSpec((4, d), lambda i, *_: (0, 0))],
        out_specs=pl.BlockSpec((tm, d), lambda i, *_: (i, 0)),
        scratch_shapes=[pltpu.VMEM((2, max_chunks, COMBINE_ROWS, d), F32), pltpu.VMEM((tm, d), F32),
                        pltpu.VMEM((6, tm, LANES), F32),
                        pltpu.SMEM((2 * max_chunks,), jnp.int32), pltpu.SMEM((2 * max_chunks,), jnp.int32),
                        pltpu.SMEM((2 * max_chunks,), jnp.int32), pltpu.SMEM((2,), jnp.int32),
                        pltpu.SemaphoreType.DMA((2, max_chunks))],
    )
    return pl.pallas_call(
        functools.partial(_combine_kernel, n_experts=n_experts),
        grid_spec=grid_spec,
        out_shape=jax.ShapeDtypeStruct((t, d), F32),
        compiler_params=_cparams("arbitrary"),
        name="moe_combine",
    )(starts, runs, segs, ys, route, x2d, mod, g)


def _moe(h4, route, meta, x2d, mod, g, wg, wu, wd, s):
    t, d = x2d.shape
    e = wg.shape[0]
    tm = EXPERT_TILE
    n_tok_tiles = t // _token_tile(s)
    n_tiles = (TOP_K * t + SUBLANES * n_tok_tiles * e) // tm + e + 1
    runs = meta[:, 0, :e]
    segs = meta[:, 0, e:2 * e]
    used = SUBLANES * jnp.sum(runs, axis=0)
    tiles_e = (used + tm - 1) // tm
    ends = jnp.cumsum(tiles_e)
    off = (ends - tiles_e) * tm
    starts = off[None, :] + SUBLANES * (jnp.cumsum(runs, axis=0) - runs)
    n_real = ends[e - 1:e].astype(jnp.int32)
    fill = jnp.concatenate([off + used, (tiles_e * tm - used) // SUBLANES,
                            n_real * tm, (n_tiles - n_real) * (tm // DISPATCH_ROWS)]).astype(jnp.int32)
    tile_ids = jnp.arange(n_tiles, dtype=jnp.int32)
    tile_expert = jnp.minimum(jnp.sum(tile_ids[:, None] >= ends[None, :], axis=1), e - 1)
    last_expert = jnp.minimum(jnp.sum(n_real[0] - 1 >= ends), e - 1)
    tile_expert = jnp.where(tile_ids < n_real[0], tile_expert, last_expert).astype(jnp.int32)
    flat = lambda a: a.reshape(-1).astype(jnp.int32)
    xs = _dispatch(h4, route, flat(starts), flat(runs), flat(segs), fill, n_tiles * tm, s, e)
    ys = _expert_ffn(xs, tile_expert, n_real, wg, wu, wd)
    return _combine(ys, flat(starts), flat(runs), flat(segs), route, x2d, mod, g, s, e)


def kernel(x, c, mod_w, mod_b, norm_g, attn_w_in, attn_w_out, attn_lambda, attn_subln, ffn_w_gate, ffn_w_up, ffn_w_down, mlstm_w_in, mlstm_conv_w, mlstm_conv_b, mlstm_w_q, mlstm_w_k, mlstm_w_v, mlstm_w_gate, mlstm_b_gate, mlstm_skip, mlstm_norm, mlstm_w_out, moe_w_router, moe_b_router, moe_w_gate, moe_w_up, moe_w_down):
    b, s, d = x.shape
    depth = mod_w.shape[0]
    mod_all = _modulation(c, mod_w, mod_b)
    x2d = x.reshape(b * s, d)
    da = d // (2 * ATTN_HEADS)
    for i in range(depth):
        mod = mod_all[i]
        g = norm_g[i]
        j = i // N_MIXERS
        if i % N_MIXERS == 0:
            lambda_init = 0.8 - 0.6 * math.exp(-0.3 * i)
            q, k, v = _in_proj(x2d, mod, g, attn_w_in[j].astype(BF16), 3, (LOG2E * da ** -0.5, 1.0, 1.0), s,
                               "attn_in_proj")
            o = _diff_attention(q, k, v, attn_lambda[j], attn_subln[j], lambda_init, b, s)
            x2d = _attn_out_ffn(o, attn_w_out[j].astype(BF16), ffn_w_gate[j].astype(BF16),
                                ffn_w_up[j].astype(BF16), ffn_w_down[j].astype(BF16), x2d, mod, g, s)
        else:
            xm, z = _in_proj(x2d, mod, g, mlstm_w_in[j].astype(BF16), 2, (1.0, 1.0), s, "mlstm_in_proj")
            xc, q, kt, v, gates = _mlstm_qkv(xm, mlstm_conv_w[j], mlstm_conv_b[j], mlstm_w_q[j], mlstm_w_k[j],
                                             mlstm_w_v[j], mlstm_w_gate[j], mlstm_b_gate[j], b, s)
            hn = _mlstm_cell(q, kt, v, gates, mlstm_norm[j], b, s)
            di = hn.shape[2]
            x2d, h4, route, meta = _mlstm_out(hn.reshape(b * s, di), xc.reshape(b * s, di), z, mlstm_skip[j],
                                              mlstm_w_out[j].astype(BF16), x2d, mod, g,
                                              moe_w_router[j], moe_b_router[j], s)
            x2d = _moe(h4, route, meta, x2d, mod, g, moe_w_gate[j].astype(BF16), moe_w_up[j].astype(BF16),
                       moe_w_down[j].astype(BF16), s)
    return x2d.reshape(b, s, d)
```

```python
import functools
import math

import jax
import jax.numpy as jnp
from jax import lax
from jax.experimental import pallas as pl
from jax.experimental.pallas import tpu as pltpu

F32 = jnp.float32
BF16 = jnp.bfloat16
HIGHEST = lax.Precision.HIGHEST

CHUNK = 64
ATTN_HEADS = 8
ALIBI_MAX_BIAS = 8.0
MLSTM_HEADS = 4
MLSTM_CONV = 4
TOP_K = 2
NORM_EPS = 1e-6
N_MIXERS = 2

LANES = 128
SUBLANES = 8
VMEM_LIMIT_BYTES = 56 * 1024 * 1024
FFN_SUBCHUNK = 256
EXPERT_TILE = 512
DISPATCH_ROWS = 128
COMBINE_ROWS = 256
NEG_BIG = -1e30
LOG2E = math.log2(math.e)


def _cparams(*sem):
    return pltpu.CompilerParams(dimension_semantics=sem, vmem_limit_bytes=VMEM_LIMIT_BYTES)


def _rms(x, g):
    return x * lax.rsqrt(jnp.mean(x * x, axis=-1, keepdims=True) + NORM_EPS) * g


def _silu(x):
    return x * jax.nn.sigmoid(x)


def _token_tile(s):
    return min(512, s)


def _mod_kernel(c_ref, w_ref, b_ref, o_ref):
    cond = _silu(c_ref[...])
    o_ref[0] = jnp.dot(cond, w_ref[0], preferred_element_type=F32, precision=HIGHEST) + b_ref[0]


def _modulation(c, mod_w, mod_b):
    depth, d, n = mod_w.shape
    b = c.shape[0]
    tn = 1536 if n % 1536 == 0 else n
    out = pl.pallas_call(
        _mod_kernel,
        grid=(depth, n // tn),
        in_specs=[pl.BlockSpec((b, d), lambda i, j: (0, 0)),
                  pl.BlockSpec((1, d, tn), lambda i, j: (i, 0, j)),
                  pl.BlockSpec((1, 1, tn), lambda i, j: (i, 0, j))],
        out_specs=pl.BlockSpec((1, b, tn), lambda i, j: (i, 0, j)),
        out_shape=jax.ShapeDtypeStruct((depth, b, n), F32),
        compiler_params=_cparams("arbitrary", "arbitrary"),
        name="modulation",
    )(c, mod_w, mod_b.reshape(depth, 1, n))
    return out.reshape(depth, b, 6, d)


def _in_proj_kernel(x_ref, mod_ref, g_ref, w_ref, *o_refs, scales):
    x = x_ref[...]
    h = _rms(x, g_ref[0:1, :]) * (1.0 + mod_ref[0, 1:2, :]) + mod_ref[0, 0:1, :]
    hb = h.astype(BF16)
    n = o_refs[0].shape[1]
    for idx, o_ref in enumerate(o_refs):
        r = jnp.dot(hb, w_ref[:, idx * n:(idx + 1) * n], preferred_element_type=F32)
        if scales[idx] != 1.0:
            r = r * scales[idx]
        o_ref[...] = r.astype(BF16)


def _in_proj(x2d, mod, g, w_bf16, n_out, scales, s, name):
    t, d = x2d.shape
    tm = _token_tile(s)
    tpb = s // tm
    n = w_bf16.shape[1] // n_out
    return pl.pallas_call(
        functools.partial(_in_proj_kernel, scales=scales),
        grid=(t // tm,),
        in_specs=[pl.BlockSpec((tm, d), lambda i: (i, 0)),
                  pl.BlockSpec((1, 6, d), lambda i: (i // tpb, 0, 0)),
                  pl.BlockSpec((4, d), lambda i: (0, 0)),
                  pl.BlockSpec(w_bf16.shape, lambda i: (0, 0))],
        out_specs=[pl.BlockSpec((tm, n), lambda i: (i, 0))] * n_out,
        out_shape=[jax.ShapeDtypeStruct((t, n), BF16)] * n_out,
        compiler_params=_cparams("arbitrary"),
        name=name,
    )(x2d, mod, g, w_bf16)


def _attn_kernel(slopes_ref, lam_ref, subln_ref, q_ref, k_ref, v_ref, o_ref, m_sc, acc_sc, bias_sc, vext_sc,
                 s_sc, smax_sc, *, tq, tk, lambda_init):
    slope = slopes_ref[pl.program_id(0)]
    hd = q_ref.shape[2]
    half = hd // 2
    n_q = q_ref.shape[1] // tq

    vext_sc[:, :hd] = v_ref[0]

    @pl.when(pl.program_id(1) == 0)
    def _():
        vext_sc[:, hd:] = jnp.ones((vext_sc.shape[0], hd), BF16)
        row = lax.broadcasted_iota(jnp.int32, (tq, tk), 0)
        col = lax.broadcasted_iota(jnp.int32, (tq, tk), 1)
        dist = (row - col).astype(F32)
        bias_sc[0] = -slope * dist
        bias_sc[1] = jnp.where(col // CHUNK <= row // CHUNK, -slope * jnp.abs(dist), NEG_BIG)

    lam = lam_ref[...]
    lam_full = (jnp.exp(jnp.sum(lam[0:1, :] * lam[1:2, :], axis=1, keepdims=True))
                - jnp.exp(jnp.sum(lam[2:3, :] * lam[3:4, :], axis=1, keepdims=True)) + lambda_init)

    def scores(qi, j, slot):
        q = q_ref[0, qi * tq:(qi + 1) * tq, :]
        lane = lax.broadcasted_iota(jnp.int32, q.shape, 1)
        zero = jnp.zeros_like(q)
        q2 = jnp.concatenate([jnp.where(lane < half, q, zero), jnp.where(lane >= half, q, zero)], axis=0)
        kj = k_ref[0, j * tk:(j + 1) * tk, :]
        s = lax.dot_general(q2, kj, (((1,), (1,)), ((), ())), preferred_element_type=F32)
        bias = bias_sc[1 if j == qi else 0]
        s = s + jnp.concatenate([bias, bias], axis=0)
        s_sc[slot] = s
        smax_sc[slot] = jnp.broadcast_to(jnp.max(s, axis=1, keepdims=True), smax_sc.shape[1:])

    def accumulate(qi, j, slot):
        const = 0.0 if j == qi else -slope * float((qi - j) * tq)
        m_prev = m_sc[...]
        m_next = jnp.maximum(m_prev, smax_sc[slot] + const)
        alpha = jnp.exp2(m_prev - m_next)
        p = jnp.exp2(s_sc[slot] - jnp.tile(m_next - const, (1, tk // LANES)))
        pv = jnp.dot(p.astype(BF16), vext_sc[j * tk:(j + 1) * tk, :], preferred_element_type=F32)
        acc_sc[...] = jnp.tile(alpha, (1, 2)) * acc_sc[...] + pv
        m_sc[...] = m_next

    def finalize(qi):
        acc = acc_sc[...]
        o_all = acc[:, :hd] / acc[:, hd:]
        o = o_all[:tq] - lam_full * o_all[tq:]
        o = _rms(o, subln_ref[...]) * (1.0 - lambda_init)
        o_ref[0, qi * tq:(qi + 1) * tq, :] = o.astype(BF16)

    blocks = [(qi, j) for qi in range(n_q) for j in range(qi + 1)]
    scores(*blocks[0], 0)
    for n, (qi, j) in enumerate(blocks):
        if n + 1 < len(blocks):
            scores(*blocks[n + 1], (n + 1) % 2)
        if j == 0:
            m_sc[...] = jnp.full(m_sc.shape, NEG_BIG, F32)
            acc_sc[...] = jnp.zeros(acc_sc.shape, F32)
        accumulate(qi, j, n % 2)
        if j == qi:
            finalize(qi)


def _diff_attention(q, k, v, lam, subln, lambda_init, b, s):
    d = q.shape[1]
    hd = d // ATTN_HEADS
    tq = tk = min(512, s)
    slopes = LOG2E * jnp.exp2(-ALIBI_MAX_BIAS * jnp.arange(1, ATTN_HEADS + 1, dtype=F32) / ATTN_HEADS)
    q3, k3, v3 = (a.reshape(b, s, d) for a in (q, k, v))
    grid_spec = pltpu.PrefetchScalarGridSpec(
        num_scalar_prefetch=1,
        grid=(ATTN_HEADS, b),
        in_specs=[pl.BlockSpec(lam.shape, lambda h, bi, sl: (0, 0)),
                  pl.BlockSpec((1, hd), lambda h, bi, sl: (0, 0)),
                  pl.BlockSpec((1, s, hd), lambda h, bi, sl: (bi, 0, h)),
                  pl.BlockSpec((1, s, hd), lambda h, bi, sl: (bi, 0, h)),
                  pl.BlockSpec((1, s, hd), lambda h, bi, sl: (bi, 0, h))],
        out_specs=pl.BlockSpec((1, s, hd), lambda h, bi, sl: (bi, 0, h)),
        scratch_shapes=[pltpu.VMEM((2 * tq, LANES), F32), pltpu.VMEM((2 * tq, 2 * hd), F32),
                        pltpu.VMEM((2, tq, tk), F32), pltpu.VMEM((s, 2 * hd), BF16),
                        pltpu.VMEM((2, 2 * tq, tk), F32), pltpu.VMEM((2, 2 * tq, LANES), F32)],
    )
    o = pl.pallas_call(
        functools.partial(_attn_kernel, tq=tq, tk=tk, lambda_init=lambda_init),
        grid_spec=grid_spec,
        out_shape=jax.ShapeDtypeStruct((b, s, d), BF16),
        compiler_params=_cparams("arbitrary", "arbitrary"),
        name="diff_attention",
    )(slopes, lam, subln.reshape(1, hd), q3, k3, v3)
    return o.reshape(b * s, d)


def _mixer_out_core(inp_bf16, w_ref, x_ref, mod_ref, g_ref, x_out_ref):
    y = jnp.dot(inp_bf16, w_ref[...], preferred_element_type=F32)
    x1 = x_ref[...] + mod_ref[0, 2:3, :] * _rms(y, g_ref[1:2, :])
    x_out_ref[...] = x1
    return _rms(x1, g_ref[2:3, :]) * (1.0 + mod_ref[0, 4:5, :]) + mod_ref[0, 3:4, :]


def _swiglu_partial(h, width, wg, wu, wd):
    out = None
    for c0 in range(0, width, FFN_SUBCHUNK):
        c = slice(c0, min(c0 + FFN_SUBCHUNK, width))
        gate = jnp.dot(h, wg(c), preferred_element_type=F32)
        up = jnp.dot(h, wu(c), preferred_element_type=F32)
        part = jnp.dot((_silu(gate) * up).astype(BF16), wd(c), preferred_element_type=F32)
        out = part if out is None else out + part
    return out


def _attn_out_ffn_kernel(o_ref, wo_ref, x_ref, mod_ref, g_ref, wg_ref, wu_ref, wd_ref, out_ref, x1_sc):
    h2 = _mixer_out_core(o_ref[...], wo_ref, x_ref, mod_ref, g_ref, x1_sc).astype(BF16)
    y = _swiglu_partial(h2, wg_ref.shape[1], lambda c: wg_ref[:, c], lambda c: wu_ref[:, c],
                        lambda c: wd_ref[c, :])
    out_ref[...] = x1_sc[...] + mod_ref[0, 5:6, :] * _rms(y, g_ref[3:4, :])


def _attn_out_ffn(o, wo, wg, wu, wd, x2d, mod, g, s):
    t, d = x2d.shape
    f = wg.shape[1]
    tm = _token_tile(s)
    tpb = s // tm
    tok = pl.BlockSpec((tm, d), lambda i: (i, 0))
    resident = lambda w: pl.BlockSpec(w.shape, lambda i: (0, 0), pipeline_mode=pl.Buffered(1))
    return pl.pallas_call(
        _attn_out_ffn_kernel,
        grid=(t // tm,),
        in_specs=[tok,
                  resident(wo),
                  tok,
                  pl.BlockSpec((1, 6, d), lambda i: (i // tpb, 0, 0)),
                  pl.BlockSpec((4, d), lambda i: (0, 0)),
                  resident(wg), resident(wu), resident(wd)],
        out_specs=tok,
        out_shape=jax.ShapeDtypeStruct((t, d), F32),
        scratch_shapes=[pltpu.VMEM((tm, d), F32)],
        compiler_params=_cparams("arbitrary"),
        name="attn_out_dense_ffn",
    )(o, wo, x2d, mod, g, wg, wu, wd)


def _mlstm_qkv_kernel(xm_ref, cw_ref, cb_ref, wqk_ref, wv_ref, wgq_ref, wgk_ref, wgv_ref, bg_ref,
                      xc_ref, q_ref, kt_ref, v_ref, gates_ref, pad_sc, *, ts, k_scale):
    c = pl.program_id(1)
    s, cw = xm_ref.shape[1], xm_ref.shape[2]
    front = SUBLANES
    pad_sc[0:front, :] = jnp.zeros((front, cw), F32)
    pad_sc[front:front + s, :] = xm_ref[0].astype(F32)

    @pl.when(c == 0)
    def _():
        gates_ref[0] = jnp.broadcast_to(bg_ref[...], gates_ref.shape[1:])

    for r in range(s // ts):
        r0 = r * ts
        conv = cb_ref[0]
        for j in reversed(range(MLSTM_CONV)):
            start = r0 + front - (MLSTM_CONV - 1) + j
            conv = conv + pad_sc[start:start + ts, :] * cw_ref[0, j:j + 1, :]
        xc = _silu(conv)
        xcb = xc.astype(BF16)
        qk = jnp.dot(xcb, wqk_ref[0], preferred_element_type=F32)
        qb = qk[:, :cw].astype(BF16)
        kb = qk[:, cw:].astype(BF16)
        vb = jnp.dot(xm_ref[0, r0:r0 + ts, :], wv_ref[0], preferred_element_type=F32).astype(BF16)
        gates_ref[0, r0:r0 + ts, :] += (jnp.dot(qb, wgq_ref[...], preferred_element_type=F32)
                                        + jnp.dot(kb, wgk_ref[...], preferred_element_type=F32)
                                        + jnp.dot(vb, wgv_ref[...], preferred_element_type=F32))
        xc_ref[0, r0:r0 + ts, :] = xcb
        q_ref[0, r0:r0 + ts, :] = qb
        kt_ref[0, :, r0:r0 + ts] = jnp.transpose(qk[:, cw:] * k_scale).astype(BF16)
        v_ref[0, r0:r0 + ts, :] = vb


def _block_diag(w, cw):
    g, qb, _ = w.shape
    per = cw // qb
    wr = w.reshape(g // per, per, qb, qb)
    eye = jnp.eye(per, dtype=w.dtype)
    return jnp.einsum("cgio,gh->cgiho", wr, eye).reshape(g // per, cw, cw)


def _mlstm_qkv(xm, conv_w, conv_b, w_q, w_k, w_v, w_gate, b_gate, b, s):
    di = xm.shape[1]
    cw = 256
    nchunk = di // cw
    nh = MLSTM_HEADS
    dh = di // nh
    ts = min(256, s)
    wqk = jnp.concatenate([_block_diag(w_q, cw), _block_diag(w_k, cw)], axis=2).astype(BF16)
    wv = _block_diag(w_v, cw).astype(BF16)
    wg = jnp.zeros((3 * di, LANES), F32).at[:, :2 * nh].set(w_gate).astype(BF16)
    bg = jnp.zeros((1, LANES), F32).at[0, :2 * nh].set(b_gate)
    xm3 = xm.reshape(b, s, di)
    blk = pl.BlockSpec((1, s, cw), lambda bi, c: (bi, 0, c))
    outs = pl.pallas_call(
        functools.partial(_mlstm_qkv_kernel, ts=ts, k_scale=dh ** -0.5),
        grid=(b, nchunk),
        in_specs=[blk,
                  pl.BlockSpec((1, MLSTM_CONV, cw), lambda bi, c: (c, 0, 0)),
                  pl.BlockSpec((1, 1, cw), lambda bi, c: (c, 0, 0)),
                  pl.BlockSpec((1, cw, 2 * cw), lambda bi, c: (c, 0, 0)),
                  pl.BlockSpec((1, cw, cw), lambda bi, c: (c, 0, 0)),
                  pl.BlockSpec((cw, LANES), lambda bi, c: (c, 0)),
                  pl.BlockSpec((cw, LANES), lambda bi, c: (nchunk + c, 0)),
                  pl.BlockSpec((cw, LANES), lambda bi, c: (2 * nchunk + c, 0)),
                  pl.BlockSpec((1, LANES), lambda bi, c: (0, 0))],
        out_specs=[blk, blk, pl.BlockSpec((1, cw, s), lambda bi, c: (bi, c, 0)), blk,
                   pl.BlockSpec((1, s, LANES), lambda bi, c: (bi, 0, 0))],
        out_shape=[jax.ShapeDtypeStruct((b, s, di), BF16)] * 2 + [jax.ShapeDtypeStruct((b, di, s), BF16)]
        + [jax.ShapeDtypeStruct((b, s, di), BF16), jax.ShapeDtypeStruct((b, s, LANES), F32)],
        scratch_shapes=[pltpu.VMEM((s + SUBLANES, cw), F32)],
        compiler_params=_cparams("arbitrary", "arbitrary"),
        name="mlstm_qkv",
    )(xm3, conv_w.reshape(MLSTM_CONV, nchunk, cw).transpose(1, 0, 2), conv_b.reshape(nchunk, 1, cw),
      wqk, wv, wg, wg, wg, bg)
    return outs


def _log_sigmoid(x):
    return jnp.minimum(x, 0.0) - jnp.log1p(jnp.exp(-jnp.abs(x)))


def _split3(x):
    hi = x.astype(BF16)
    r = x - hi.astype(F32)
    mid = r.astype(BF16)
    lo = (r - mid.astype(F32)).astype(BF16)
    return hi, mid, lo


def _mlstm_cell_kernel(q_ref, kt_ref, v_ref, gc_ref, gr_ref, ng_ref, o_ref, ct_sc, n_sc, m_sc, *, nh):
    c = pl.program_id(1)
    L = q_ref.shape[1]
    dh = q_ref.shape[2] // nh

    @pl.when(c == 0)
    def _():
        ct_sc[...] = jnp.zeros(ct_sc.shape, F32)
        n_sc[...] = jnp.zeros(n_sc.shape, F32)
        m_sc[...] = jnp.zeros(m_sc.shape, F32)

    t_idx = lax.broadcasted_iota(jnp.int32, (L, L), 0)
    s_idx = lax.broadcasted_iota(jnp.int32, (L, L), 1)
    causal = s_idx <= t_idx
    tri = causal.astype(BF16)
    tri_t = (t_idx <= s_idx).astype(F32)
    gc = gc_ref[0]
    gr = gr_ref[0]
    b_cols = sum(jnp.dot(tri, part, preferred_element_type=F32) for part in _split3(_log_sigmoid(gc)))
    b_rows = jnp.dot(_log_sigmoid(gr), tri_t, preferred_element_type=F32, precision=HIGHEST)
    lane = lax.broadcasted_iota(jnp.int32, gc.shape, 1)

    for hl in range(nh):
        h = hl
        cols = slice(hl * dh, (hl + 1) * dh)
        i_col = jnp.sum(jnp.where(lane == h, gc, 0.0), axis=1, keepdims=True)
        b_col = jnp.sum(jnp.where(lane == h + nh, b_cols, 0.0), axis=1, keepdims=True)
        i_row = gr[h:h + 1, :]
        b_row = b_rows[nh + h:nh + h + 1, :]
        u_row = i_row - b_row
        u_col = i_col - b_col

        m_prev = m_sc[hl, 0:1, 0:1]
        dlog = jnp.where(causal, b_col + u_row, NEG_BIG)
        g = b_col + m_prev
        m_t = jnp.maximum(g, jnp.max(dlog, axis=1, keepdims=True))
        qb = q_ref[0, :, cols]
        ktb = kt_ref[0, cols, :]
        vb = v_ref[0, :, cols]
        w = jnp.exp(dlog - m_t) * jnp.dot(qb, ktb, preferred_element_type=F32)
        inter = jnp.exp(g - m_t)
        num = (jnp.dot(w.astype(BF16), vb, preferred_element_type=F32)
               + inter * jnp.dot(qb, ct_sc[hl].astype(BF16), preferred_element_type=F32))
        den = (jnp.sum(w, axis=1, keepdims=True)
               + inter * jnp.dot(qb, n_sc[hl].astype(BF16), preferred_element_type=F32))
        scale = 1.0 / jnp.maximum(jnp.abs(den), jnp.exp(-m_t))
        hh = num * jnp.tile(scale, (1, dh // LANES))

        mu = jnp.mean(hh, axis=1, keepdims=True)
        cen = hh - mu
        var = jnp.mean(cen * cen, axis=1, keepdims=True)
        o_ref[0, :, cols] = (cen * lax.rsqrt(var + NORM_EPS) * ng_ref[h]).astype(BF16)

        b_last = b_row[:, L - 1:L]
        m_new = jnp.maximum(b_last + m_prev, jnp.max(b_last + u_row, axis=1, keepdims=True))
        decay = jnp.exp(b_last + m_prev - m_new)
        ws_col = jnp.exp(b_last - m_new + u_col).astype(BF16)
        ct_sc[hl] = decay * ct_sc[hl] + jnp.dot(ktb, vb * ws_col, preferred_element_type=F32)
        n_sc[hl] = decay * n_sc[hl] + jnp.dot(ktb, jnp.broadcast_to(ws_col, (L, LANES)),
                                              preferred_element_type=F32)
        m_sc[hl] = jnp.broadcast_to(m_new, m_sc.shape[1:])


def _mlstm_cell(q, kt, v, gates, norm_g, b, s):
    di = q.shape[2]
    nh = MLSTM_HEADS
    dh = di // nh
    L = min(256, s)
    gates_t = jnp.transpose(gates[:, :, :SUBLANES], (0, 2, 1))
    blk = pl.BlockSpec((1, L, di), lambda bi, c: (bi, c, 0))
    return pl.pallas_call(
        functools.partial(_mlstm_cell_kernel, nh=nh),
        grid=(b, s // L),
        in_specs=[blk,
                  pl.BlockSpec((1, di, L), lambda bi, c: (bi, 0, c)),
                  blk,
                  pl.BlockSpec((1, L, LANES), lambda bi, c: (bi, c, 0)),
                  pl.BlockSpec((1, SUBLANES, L), lambda bi, c: (bi, 0, c)),
                  pl.BlockSpec((nh, 1, dh), lambda bi, c: (0, 0, 0))],
        out_specs=blk,
        out_shape=jax.ShapeDtypeStruct((b, s, di), BF16),
        scratch_shapes=[pltpu.VMEM((nh, dh, dh), F32), pltpu.VMEM((nh, dh, LANES), F32),
                        pltpu.VMEM((nh, SUBLANES, LANES), F32)],
        compiler_params=_cparams("arbitrary", "arbitrary"),
        name="mlstm_cell",
    )(q, kt, v, gates, gates_t, norm_g.reshape(nh, 1, dh))


def _mlstm_out_kernel(hn_ref, xc_ref, z_ref, skip_ref, w_ref, x_ref, mod_ref, g_ref, wr_ref, br_ref,
                      x_out_ref, h_out_ref, route_ref, meta_ref, *, n_experts):
    inner = ((hn_ref[...].astype(F32) + skip_ref[...] * xc_ref[...].astype(F32))
             * _silu(z_ref[...].astype(F32)))
    h4 = _mixer_out_core(inner.astype(BF16), w_ref, x_ref, mod_ref, g_ref, x_out_ref)
    h_hi = h4.astype(BF16)
    h_out_ref[...] = h_hi

    tm = h4.shape[0]
    lane = lax.broadcasted_iota(jnp.int32, (tm, LANES), 1)
    h_lo = (h4 - h_hi.astype(F32)).astype(BF16)
    w_hi = wr_ref[...].astype(BF16)
    w_lo = (wr_ref[...] - w_hi.astype(F32)).astype(BF16)
    hi_terms = jnp.dot(h_hi, jnp.concatenate([w_hi, w_lo], axis=1), preferred_element_type=F32)
    logits = (hi_terms[:, :LANES] + hi_terms[:, LANES:] + jnp.dot(h_lo, w_hi, preferred_element_type=F32)
              + br_ref[...])
    logits = jnp.where(lane < n_experts, logits, NEG_BIG)
    ex = jnp.exp(logits - jnp.max(logits, axis=1, keepdims=True))
    probs = ex / jnp.sum(ex, axis=1, keepdims=True)
    probs = jnp.where(lane < n_experts, probs, -1.0)
    lane_f = lane.astype(F32)
    p0 = jnp.max(probs, axis=1, keepdims=True)
    e0 = jnp.min(jnp.where(probs == p0, lane_f, float(LANES)), axis=1, keepdims=True)
    rest = jnp.where(lane_f == e0, -1.0, probs)
    p1 = jnp.max(rest, axis=1, keepdims=True)
    e1 = jnp.min(jnp.where(rest == p1, lane_f, float(LANES)), axis=1, keepdims=True)
    tot = p0 + p1
    sel0 = lane_f == e0
    sel1 = lane_f == e1
    sel = jnp.where(sel0 | sel1, 1.0, 0.0)
    r_idx = lax.broadcasted_iota(jnp.int32, (tm, tm), 0)
    c_idx = lax.broadcasted_iota(jnp.int32, (tm, tm), 1)
    before = (c_idx < r_idx).astype(BF16)
    cum = jnp.dot(before, sel.astype(BF16), preferred_element_type=F32)
    run8 = jnp.floor((jnp.sum(sel, axis=0, keepdims=True) + 7.0) * 0.125)
    e_r = lax.broadcasted_iota(jnp.int32, (LANES, LANES), 0)
    e_c = lax.broadcasted_iota(jnp.int32, (LANES, LANES), 1)
    seg8 = jnp.dot(jnp.broadcast_to(run8, (SUBLANES, LANES)).astype(BF16), (e_r < e_c).astype(BF16),
                   preferred_element_type=F32)[0:1, :]
    pos = cum + 8.0 * seg8
    pos0 = jnp.sum(jnp.where(sel0, pos, 0.0), axis=1, keepdims=True)
    pos1 = jnp.sum(jnp.where(sel1, pos, 0.0), axis=1, keepdims=True)
    vals = (p0 / tot, p1 / tot, e0, e1, pos0, pos1)
    route = jnp.zeros((tm, LANES), F32)
    for idx, val in enumerate(vals):
        route = jnp.where(lane == idx, val, route)
    route_ref[...] = route
    meta = jnp.where(lane[0:1, :] < n_experts, run8, 0.0)
    meta = jnp.where((lane[0:1, :] >= n_experts) & (lane[0:1, :] < 2 * n_experts),
                     pltpu.roll(8.0 * seg8, n_experts, 1), meta)
    meta_ref[0] = jnp.broadcast_to(meta, meta_ref.shape[1:]).astype(jnp.int32)


def _mlstm_out(hn, xc, z, skip, w_bf16, x2d, mod, g, w_router, b_router, s):
    t, d = x2d.shape
    di = hn.shape[1]
    e = w_router.shape[1]
    tm = _token_tile(s)
    tpb = s // tm
    n_tok_tiles = t // tm
    wr = jnp.zeros((d, LANES), F32).at[:, :e].set(w_router)
    br = jnp.zeros((1, LANES), F32).at[0, :e].set(b_router)
    tok = lambda n: pl.BlockSpec((tm, n), lambda i: (i, 0))
    return pl.pallas_call(
        functools.partial(_mlstm_out_kernel, n_experts=e),
        grid=(n_tok_tiles,),
        in_specs=[tok(di), tok(di), tok(di),
                  pl.BlockSpec((1, di), lambda i: (0, 0)),
                  pl.BlockSpec(w_bf16.shape, lambda i: (0, 0)),
                  tok(d),
                  pl.BlockSpec((1, 6, d), lambda i: (i // tpb, 0, 0)),
                  pl.BlockSpec((4, d), lambda i: (0, 0)),
                  pl.BlockSpec((d, LANES), lambda i: (0, 0)),
                  pl.BlockSpec((1, LANES), lambda i: (0, 0))],
        out_specs=[tok(d), tok(d), tok(LANES), pl.BlockSpec((1, SUBLANES, LANES), lambda i: (i, 0, 0))],
        out_shape=[jax.ShapeDtypeStruct((t, d), F32), jax.ShapeDtypeStruct((t, d), BF16),
                   jax.ShapeDtypeStruct((t, LANES), F32),
                   jax.ShapeDtypeStruct((n_tok_tiles, SUBLANES, LANES), jnp.int32)],
        compiler_params=_cparams("arbitrary"),
        name="mlstm_out_router",
    )(hn, xc, z, skip.reshape(1, di), w_bf16, x2d, mod, g, wr, br)


def _dispatch_kernel(start_ref, run_ref, seg_ref, fill_ref, h_ref, route_ref, xs_ref, stage_sc, zero_sc,
                     inflight_sm, sems, *, n_experts):
    i = pl.program_id(0)
    n_steps = pl.num_programs(0)
    tm = h_ref.shape[0]
    slot = i % 2
    sizes = (DISPATCH_ROWS, SUBLANES)

    def copy(src, src_row, dst_row, rows, sem_slot):
        return pltpu.make_async_copy(src.at[pl.ds(src_row, rows)], xs_ref.at[pl.ds(dst_row, rows)],
                                     sems.at[sem_slot])

    def wait_copies(n, rows, sem_slot):
        def body(_, carry):
            copy(zero_sc, 0, 0, rows, sem_slot).wait()
            return carry
        lax.fori_loop(0, n, body, 0)

    def wait_slot(sl):
        for which, rows in enumerate(sizes):
            wait_copies(inflight_sm[sl * 2 + which], rows, sl)

    @pl.when(i == 0)
    def _():
        for idx in range(4):
            inflight_sm[idx] = 0
        zero_sc[...] = jnp.zeros(zero_sc.shape, F32)

    route_t = jnp.transpose(route_ref[...])
    r_stage = stage_sc.shape[1]
    slot_row = lax.broadcasted_iota(jnp.int32, (r_stage, tm), 0).astype(F32)
    onehot = jnp.where((route_t[4:5, :] == slot_row) | (route_t[5:6, :] == slot_row), 1.0, 0.0).astype(BF16)
    sorted_rows = jnp.dot(onehot, h_ref[...], preferred_element_type=F32)

    wait_slot(slot)
    stage_sc[slot] = sorted_rows
    n_big = 0
    n_small = 0
    for e in range(n_experts):
        rows = run_ref[i * n_experts + e] * SUBLANES
        start = start_ref[i * n_experts + e]
        seg = seg_ref[i * n_experts + e]
        full = rows // DISPATCH_ROWS
        for k in range(tm // DISPATCH_ROWS):
            @pl.when(k < full)
            def _(seg=seg, start=start, k=k):
                copy(stage_sc.at[slot], pl.multiple_of(seg + k * DISPATCH_ROWS, SUBLANES),
                     pl.multiple_of(start + k * DISPATCH_ROWS, SUBLANES), DISPATCH_ROWS, slot).start()
        rest = (rows - full * DISPATCH_ROWS) // SUBLANES

        def small(r, carry, seg=seg, start=start, full=full):
            off = full * DISPATCH_ROWS + r * SUBLANES
            copy(stage_sc.at[slot], pl.multiple_of(seg + off, SUBLANES), pl.multiple_of(start + off, SUBLANES),
                 SUBLANES, slot).start()
            return carry

        lax.fori_loop(0, rest, small, 0)
        n_big = n_big + full
        n_small = n_small + rest
    inflight_sm[slot * 2] = n_big
    inflight_sm[slot * 2 + 1] = n_small

    @pl.when(i == n_steps - 1)
    def _():
        def fill(first_row, n, rows):
            def body(j, carry):
                copy(zero_sc, 0, pl.multiple_of(first_row + j * rows, SUBLANES), rows, 2).start()
                return carry
            lax.fori_loop(0, n, body, 0)

        for e in range(n_experts):
            fill(fill_ref[e], fill_ref[n_experts + e], SUBLANES)
        fill(fill_ref[2 * n_experts], fill_ref[2 * n_experts + 1], DISPATCH_ROWS)
        wait_slot(0)
        wait_slot(1)
        for e in range(n_experts):
            wait_copies(fill_ref[n_experts + e], SUBLANES, 2)
        wait_copies(fill_ref[2 * n_experts + 1], DISPATCH_ROWS, 2)


def _dispatch(h4, route, starts, runs, segs, fill, n_rows, s, n_experts):
    t, d = h4.shape
    tm = _token_tile(s)
    r_stage = TOP_K * tm + SUBLANES * n_experts
    grid_spec = pltpu.PrefetchScalarGridSpec(
        num_scalar_prefetch=4,
        grid=(t // tm,),
        in_specs=[pl.BlockSpec((tm, d), lambda i, *_: (i, 0)),
                  pl.BlockSpec((tm, LANES), lambda i, *_: (i, 0))],
        out_specs=pl.BlockSpec(memory_space=pl.ANY),
        scratch_shapes=[pltpu.VMEM((2, r_stage, d), F32), pltpu.VMEM((DISPATCH_ROWS, d), F32),
                        pltpu.SMEM((4,), jnp.int32), pltpu.SemaphoreType.DMA((3,))],
    )
    return pl.pallas_call(
        functools.partial(_dispatch_kernel, n_experts=n_experts),
        grid_spec=grid_spec,
        out_shape=jax.ShapeDtypeStruct((n_rows, d), F32),
        compiler_params=_cparams("arbitrary"),
        name="moe_dispatch",
    )(starts, runs, segs, fill, h4, route)


def _expert_ffn_kernel(te_ref, nreal_ref, xs_ref, wg_ref, wu_ref, wd_ref, ys_ref):
    real = pl.program_id(0) < nreal_ref[0]

    @pl.when(real)
    def _():
        ys_ref[...] = _swiglu_partial(xs_ref[...].astype(BF16), wg_ref.shape[2], lambda c: wg_ref[0, :, c],
                                      lambda c: wu_ref[0, :, c], lambda c: wd_ref[0, c, :])

    @pl.when(jnp.logical_not(real))
    def _():
        ys_ref[...] = jnp.zeros(ys_ref.shape, F32)


def _expert_ffn(xs, tile_expert, n_real, wg, wu, wd):
    p, d = xs.shape
    tm = EXPERT_TILE
    expert_block = lambda w: pl.BlockSpec((1,) + w.shape[1:], lambda i, te, nr: (te[i], 0, 0),
                                          pipeline_mode=pl.Buffered(1))
    grid_spec = pltpu.PrefetchScalarGridSpec(
        num_scalar_prefetch=2,
        grid=(p // tm,),
        in_specs=[pl.BlockSpec((tm, d), lambda i, te, nr: (jnp.minimum(i, nr[0] - 1), 0)),
                  expert_block(wg), expert_block(wu), expert_block(wd)],
        out_specs=pl.BlockSpec((tm, d), lambda i, te, nr: (i, 0)),
    )
    return pl.pallas_call(
        _expert_ffn_kernel,
        grid_spec=grid_spec,
        out_shape=jax.ShapeDtypeStruct((p, d), F32),
        compiler_params=_cparams("arbitrary"),
        name="moe_expert_ffn",
    )(tile_expert, n_real, xs, wg, wu, wd)


def _combine_kernel(start_ref, run_ref, seg_ref, ys_ref, route_ref, x_ref, mod_ref, g_ref, o_ref, buf_sc, y_sc,
                    col_sc, src_sm, exp_sm, off_sm, count_sm, sems, *, n_experts):
    i = pl.program_id(0)
    tm = x_ref.shape[0]
    d = x_ref.shape[1]
    max_chunks = buf_sc.shape[1]

    def chunk_copy(slot, ci):
        return pltpu.make_async_copy(
            ys_ref.at[pl.ds(pl.multiple_of(src_sm[slot * max_chunks + ci], SUBLANES), COMBINE_ROWS)],
            buf_sc.at[slot, ci], sems.at[slot, ci])

    def fetch(tile, slot):
        n_chunks = 0
        for e in range(n_experts):
            start = start_ref[tile * n_experts + e]
            rows = run_ref[tile * n_experts + e] * SUBLANES
            seg = seg_ref[tile * n_experts + e]
            for k in range(tm // COMBINE_ROWS):
                @pl.when(rows > k * COMBINE_ROWS)
                def _(start=start, seg=seg, k=k, ci=n_chunks + k):
                    src_sm[slot * max_chunks + ci] = start + k * COMBINE_ROWS
                    exp_sm[slot * max_chunks + ci] = e
                    off_sm[slot * max_chunks + ci] = seg + k * COMBINE_ROWS
                    chunk_copy(slot, ci).start()
            n_chunks = n_chunks + (rows + COMBINE_ROWS - 1) // COMBINE_ROWS
        count_sm[slot] = n_chunks

    slot = i % 2

    @pl.when(i == 0)
    def _():
        fetch(i, slot)

    @pl.when(i + 1 < pl.num_programs(0))
    def _():
        fetch(i + 1, 1 - slot)

    route = route_ref[...]
    for idx in range(col_sc.shape[0]):
        col_sc[idx] = jnp.broadcast_to(route[:, idx:idx + 1], col_sc.shape[1:])
    lane = lax.broadcasted_iota(jnp.int32, (tm, COMBINE_ROWS), 1).astype(F32)
    y_sc[...] = jnp.zeros(y_sc.shape, F32)

    def body(ci, carry):
        chunk_copy(slot, ci).wait()
        ef = exp_sm[slot * max_chunks + ci].astype(F32)
        target = lane + off_sm[slot * max_chunks + ci].astype(F32)
        wide = lambda idx: jnp.tile(col_sc[idx], (1, COMBINE_ROWS // LANES))
        first = col_sc[2] == ef
        hit = (wide(2) == ef) & (wide(4) == target) | (wide(3) == ef) & (wide(5) == target)
        spread = jnp.where(hit, 1.0, 0.0).astype(BF16)
        rows_out = jnp.dot(spread, buf_sc[slot, ci].astype(BF16), preferred_element_type=F32)
        weight = jnp.where(first, col_sc[0], col_sc[1])
        y_sc[...] += jnp.tile(weight, (1, d // LANES)) * rows_out
        return carry

    lax.fori_loop(0, count_sm[slot], body, 0)
    o_ref[...] = x_ref[...] + mod_ref[0, 5:6, :] * _rms(y_sc[...], g_ref[3:4, :])


def _combine(ys, starts, runs, segs, route, x2d, mod, g, s, n_experts):
    t, d = x2d.shape
    tm = _token_tile(s)
    tpb = s // tm
    max_chunks = TOP_K * tm // COMBINE_ROWS + n_experts
    grid_spec = pltpu.PrefetchScalarGridSpec(
        num_scalar_prefetch=3,
        grid=(t // tm,),
        in_specs=[pl.BlockSpec(memory_space=pl.ANY),
                  pl.BlockSpec((tm, LANES), lambda i, *_: (i, 0)),
                  pl.BlockSpec((tm, d), lambda i, *_: (i, 0)),
                  pl.BlockSpec((1, 6, d), lambda i, *_: (i // tpb, 0, 0)),
                  pl.BlockSpec((4, d), lambda i, *_: (0, 0))],
        out_specs=pl.BlockSpec((tm, d), lambda i, *_: (i, 0)),
        scratch_shapes=[pltpu.VMEM((2, max_chunks, COMBINE_ROWS, d), F32), pltpu.VMEM((tm, d), F32),
                        pltpu.VMEM((6, tm, LANES), F32),
                        pltpu.SMEM((2 * max_chunks,), jnp.int32), pltpu.SMEM((2 * max_chunks,), jnp.int32),
                        pltpu.SMEM((2 * max_chunks,), jnp.int32), pltpu.SMEM((2,), jnp.int32),
                        pltpu.SemaphoreType.DMA((2, max_chunks))],
    )
    return pl.pallas_call(
        functools.partial(_combine_kernel, n_experts=n_experts),
        grid_spec=grid_spec,
        out_shape=jax.ShapeDtypeStruct((t, d), F32),
        compiler_params=_cparams("arbitrary"),
        name="moe_combine",
    )(starts, runs, segs, ys, route, x2d, mod, g)


def _moe(h4, route, meta, x2d, mod, g, wg, wu, wd, s):
    t, d = x2d.shape
    e = wg.shape[0]
    tm = EXPERT_TILE
    n_tok_tiles = t // _token_tile(s)
    n_tiles = (TOP_K * t + SUBLANES * n_tok_tiles * e) // tm + e + 1
    runs = meta[:, 0, :e]
    segs = meta[:, 0, e:2 * e]
    used = SUBLANES * jnp.sum(runs, axis=0)
    tiles_e = (used + tm - 1) // tm
    ends = jnp.cumsum(tiles_e)
    off = (ends - tiles_e) * tm
    starts = off[None, :] + SUBLANES * (jnp.cumsum(runs, axis=0) - runs)
    n_real = ends[e - 1:e].astype(jnp.int32)
    fill = jnp.concatenate([off + used, (tiles_e * tm - used) // SUBLANES,
                            n_real * tm, (n_tiles - n_real) * (tm // DISPATCH_ROWS)]).astype(jnp.int32)
    tile_ids = jnp.arange(n_tiles, dtype=jnp.int32)
    tile_expert = jnp.minimum(jnp.sum(tile_ids[:, None] >= ends[None, :], axis=1), e - 1)
    last_expert = jnp.minimum(jnp.sum(n_real[0] - 1 >= ends), e - 1)
    tile_expert = jnp.where(tile_ids < n_real[0], tile_expert, last_expert).astype(jnp.int32)
    flat = lambda a: a.reshape(-1).astype(jnp.int32)
    xs = _dispatch(h4, route, flat(starts), flat(runs), flat(segs), fill, n_tiles * tm, s, e)
    ys = _expert_ffn(xs, tile_expert, n_real, wg, wu, wd)
    return _combine(ys, flat(starts), flat(runs), flat(segs), route, x2d, mod, g, s, e)


def kernel(x, c, mod_w, mod_b, norm_g, attn_w_in, attn_w_out, attn_lambda, attn_subln, ffn_w_gate, ffn_w_up, ffn_w_down, mlstm_w_in, mlstm_conv_w, mlstm_conv_b, mlstm_w_q, mlstm_w_k, mlstm_w_v, mlstm_w_gate, mlstm_b_gate, mlstm_skip, mlstm_norm, mlstm_w_out, moe_w_router, moe_b_router, moe_w_gate, moe_w_up, moe_w_down):
    b, s, d = x.shape
    depth = mod_w.shape[0]
    mod_all = _modulation(c, mod_w, mod_b)
    x2d = x.reshape(b * s, d)
    da = d // (2 * ATTN_HEADS)
    for i in range(depth):
        mod = mod_all[i]
        g = norm_g[i]
        j = i // N_MIXERS
        if i % N_MIXERS == 0:
            lambda_init = 0.8 - 0.6 * math.exp(-0.3 * i)
            q, k, v = _in_proj(x2d, mod, g, attn_w_in[j].astype(BF16), 3, (LOG2E * da ** -0.5, 1.0, 1.0), s,
                               "attn_in_proj")
            o = _diff_attention(q, k, v, attn_lambda[j], attn_subln[j], lambda_init, b, s)
            x2d = _attn_out_ffn(o, attn_w_out[j].astype(BF16), ffn_w_gate[j].astype(BF16),
                                ffn_w_up[j].astype(BF16), ffn_w_down[j].astype(BF16), x2d, mod, g, s)
        else:
            xm, z = _in_proj(x2d, mod, g, mlstm_w_in[j].astype(BF16), 2, (1.0, 1.0), s, "mlstm_in_proj")
            xc, q, kt, v, gates = _mlstm_qkv(xm, mlstm_conv_w[j], mlstm_conv_b[j], mlstm_w_q[j], mlstm_w_k[j],
                                             mlstm_w_v[j], mlstm_w_gate[j], mlstm_b_gate[j], b, s)
            hn = _mlstm_cell(q, kt, v, gates, mlstm_norm[j], b, s)
            di = hn.shape[2]
            x2d, h4, route, meta = _mlstm_out(hn.reshape(b * s, di), xc.reshape(b * s, di), z, mlstm_skip[j],
                                              mlstm_w_out[j].astype(BF16), x2d, mod, g,
                                              moe_w_router[j], moe_b_router[j], s)
            x2d = _moe(h4, route, meta, x2d, mod, g, moe_w_gate[j].astype(BF16), moe_w_up[j].astype(BF16),
                       moe_w_down[j].astype(BF16), s)
    return x2d.reshape(b, s, d)
```

```python
import functools
import math

import jax
import jax.numpy as jnp
from jax import lax
from jax.experimental import pallas as pl
from jax.experimental.pallas import tpu as pltpu

F32 = jnp.float32
BF16 = jnp.bfloat16
HIGHEST = lax.Precision.HIGHEST

CHUNK = 64
ATTN_HEADS = 8
ALIBI_MAX_BIAS = 8.0
MLSTM_HEADS = 4
MLSTM_CONV = 4
TOP_K = 2
NORM_EPS = 1e-6
N_MIXERS = 2

LANES = 128
SUBLANES = 8
VMEM_LIMIT_BYTES = 56 * 1024 * 1024
FFN_SUBCHUNK = 256
EXPERT_TILE = 512
DISPATCH_ROWS = 128
COMBINE_ROWS = 256
NEG_BIG = -1e30
LOG2E = math.log2(math.e)


def _cparams(*sem):
    return pltpu.CompilerParams(dimension_semantics=sem, vmem_limit_bytes=VMEM_LIMIT_BYTES)


def _rms(x, g):
    return x * lax.rsqrt(jnp.mean(x * x, axis=-1, keepdims=True) + NORM_EPS) * g


def _silu(x):
    return x * jax.nn.sigmoid(x)


def _token_tile(s):
    return min(512, s)


def _mod_kernel(c_ref, w_ref, b_ref, o_ref):
    cond = _silu(c_ref[...])
    o_ref[0] = jnp.dot(cond, w_ref[0], preferred_element_type=F32, precision=HIGHEST) + b_ref[0]


def _modulation(c, mod_w, mod_b):
    depth, d, n = mod_w.shape
    b = c.shape[0]
    tn = 1536 if n % 1536 == 0 else n
    out = pl.pallas_call(
        _mod_kernel,
        grid=(depth, n // tn),
        in_specs=[pl.BlockSpec((b, d), lambda i, j: (0, 0)),
                  pl.BlockSpec((1, d, tn), lambda i, j: (i, 0, j)),
                  pl.BlockSpec((1, 1, tn), lambda i, j: (i, 0, j))],
        out_specs=pl.BlockSpec((1, b, tn), lambda i, j: (i, 0, j)),
        out_shape=jax.ShapeDtypeStruct((depth, b, n), F32),
        compiler_params=_cparams("arbitrary", "arbitrary"),
        name="modulation",
    )(c, mod_w, mod_b.reshape(depth, 1, n))
    return out.reshape(depth, b, 6, d)


def _in_proj_kernel(x_ref, mod_ref, g_ref, w_ref, *o_refs, scales):
    x = x_ref[...]
    h = _rms(x, g_ref[0:1, :]) * (1.0 + mod_ref[0, 1:2, :]) + mod_ref[0, 0:1, :]
    hb = h.astype(BF16)
    n = o_refs[0].shape[1]
    for idx, o_ref in enumerate(o_refs):
        r = jnp.dot(hb, w_ref[:, idx * n:(idx + 1) * n], preferred_element_type=F32)
        if scales[idx] != 1.0:
            r = r * scales[idx]
        o_ref[...] = r.astype(BF16)


def _in_proj(x2d, mod, g, w_bf16, n_out, scales, s, name):
    t, d = x2d.shape
    tm = _token_tile(s)
    tpb = s // tm
    n = w_bf16.shape[1] // n_out
    return pl.pallas_call(
        functools.partial(_in_proj_kernel, scales=scales),
        grid=(t // tm,),
        in_specs=[pl.BlockSpec((tm, d), lambda i: (i, 0)),
                  pl.BlockSpec((1, 6, d), lambda i: (i // tpb, 0, 0)),
                  pl.BlockSpec((4, d), lambda i: (0, 0)),
                  pl.BlockSpec(w_bf16.shape, lambda i: (0, 0))],
        out_specs=[pl.BlockSpec((tm, n), lambda i: (i, 0))] * n_out,
        out_shape=[jax.ShapeDtypeStruct((t, n), BF16)] * n_out,
        compiler_params=_cparams("arbitrary"),
        name=name,
    )(x2d, mod, g, w_bf16)


def _attn_kernel(slopes_ref, lam_ref, subln_ref, q_ref, k_ref, v_ref, o_ref, m_sc, acc_sc, bias_sc, vext_sc,
                 s_sc, smax_sc, *, tq, tk, lambda_init):
    slope = slopes_ref[pl.program_id(0)]
    hd = q_ref.shape[2]
    half = hd // 2
    n_q = q_ref.shape[1] // tq

    vext_sc[:, :hd] = v_ref[0]

    @pl.when(pl.program_id(1) == 0)
    def _():
        vext_sc[:, hd:] = jnp.ones((vext_sc.shape[0], hd), BF16)
        row = lax.broadcasted_iota(jnp.int32, (tq, tk), 0)
        col = lax.broadcasted_iota(jnp.int32, (tq, tk), 1)
        dist = (row - col).astype(F32)
        bias_sc[0] = -slope * dist
        bias_sc[1] = jnp.where(col // CHUNK <= row // CHUNK, -slope * jnp.abs(dist), NEG_BIG)

    lam = lam_ref[...]
    lam_full = (jnp.exp(jnp.sum(lam[0:1, :] * lam[1:2, :], axis=1, keepdims=True))
                - jnp.exp(jnp.sum(lam[2:3, :] * lam[3:4, :], axis=1, keepdims=True)) + lambda_init)

    def scores(qi, j, slot):
        q = q_ref[0, qi * tq:(qi + 1) * tq, :]
        lane = lax.broadcasted_iota(jnp.int32, q.shape, 1)
        zero = jnp.zeros_like(q)
        q2 = jnp.concatenate([jnp.where(lane < half, q, zero), jnp.where(lane >= half, q, zero)], axis=0)
        kj = k_ref[0, j * tk:(j + 1) * tk, :]
        s = lax.dot_general(q2, kj, (((1,), (1,)), ((), ())), preferred_element_type=F32)
        bias = bias_sc[1 if j == qi else 0]
        s = s + jnp.concatenate([bias, bias], axis=0)
        s_sc[slot] = s
        smax_sc[slot] = jnp.broadcast_to(jnp.max(s, axis=1, keepdims=True), smax_sc.shape[1:])

    def accumulate(qi, j, slot):
        const = 0.0 if j == qi else -slope * float((qi - j) * tq)
        m_prev = m_sc[...]
        m_next = jnp.maximum(m_prev, smax_sc[slot] + const)
        alpha = jnp.exp2(m_prev - m_next)
        p = jnp.exp2(s_sc[slot] - jnp.tile(m_next - const, (1, tk // LANES)))
        pv = jnp.dot(p.astype(BF16), vext_sc[j * tk:(j + 1) * tk, :], preferred_element_type=F32)
        acc_sc[...] = jnp.tile(alpha, (1, 2)) * acc_sc[...] + pv
        m_sc[...] = m_next

    def finalize(qi):
        acc = acc_sc[...]
        o_all = acc[:, :hd] / acc[:, hd:]
        o = o_all[:tq] - lam_full * o_all[tq:]
        o = _rms(o, subln_ref[...]) * (1.0 - lambda_init)
        o_ref[0, qi * tq:(qi + 1) * tq, :] = o.astype(BF16)

    blocks = [(qi, j) for qi in range(n_q) for j in range(qi + 1)]
    scores(*blocks[0], 0)
    for n, (qi, j) in enumerate(blocks):
        if n + 1 < len(blocks):
            scores(*blocks[n + 1], (n + 1) % 2)
        if j == 0:
            m_sc[...] = jnp.full(m_sc.shape, NEG_BIG, F32)
            acc_sc[...] = jnp.zeros(acc_sc.shape, F32)
        accumulate(qi, j, n % 2)
        if j == qi:
            finalize(qi)


def _diff_attention(q, k, v, lam, subln, lambda_init, b, s):
    d = q.shape[1]
    hd = d // ATTN_HEADS
    tq = tk = min(512, s)
    slopes = LOG2E * jnp.exp2(-ALIBI_MAX_BIAS * jnp.arange(1, ATTN_HEADS + 1, dtype=F32) / ATTN_HEADS)
    q3, k3, v3 = (a.reshape(b, s, d) for a in (q, k, v))
    grid_spec = pltpu.PrefetchScalarGridSpec(
        num_scalar_prefetch=1,
        grid=(ATTN_HEADS, b),
        in_specs=[pl.BlockSpec(lam.shape, lambda h, bi, sl: (0, 0)),
                  pl.BlockSpec((1, hd), lambda h, bi, sl: (0, 0)),
                  pl.BlockSpec((1, s, hd), lambda h, bi, sl: (bi, 0, h)),
                  pl.BlockSpec((1, s, hd), lambda h, bi, sl: (bi, 0, h)),
                  pl.BlockSpec((1, s, hd), lambda h, bi, sl: (bi, 0, h))],
        out_specs=pl.BlockSpec((1, s, hd), lambda h, bi, sl: (bi, 0, h)),
        scratch_shapes=[pltpu.VMEM((2 * tq, LANES), F32), pltpu.VMEM((2 * tq, 2 * hd), F32),
                        pltpu.VMEM((2, tq, tk), F32), pltpu.VMEM((s, 2 * hd), BF16),
                        pltpu.VMEM((2, 2 * tq, tk), F32), pltpu.VMEM((2, 2 * tq, LANES), F32)],
    )
    o = pl.pallas_call(
        functools.partial(_attn_kernel, tq=tq, tk=tk, lambda_init=lambda_init),
        grid_spec=grid_spec,
        out_shape=jax.ShapeDtypeStruct((b, s, d), BF16),
        compiler_params=_cparams("arbitrary", "arbitrary"),
        name="diff_attention",
    )(slopes, lam, subln.reshape(1, hd), q3, k3, v3)
    return o.reshape(b * s, d)


def _mixer_out_core(inp_bf16, w_ref, x_ref, mod_ref, g_ref, x_out_ref):
    y = jnp.dot(inp_bf16, w_ref[...], preferred_element_type=F32)
    x1 = x_ref[...] + mod_ref[0, 2:3, :] * _rms(y, g_ref[1:2, :])
    x_out_ref[...] = x1
    return _rms(x1, g_ref[2:3, :]) * (1.0 + mod_ref[0, 4:5, :]) + mod_ref[0, 3:4, :]


def _swiglu_partial(h, width, wg, wu, wd):
    out = None
    for c0 in range(0, width, FFN_SUBCHUNK):
        c = slice(c0, min(c0 + FFN_SUBCHUNK, width))
        gate = jnp.dot(h, wg(c), preferred_element_type=F32)
        up = jnp.dot(h, wu(c), preferred_element_type=F32)
        part = jnp.dot((_silu(gate) * up).astype(BF16), wd(c), preferred_element_type=F32)
        out = part if out is None else out + part
    return out


def _attn_out_ffn_kernel(o_ref, wo_ref, x_ref, mod_ref, g_ref, wg_ref, wu_ref, wd_ref, out_ref, x1_sc):
    h2 = _mixer_out_core(o_ref[...], wo_ref, x_ref, mod_ref, g_ref, x1_sc).astype(BF16)
    y = _swiglu_partial(h2, wg_ref.shape[1], lambda c: wg_ref[:, c], lambda c: wu_ref[:, c],
                        lambda c: wd_ref[c, :])
    out_ref[...] = x1_sc[...] + mod_ref[0, 5:6, :] * _rms(y, g_ref[3:4, :])


def _attn_out_ffn(o, wo, wg, wu, wd, x2d, mod, g, s):
    t, d = x2d.shape
    f = wg.shape[1]
    tm = _token_tile(s)
    tpb = s // tm
    tok = pl.BlockSpec((tm, d), lambda i: (i, 0))
    resident = lambda w: pl.BlockSpec(w.shape, lambda i: (0, 0), pipeline_mode=pl.Buffered(1))
    return pl.pallas_call(
        _attn_out_ffn_kernel,
        grid=(t // tm,),
        in_specs=[tok,
                  resident(wo),
                  tok,
                  pl.BlockSpec((1, 6, d), lambda i: (i // tpb, 0, 0)),
                  pl.BlockSpec((4, d), lambda i: (0, 0)),
                  resident(wg), resident(wu), resident(wd)],
        out_specs=tok,
        out_shape=jax.ShapeDtypeStruct((t, d), F32),
        scratch_shapes=[pltpu.VMEM((tm, d), F32)],
        compiler_params=_cparams("arbitrary"),
        name="attn_out_dense_ffn",
    )(o, wo, x2d, mod, g, wg, wu, wd)


def _mlstm_qkv_kernel(xm_ref, cw_ref, cb_ref, wqk_ref, wv_ref, wgq_ref, wgk_ref, wgv_ref, bg_ref,
                      xc_ref, q_ref, kt_ref, v_ref, gates_ref, pad_sc, *, ts, k_scale):
    c = pl.program_id(1)
    s, cw = xm_ref.shape[1], xm_ref.shape[2]
    front = SUBLANES
    pad_sc[0:front, :] = jnp.zeros((front, cw), F32)
    pad_sc[front:front + s, :] = xm_ref[0].astype(F32)

    @pl.when(c == 0)
    def _():
        gates_ref[0] = jnp.broadcast_to(bg_ref[...], gates_ref.shape[1:])

    for r in range(s // ts):
        r0 = r * ts
        conv = cb_ref[0]
        for j in reversed(range(MLSTM_CONV)):
            start = r0 + front - (MLSTM_CONV - 1) + j
            conv = conv + pad_sc[start:start + ts, :] * cw_ref[0, j:j + 1, :]
        xc = _silu(conv)
        xcb = xc.astype(BF16)
        qk = jnp.dot(xcb, wqk_ref[0], preferred_element_type=F32)
        qb = qk[:, :cw].astype(BF16)
        kb = qk[:, cw:].astype(BF16)
        vb = jnp.dot(xm_ref[0, r0:r0 + ts, :], wv_ref[0], preferred_element_type=F32).astype(BF16)
        gates_ref[0, r0:r0 + ts, :] += (jnp.dot(qb, wgq_ref[...], preferred_element_type=F32)
                                        + jnp.dot(kb, wgk_ref[...], preferred_element_type=F32)
                                        + jnp.dot(vb, wgv_ref[...], preferred_element_type=F32))
        xc_ref[0, r0:r0 + ts, :] = xcb
        q_ref[0, r0:r0 + ts, :] = qb
        kt_ref[0, :, r0:r0 + ts] = jnp.transpose(qk[:, cw:] * k_scale).astype(BF16)
        v_ref[0, r0:r0 + ts, :] = vb


def _block_diag(w, cw):
    g, qb, _ = w.shape
    per = cw // qb
    wr = w.reshape(g // per, per, qb, qb)
    eye = jnp.eye(per, dtype=w.dtype)
    return jnp.einsum("cgio,gh->cgiho", wr, eye).reshape(g // per, cw, cw)


def _mlstm_qkv(xm, conv_w, conv_b, w_q, w_k, w_v, w_gate, b_gate, b, s):
    di = xm.shape[1]
    cw = 256
    nchunk = di // cw
    nh = MLSTM_HEADS
    dh = di // nh
    ts = min(256, s)
    wqk = jnp.concatenate([_block_diag(w_q, cw), _block_diag(w_k, cw)], axis=2).astype(BF16)
    wv = _block_diag(w_v, cw).astype(BF16)
    wg = jnp.zeros((3 * di, LANES), F32).at[:, :2 * nh].set(w_gate).astype(BF16)
    bg = jnp.zeros((1, LANES), F32).at[0, :2 * nh].set(b_gate)
    xm3 = xm.reshape(b, s, di)
    blk = pl.BlockSpec((1, s, cw), lambda bi, c: (bi, 0, c))
    outs = pl.pallas_call(
        functools.partial(_mlstm_qkv_kernel, ts=ts, k_scale=dh ** -0.5),
        grid=(b, nchunk),
        in_specs=[blk,
                  pl.BlockSpec((1, MLSTM_CONV, cw), lambda bi, c: (c, 0, 0)),
                  pl.BlockSpec((1, 1, cw), lambda bi, c: (c, 0, 0)),
                  pl.BlockSpec((1, cw, 2 * cw), lambda bi, c: (c, 0, 0)),
                  pl.BlockSpec((1, cw, cw), lambda bi, c: (c, 0, 0)),
                  pl.BlockSpec((cw, LANES), lambda bi, c: (c, 0)),
                  pl.BlockSpec((cw, LANES), lambda bi, c: (nchunk + c, 0)),
                  pl.BlockSpec((cw, LANES), lambda bi, c: (2 * nchunk + c, 0)),
                  pl.BlockSpec((1, LANES), lambda bi, c: (0, 0))],
        out_specs=[blk, blk, pl.BlockSpec((1, cw, s), lambda bi, c: (bi, c, 0)), blk,
                   pl.BlockSpec((1, s, LANES), lambda bi, c: (bi, 0, 0))],
        out_shape=[jax.ShapeDtypeStruct((b, s, di), BF16)] * 2 + [jax.ShapeDtypeStruct((b, di, s), BF16)]
        + [jax.ShapeDtypeStruct((b, s, di), BF16), jax.ShapeDtypeStruct((b, s, LANES), F32)],
        scratch_shapes=[pltpu.VMEM((s + SUBLANES, cw), F32)],
        compiler_params=_cparams("arbitrary", "arbitrary"),
        name="mlstm_qkv",
    )(xm3, conv_w.reshape(MLSTM_CONV, nchunk, cw).transpose(1, 0, 2), conv_b.reshape(nchunk, 1, cw),
      wqk, wv, wg, wg, wg, bg)
    return outs


def _log_sigmoid(x):
    return jnp.minimum(x, 0.0) - jnp.log1p(jnp.exp(-jnp.abs(x)))


def _split3(x):
    hi = x.astype(BF16)
    r = x - hi.astype(F32)
    mid = r.astype(BF16)
    lo = (r - mid.astype(F32)).astype(BF16)
    return hi, mid, lo


def _mlstm_cell_kernel(q_ref, kt_ref, v_ref, gc_ref, gr_ref, ng_ref, o_ref, ct_sc, n_sc, m_sc, *, nh):
    c = pl.program_id(1)
    L = q_ref.shape[1]
    dh = q_ref.shape[2] // nh

    @pl.when(c == 0)
    def _():
        ct_sc[...] = jnp.zeros(ct_sc.shape, F32)
        n_sc[...] = jnp.zeros(n_sc.shape, F32)
        m_sc[...] = jnp.zeros(m_sc.shape, F32)

    t_idx = lax.broadcasted_iota(jnp.int32, (L, L), 0)
    s_idx = lax.broadcasted_iota(jnp.int32, (L, L), 1)
    causal = s_idx <= t_idx
    tri = causal.astype(BF16)
    tri_t = (t_idx <= s_idx).astype(F32)
    gc = gc_ref[0]
    gr = gr_ref[0]
    b_cols = sum(jnp.dot(tri, part, preferred_element_type=F32) for part in _split3(_log_sigmoid(gc)))
    b_rows = jnp.dot(_log_sigmoid(gr), tri_t, preferred_element_type=F32, precision=HIGHEST)
    lane = lax.broadcasted_iota(jnp.int32, gc.shape, 1)

    for hl in range(nh):
        h = hl
        cols = slice(hl * dh, (hl + 1) * dh)
        i_col = jnp.sum(jnp.where(lane == h, gc, 0.0), axis=1, keepdims=True)
        b_col = jnp.sum(jnp.where(lane == h + nh, b_cols, 0.0), axis=1, keepdims=True)
        i_row = gr[h:h + 1, :]
        b_row = b_rows[nh + h:nh + h + 1, :]
        u_row = i_row - b_row
        u_col = i_col - b_col

        m_prev = m_sc[hl, 0:1, 0:1]
        dlog = jnp.where(causal, b_col + u_row, NEG_BIG)
        g = b_col + m_prev
        m_t = jnp.maximum(g, jnp.max(dlog, axis=1, keepdims=True))
        qb = q_ref[0, :, cols]
        ktb = kt_ref[0, cols, :]
        vb = v_ref[0, :, cols]
        w = jnp.exp(dlog - m_t) * jnp.dot(qb, ktb, preferred_element_type=F32)
        inter = jnp.exp(g - m_t)
        num = (jnp.dot(w.astype(BF16), vb, preferred_element_type=F32)
               + inter * jnp.dot(qb, ct_sc[hl].astype(BF16), preferred_element_type=F32))
        den = (jnp.sum(w, axis=1, keepdims=True)
               + inter * jnp.dot(qb, n_sc[hl].astype(BF16), preferred_element_type=F32))
        scale = 1.0 / jnp.maximum(jnp.abs(den), jnp.exp(-m_t))
        hh = num * jnp.tile(scale, (1, dh // LANES))

        mu = jnp.mean(hh, axis=1, keepdims=True)
        cen = hh - mu
        var = jnp.mean(cen * cen, axis=1, keepdims=True)
        o_ref[0, :, cols] = (cen * lax.rsqrt(var + NORM_EPS) * ng_ref[h]).astype(BF16)

        b_last = b_row[:, L - 1:L]
        m_new = jnp.maximum(b_last + m_prev, jnp.max(b_last + u_row, axis=1, keepdims=True))
        decay = jnp.exp(b_last + m_prev - m_new)
        ws_col = jnp.exp(b_last - m_new + u_col).astype(BF16)
        ct_sc[hl] = decay * ct_sc[hl] + jnp.dot(ktb, vb * ws_col, preferred_element_type=F32)
        n_sc[hl] = decay * n_sc[hl] + jnp.dot(ktb, jnp.broadcast_to(ws_col, (L, LANES)),
                                              preferred_element_type=F32)
        m_sc[hl] = jnp.broadcast_to(m_new, m_sc.shape[1:])


def _mlstm_cell(q, kt, v, gates, norm_g, b, s):
    di = q.shape[2]
    nh = MLSTM_HEADS
    dh = di // nh
    L = min(256, s)
    gates_t = jnp.transpose(gates[:, :, :SUBLANES], (0, 2, 1))
    blk = pl.BlockSpec((1, L, di), lambda bi, c: (bi, c, 0))
    return pl.pallas_call(
        functools.partial(_mlstm_cell_kernel, nh=nh),
        grid=(b, s // L),
        in_specs=[blk,
                  pl.BlockSpec((1, di, L), lambda bi, c: (bi, 0, c)),
                  blk,
                  pl.BlockSpec((1, L, LANES), lambda bi, c: (bi, c, 0)),
                  pl.BlockSpec((1, SUBLANES, L), lambda bi, c: (bi, 0, c)),
                  pl.BlockSpec((nh, 1, dh), lambda bi, c: (0, 0, 0))],
        out_specs=blk,
        out_shape=jax.ShapeDtypeStruct((b, s, di), BF16),
        scratch_shapes=[pltpu.VMEM((nh, dh, dh), F32), pltpu.VMEM((nh, dh, LANES), F32),
                        pltpu.VMEM((nh, SUBLANES, LANES), F32)],
        compiler_params=_cparams("arbitrary", "arbitrary"),
        name="mlstm_cell",
    )(q, kt, v, gates, gates_t, norm_g.reshape(nh, 1, dh))


def _mlstm_out_kernel(hn_ref, xc_ref, z_ref, skip_ref, w_ref, x_ref, mod_ref, g_ref, wr_ref, br_ref,
                      x_out_ref, h_out_ref, route_ref, meta_ref, *, n_experts):
    inner = ((hn_ref[...].astype(F32) + skip_ref[...] * xc_ref[...].astype(F32))
             * _silu(z_ref[...].astype(F32)))
    h4 = _mixer_out_core(inner.astype(BF16), w_ref, x_ref, mod_ref, g_ref, x_out_ref)
    h_hi = h4.astype(BF16)
    h_out_ref[...] = h_hi

    tm = h4.shape[0]
    lane = lax.broadcasted_iota(jnp.int32, (tm, LANES), 1)
    h_lo = (h4 - h_hi.astype(F32)).astype(BF16)
    w_hi = wr_ref[...].astype(BF16)
    w_lo = (wr_ref[...] - w_hi.astype(F32)).astype(BF16)
    hi_terms = jnp.dot(h_hi, jnp.concatenate([w_hi, w_lo], axis=1), preferred_element_type=F32)
    logits = (hi_terms[:, :LANES] + hi_terms[:, LANES:] + jnp.dot(h_lo, w_hi, preferred_element_type=F32)
              + br_ref[...])
    logits = jnp.where(lane < n_experts, logits, NEG_BIG)
    ex = jnp.exp(logits - jnp.max(logits, axis=1, keepdims=True))
    probs = ex / jnp.sum(ex, axis=1, keepdims=True)
    probs = jnp.where(lane < n_experts, probs, -1.0)
    lane_f = lane.astype(F32)
    p0 = jnp.max(probs, axis=1, keepdims=True)
    e0 = jnp.min(jnp.where(probs == p0, lane_f, float(LANES)), axis=1, keepdims=True)
    rest = jnp.where(lane_f == e0, -1.0, probs)
    p1 = jnp.max(rest, axis=1, keepdims=True)
    e1 = jnp.min(jnp.where(rest == p1, lane_f, float(LANES)), axis=1, keepdims=True)
    tot = p0 + p1
    sel0 = lane_f == e0
    sel1 = lane_f == e1
    sel = jnp.where(sel0 | sel1, 1.0, 0.0)
    r_idx = lax.broadcasted_iota(jnp.int32, (tm, tm), 0)
    c_idx = lax.broadcasted_iota(jnp.int32, (tm, tm), 1)
    before = (c_idx < r_idx).astype(BF16)
    cum = jnp.dot(before, sel.astype(BF16), preferred_element_type=F32)
    run8 = jnp.floor((jnp.sum(sel, axis=0, keepdims=True) + 7.0) * 0.125)
    e_r = lax.broadcasted_iota(jnp.int32, (LANES, LANES), 0)
    e_c = lax.broadcasted_iota(jnp.int32, (LANES, LANES), 1)
    seg8 = jnp.dot(jnp.broadcast_to(run8, (SUBLANES, LANES)).astype(BF16), (e_r < e_c).astype(BF16),
                   preferred_element_type=F32)[0:1, :]
    pos = cum + 8.0 * seg8
    pos0 = jnp.sum(jnp.where(sel0, pos, 0.0), axis=1, keepdims=True)
    pos1 = jnp.sum(jnp.where(sel1, pos, 0.0), axis=1, keepdims=True)
    vals = (p0 / tot, p1 / tot, e0, e1, pos0, pos1)
    route = jnp.zeros((tm, LANES), F32)
    for idx, val in enumerate(vals):
        route = jnp.where(lane == idx, val, route)
    route_ref[...] = route
    meta = jnp.where(lane[0:1, :] < n_experts, run8, 0.0)
    meta = jnp.where((lane[0:1, :] >= n_experts) & (lane[0:1, :] < 2 * n_experts),
                     pltpu.roll(8.0 * seg8, n_experts, 1), meta)
    meta_ref[0] = jnp.broadcast_to(meta, meta_ref.shape[1:]).astype(jnp.int32)


def _mlstm_out(hn, xc, z, skip, w_bf16, x2d, mod, g, w_router, b_router, s):
    t, d = x2d.shape
    di = hn.shape[1]
    e = w_router.shape[1]
    tm = _token_tile(s)
    tpb = s // tm
    n_tok_tiles = t // tm
    wr = jnp.zeros((d, LANES), F32).at[:, :e].set(w_router)
    br = jnp.zeros((1, LANES), F32).at[0, :e].set(b_router)
    tok = lambda n: pl.BlockSpec((tm, n), lambda i: (i, 0))
    return pl.pallas_call(
        functools.partial(_mlstm_out_kernel, n_experts=e),
        grid=(n_tok_tiles,),
        in_specs=[tok(di), tok(di), tok(di),
                  pl.BlockSpec((1, di), lambda i: (0, 0)),
                  pl.BlockSpec(w_bf16.shape, lambda i: (0, 0)),
                  tok(d),
                  pl.BlockSpec((1, 6, d), lambda i: (i // tpb, 0, 0)),
                  pl.BlockSpec((4, d), lambda i: (0, 0)),
                  pl.BlockSpec((d, LANES), lambda i: (0, 0)),
                  pl.BlockSpec((1, LANES), lambda i: (0, 0))],
        out_specs=[tok(d), tok(d), tok(LANES), pl.BlockSpec((1, SUBLANES, LANES), lambda i: (i, 0, 0))],
        out_shape=[jax.ShapeDtypeStruct((t, d), F32), jax.ShapeDtypeStruct((t, d), BF16),
                   jax.ShapeDtypeStruct((t, LANES), F32),
                   jax.ShapeDtypeStruct((n_tok_tiles, SUBLANES, LANES), jnp.int32)],
        compiler_params=_cparams("arbitrary"),
        name="mlstm_out_router",
    )(hn, xc, z, skip.reshape(1, di), w_bf16, x2d, mod, g, wr, br)


def _dispatch_kernel(start_ref, run_ref, seg_ref, fill_ref, h_ref, route_ref, xs_ref, stage_sc, zero_sc,
                     inflight_sm, sems, *, n_experts):
    i = pl.program_id(0)
    n_steps = pl.num_programs(0)
    tm = h_ref.shape[0]
    slot = i % 2
    sizes = (DISPATCH_ROWS, SUBLANES)

    def copy(src, src_row, dst_row, rows, sem_slot):
        return pltpu.make_async_copy(src.at[pl.ds(src_row, rows)], xs_ref.at[pl.ds(dst_row, rows)],
                                     sems.at[sem_slot])

    def wait_copies(n, rows, sem_slot):
        def body(_, carry):
            copy(zero_sc, 0, 0, rows, sem_slot).wait()
            return carry
        lax.fori_loop(0, n, body, 0)

    def wait_slot(sl):
        for which, rows in enumerate(sizes):
            wait_copies(inflight_sm[sl * 2 + which], rows, sl)

    @pl.when(i == 0)
    def _():
        for idx in range(4):
            inflight_sm[idx] = 0
        zero_sc[...] = jnp.zeros(zero_sc.shape, F32)

    route_t = jnp.transpose(route_ref[...])
    r_stage = stage_sc.shape[1]
    slot_row = lax.broadcasted_iota(jnp.int32, (r_stage, tm), 0).astype(F32)
    onehot = jnp.where((route_t[4:5, :] == slot_row) | (route_t[5:6, :] == slot_row), 1.0, 0.0).astype(BF16)
    sorted_rows = jnp.dot(onehot, h_ref[...], preferred_element_type=F32)

    wait_slot(slot)
    stage_sc[slot] = sorted_rows
    n_big = 0
    n_small = 0
    for e in range(n_experts):
        rows = run_ref[i * n_experts + e] * SUBLANES
        start = start_ref[i * n_experts + e]
        seg = seg_ref[i * n_experts + e]
        full = rows // DISPATCH_ROWS
        for k in range(tm // DISPATCH_ROWS):
            @pl.when(k < full)
            def _(seg=seg, start=start, k=k):
                copy(stage_sc.at[slot], pl.multiple_of(seg + k * DISPATCH_ROWS, SUBLANES),
                     pl.multiple_of(start + k * DISPATCH_ROWS, SUBLANES), DISPATCH_ROWS, slot).start()
        rest = (rows - full * DISPATCH_ROWS) // SUBLANES

        def small(r, carry, seg=seg, start=start, full=full):
            off = full * DISPATCH_ROWS + r * SUBLANES
            copy(stage_sc.at[slot], pl.multiple_of(seg + off, SUBLANES), pl.multiple_of(start + off, SUBLANES),
                 SUBLANES, slot).start()
            return carry

        lax.fori_loop(0, rest, small, 0)
        n_big = n_big + full
        n_small = n_small + rest
    inflight_sm[slot * 2] = n_big
    inflight_sm[slot * 2 + 1] = n_small

    @pl.when(i == n_steps - 1)
    def _():
        def fill(first_row, n, rows):
            def body(j, carry):
                copy(zero_sc, 0, pl.multiple_of(first_row + j * rows, SUBLANES), rows, 2).start()
                return carry
            lax.fori_loop(0, n, body, 0)

        for e in range(n_experts):
            fill(fill_ref[e], fill_ref[n_experts + e], SUBLANES)
        fill(fill_ref[2 * n_experts], fill_ref[2 * n_experts + 1], DISPATCH_ROWS)
        wait_slot(0)
        wait_slot(1)
        for e in range(n_experts):
            wait_copies(fill_ref[n_experts + e], SUBLANES, 2)
        wait_copies(fill_ref[2 * n_experts + 1], DISPATCH_ROWS, 2)


def _dispatch(h4, route, starts, runs, segs, fill, n_rows, s, n_experts):
    t, d = h4.shape
    tm = _token_tile(s)
    r_stage = TOP_K * tm + SUBLANES * n_experts
    grid_spec = pltpu.PrefetchScalarGridSpec(
        num_scalar_prefetch=4,
        grid=(t // tm,),
        in_specs=[pl.BlockSpec((tm, d), lambda i, *_: (i, 0)),
                  pl.BlockSpec((tm, LANES), lambda i, *_: (i, 0))],
        out_specs=pl.BlockSpec(memory_space=pl.ANY),
        scratch_shapes=[pltpu.VMEM((2, r_stage, d), F32), pltpu.VMEM((DISPATCH_ROWS, d), F32),
                        pltpu.SMEM((4,), jnp.int32), pltpu.SemaphoreType.DMA((3,))],
    )
    return pl.pallas_call(
        functools.partial(_dispatch_kernel, n_experts=n_experts),
        grid_spec=grid_spec,
        out_shape=jax.ShapeDtypeStruct((n_rows, d), F32),
        compiler_params=_cparams("arbitrary"),
        name="moe_dispatch",
    )(starts, runs, segs, fill, h4, route)


def _expert_ffn_kernel(te_ref, nreal_ref, xs_ref, wg_ref, wu_ref, wd_ref, ys_ref):
    real = pl.program_id(0) < nreal_ref[0]

    @pl.when(real)
    def _():
        ys_ref[...] = _swiglu_partial(xs_ref[...].astype(BF16), wg_ref.shape[2], lambda c: wg_ref[0, :, c],
                                      lambda c: wu_ref[0, :, c], lambda c: wd_ref[0, c, :])

    @pl.when(jnp.logical_not(real))
    def _():
        ys_ref[...] = jnp.zeros(ys_ref.shape, F32)


def _expert_ffn(xs, tile_expert, n_real, wg, wu, wd):
    p, d = xs.shape
    tm = EXPERT_TILE
    expert_block = lambda w, buffers: pl.BlockSpec((1,) + w.shape[1:], lambda i, te, nr: (te[i], 0, 0),
                                                   pipeline_mode=pl.Buffered(buffers))
    grid_spec = pltpu.PrefetchScalarGridSpec(
        num_scalar_prefetch=2,
        grid=(p // tm,),
        in_specs=[pl.BlockSpec((tm, d), lambda i, te, nr: (jnp.minimum(i, nr[0] - 1), 0)),
                  expert_block(wg, 2), expert_block(wu, 1), expert_block(wd, 2)],
        out_specs=pl.BlockSpec((tm, d), lambda i, te, nr: (i, 0)),
    )
    return pl.pallas_call(
        _expert_ffn_kernel,
        grid_spec=grid_spec,
        out_shape=jax.ShapeDtypeStruct((p, d), F32),
        compiler_params=_cparams("arbitrary"),
        name="moe_expert_ffn",
    )(tile_expert, n_real, xs, wg, wu, wd)


def _combine_kernel(start_ref, run_ref, seg_ref, ys_ref, route_ref, x_ref, mod_ref, g_ref, o_ref, buf_sc, y_sc,
                    col_sc, src_sm, exp_sm, off_sm, count_sm, sems, *, n_experts):
    i = pl.program_id(0)
    tm = x_ref.shape[0]
    d = x_ref.shape[1]
    max_chunks = buf_sc.shape[1]

    def chunk_copy(slot, ci):
        return pltpu.make_async_copy(
            ys_ref.at[pl.ds(pl.multiple_of(src_sm[slot * max_chunks + ci], SUBLANES), COMBINE_ROWS)],
            buf_sc.at[slot, ci], sems.at[slot, ci])

    def fetch(tile, slot):
        n_chunks = 0
        for e in range(n_experts):
            start = start_ref[tile * n_experts + e]
            rows = run_ref[tile * n_experts + e] * SUBLANES
            seg = seg_ref[tile * n_experts + e]
            for k in range(tm // COMBINE_ROWS):
                @pl.when(rows > k * COMBINE_ROWS)
                def _(start=start, seg=seg, k=k, ci=n_chunks + k):
                    src_sm[slot * max_chunks + ci] = start + k * COMBINE_ROWS
                    exp_sm[slot * max_chunks + ci] = e
                    off_sm[slot * max_chunks + ci] = seg + k * COMBINE_ROWS
                    chunk_copy(slot, ci).start()
            n_chunks = n_chunks + (rows + COMBINE_ROWS - 1) // COMBINE_ROWS
        count_sm[slot] = n_chunks

    slot = i % 2

    @pl.when(i == 0)
    def _():
        fetch(i, slot)

    @pl.when(i + 1 < pl.num_programs(0))
    def _():
        fetch(i + 1, 1 - slot)

    route = route_ref[...]
    for idx in range(col_sc.shape[0]):
        col_sc[idx] = jnp.broadcast_to(route[:, idx:idx + 1], col_sc.shape[1:])
    lane = lax.broadcasted_iota(jnp.int32, (tm, COMBINE_ROWS), 1).astype(F32)
    y_sc[...] = jnp.zeros(y_sc.shape, F32)

    def body(ci, carry):
        chunk_copy(slot, ci).wait()
        ef = exp_sm[slot * max_chunks + ci].astype(F32)
        target = lane + off_sm[slot * max_chunks + ci].astype(F32)
        wide = lambda idx: jnp.tile(col_sc[idx], (1, COMBINE_ROWS // LANES))
        first = col_sc[2] == ef
        hit = (wide(2) == ef) & (wide(4) == target) | (wide(3) == ef) & (wide(5) == target)
        spread = jnp.where(hit, 1.0, 0.0).astype(BF16)
        rows_out = jnp.dot(spread, buf_sc[slot, ci].astype(BF16), preferred_element_type=F32)
        weight = jnp.where(first, col_sc[0], col_sc[1])
        y_sc[...] += jnp.tile(weight, (1, d // LANES)) * rows_out
        return carry

    lax.fori_loop(0, count_sm[slot], body, 0)
    o_ref[...] = x_ref[...] + mod_ref[0, 5:6, :] * _rms(y_sc[...], g_ref[3:4, :])


def _combine(ys, starts, runs, segs, route, x2d, mod, g, s, n_experts):
    t, d = x2d.shape
    tm = _token_tile(s)
    tpb = s // tm
    max_chunks = TOP_K * tm // COMBINE_ROWS + n_experts
    grid_spec = pltpu.PrefetchScalarGridSpec(
        num_scalar_prefetch=3,
        grid=(t // tm,),
        in_specs=[pl.BlockSpec(memory_space=pl.ANY),
                  pl.BlockSpec((tm, LANES), lambda i, *_: (i, 0)),
                  pl.BlockSpec((tm, d), lambda i, *_: (i, 0)),
                  pl.BlockSpec((1, 6, d), lambda i, *_: (i // tpb, 0, 0)),
                  pl.BlockSpec((4, d), lambda i, *_: (0, 0))],
        out_specs=pl.BlockSpec((tm, d), lambda i, *_: (i, 0)),
        scratch_shapes=[pltpu.VMEM((2, max_chunks, COMBINE_ROWS, d), F32), pltpu.VMEM((tm, d), F32),
                        pltpu.VMEM((6, tm, LANES), F32),
                        pltpu.SMEM((2 * max_chunks,), jnp.int32), pltpu.SMEM((2 * max_chunks,), jnp.int32),
                        pltpu.SMEM((2 * max_chunks,), jnp.int32), pltpu.SMEM((2,), jnp.int32),
                        pltpu.SemaphoreType.DMA((2, max_chunks))],
    )
    return pl.pallas_call(
        functools.partial(_combine_kernel, n_experts=n_experts),
        grid_spec=grid_spec,
        out_shape=jax.ShapeDtypeStruct((t, d), F32),
        compiler_params=_cparams("arbitrary"),
        name="moe_combine",
    )(starts, runs, segs, ys, route, x2d, mod, g)


def _moe(h4, route, meta, x2d, mod, g, wg, wu, wd, s):
    t, d = x2d.shape
    e = wg.shape[0]
    tm = EXPERT_TILE
    n_tok_tiles = t // _token_tile(s)
    n_tiles = (TOP_K * t + SUBLANES * n_tok_tiles * e) // tm + e + 1
    runs = meta[:, 0, :e]
    segs = meta[:, 0, e:2 * e]
    used = SUBLANES * jnp.sum(runs, axis=0)
    tiles_e = (used + tm - 1) // tm
    ends = jnp.cumsum(tiles_e)
    off = (ends - tiles_e) * tm
    starts = off[None, :] + SUBLANES * (jnp.cumsum(runs, axis=0) - runs)
    n_real = ends[e - 1:e].astype(jnp.int32)
    fill = jnp.concatenate([off + used, (tiles_e * tm - used) // SUBLANES,
                            n_real * tm, (n_tiles - n_real) * (tm // DISPATCH_ROWS)]).astype(jnp.int32)
    tile_ids = jnp.arange(n_tiles, dtype=jnp.int32)
    tile_expert = jnp.minimum(jnp.sum(tile_ids[:, None] >= ends[None, :], axis=1), e - 1)
    last_expert = jnp.minimum(jnp.sum(n_real[0] - 1 >= ends), e - 1)
    tile_expert = jnp.where(tile_ids < n_real[0], tile_expert, last_expert).astype(jnp.int32)
    flat = lambda a: a.reshape(-1).astype(jnp.int32)
    xs = _dispatch(h4, route, flat(starts), flat(runs), flat(segs), fill, n_tiles * tm, s, e)
    ys = _expert_ffn(xs, tile_expert, n_real, wg, wu, wd)
    return _combine(ys, flat(starts), flat(runs), flat(segs), route, x2d, mod, g, s, e)


def kernel(x, c, mod_w, mod_b, norm_g, attn_w_in, attn_w_out, attn_lambda, attn_subln, ffn_w_gate, ffn_w_up, ffn_w_down, mlstm_w_in, mlstm_conv_w, mlstm_conv_b, mlstm_w_q, mlstm_w_k, mlstm_w_v, mlstm_w_gate, mlstm_b_gate, mlstm_skip, mlstm_norm, mlstm_w_out, moe_w_router, moe_b_router, moe_w_gate, moe_w_up, moe_w_down):
    b, s, d = x.shape
    depth = mod_w.shape[0]
    mod_all = _modulation(c, mod_w, mod_b)
    x2d = x.reshape(b * s, d)
    da = d // (2 * ATTN_HEADS)
    for i in range(depth):
        mod = mod_all[i]
        g = norm_g[i]
        j = i // N_MIXERS
        if i % N_MIXERS == 0:
            lambda_init = 0.8 - 0.6 * math.exp(-0.3 * i)
            q, k, v = _in_proj(x2d, mod, g, attn_w_in[j].astype(BF16), 3, (LOG2E * da ** -0.5, 1.0, 1.0), s,
                               "attn_in_proj")
            o = _diff_attention(q, k, v, attn_lambda[j], attn_subln[j], lambda_init, b, s)
            x2d = _attn_out_ffn(o, attn_w_out[j].astype(BF16), ffn_w_gate[j].astype(BF16),
                                ffn_w_up[j].astype(BF16), ffn_w_down[j].astype(BF16), x2d, mod, g, s)
        else:
            xm, z = _in_proj(x2d, mod, g, mlstm_w_in[j].astype(BF16), 2, (1.0, 1.0), s, "mlstm_in_proj")
            xc, q, kt, v, gates = _mlstm_qkv(xm, mlstm_conv_w[j], mlstm_conv_b[j], mlstm_w_q[j], mlstm_w_k[j],
                                             mlstm_w_v[j], mlstm_w_gate[j], mlstm_b_gate[j], b, s)
            hn = _mlstm_cell(q, kt, v, gates, mlstm_norm[j], b, s)
            di = hn.shape[2]
            x2d, h4, route, meta = _mlstm_out(hn.reshape(b * s, di), xc.reshape(b * s, di), z, mlstm_skip[j],
                                              mlstm_w_out[j].astype(BF16), x2d, mod, g,
                                              moe_w_router[j], moe_b_router[j], s)
            x2d = _moe(h4, route, meta, x2d, mod, g, moe_w_gate[j].astype(BF16), moe_w_up[j].astype(BF16),
                       moe_w_down[j].astype(BF16), s)
    return x2d.reshape(b, s, d)
```

```python
import functools
import math

import jax
import jax.numpy as jnp
from jax import lax
from jax.experimental import pallas as pl
from jax.experimental.pallas import tpu as pltpu

F32 = jnp.float32
BF16 = jnp.bfloat16
HIGHEST = lax.Precision.HIGHEST

CHUNK = 64
ATTN_HEADS = 8
ALIBI_MAX_BIAS = 8.0
MLSTM_HEADS = 4
MLSTM_CONV = 4
TOP_K = 2
NORM_EPS = 1e-6
N_MIXERS = 2

LANES = 128
SUBLANES = 8
VMEM_LIMIT_BYTES = 56 * 1024 * 1024
FFN_SUBCHUNK = 256
EXPERT_TILE = 512
DISPATCH_ROWS = 128
COMBINE_ROWS = 256
NEG_BIG = -1e30
LOG2E = math.log2(math.e)


def _cparams(*sem):
    return pltpu.CompilerParams(dimension_semantics=sem, vmem_limit_bytes=VMEM_LIMIT_BYTES)


def _rms(x, g):
    return x * lax.rsqrt(jnp.mean(x * x, axis=-1, keepdims=True) + NORM_EPS) * g


def _silu(x):
    return x * jax.nn.sigmoid(x)


def _token_tile(s):
    return min(512, s)


def _mod_kernel(c_ref, w_ref, b_ref, o_ref):
    cond = _silu(c_ref[...])
    o_ref[0] = jnp.dot(cond, w_ref[0], preferred_element_type=F32, precision=HIGHEST) + b_ref[0]


def _modulation(c, mod_w, mod_b):
    depth, d, n = mod_w.shape
    b = c.shape[0]
    tn = 1536 if n % 1536 == 0 else n
    out = pl.pallas_call(
        _mod_kernel,
        grid=(depth, n // tn),
        in_specs=[pl.BlockSpec((b, d), lambda i, j: (0, 0)),
                  pl.BlockSpec((1, d, tn), lambda i, j: (i, 0, j)),
                  pl.BlockSpec((1, 1, tn), lambda i, j: (i, 0, j))],
        out_specs=pl.BlockSpec((1, b, tn), lambda i, j: (i, 0, j)),
        out_shape=jax.ShapeDtypeStruct((depth, b, n), F32),
        compiler_params=_cparams("arbitrary", "arbitrary"),
        name="modulation",
    )(c, mod_w, mod_b.reshape(depth, 1, n))
    return out.reshape(depth, b, 6, d)


def _in_proj_kernel(x_ref, mod_ref, g_ref, w_ref, *o_refs, scales):
    x = x_ref[...]
    h = _rms(x, g_ref[0:1, :]) * (1.0 + mod_ref[0, 1:2, :]) + mod_ref[0, 0:1, :]
    hb = h.astype(BF16)
    n = o_refs[0].shape[1]
    for idx, o_ref in enumerate(o_refs):
        r = jnp.dot(hb, w_ref[:, idx * n:(idx + 1) * n], preferred_element_type=F32)
        if scales[idx] != 1.0:
            r = r * scales[idx]
        o_ref[...] = r.astype(BF16)


def _in_proj(x2d, mod, g, w_bf16, n_out, scales, s, name):
    t, d = x2d.shape
    tm = _token_tile(s)
    tpb = s // tm
    n = w_bf16.shape[1] // n_out
    return pl.pallas_call(
        functools.partial(_in_proj_kernel, scales=scales),
        grid=(t // tm,),
        in_specs=[pl.BlockSpec((tm, d), lambda i: (i, 0)),
                  pl.BlockSpec((1, 6, d), lambda i: (i // tpb, 0, 0)),
                  pl.BlockSpec((4, d), lambda i: (0, 0)),
                  pl.BlockSpec(w_bf16.shape, lambda i: (0, 0))],
        out_specs=[pl.BlockSpec((tm, n), lambda i: (i, 0))] * n_out,
        out_shape=[jax.ShapeDtypeStruct((t, n), BF16)] * n_out,
        compiler_params=_cparams("arbitrary"),
        name=name,
    )(x2d, mod, g, w_bf16)


def _attn_kernel(slopes_ref, lam_ref, subln_ref, q_ref, k_ref, v_ref, o_ref, m_sc, acc_sc, bias_sc, vext_sc,
                 s_sc, smax_sc, *, tq, tk, lambda_init):
    slope = slopes_ref[pl.program_id(0)]
    hd = q_ref.shape[2]
    half = hd // 2
    n_q = q_ref.shape[1] // tq

    vext_sc[:, :hd] = v_ref[0]

    @pl.when(pl.program_id(1) == 0)
    def _():
        vext_sc[:, hd:] = jnp.ones((vext_sc.shape[0], hd), BF16)
        row = lax.broadcasted_iota(jnp.int32, (tq, tk), 0)
        col = lax.broadcasted_iota(jnp.int32, (tq, tk), 1)
        dist = (row - col).astype(F32)
        bias_sc[0] = -slope * dist
        bias_sc[1] = jnp.where(col // CHUNK <= row // CHUNK, -slope * jnp.abs(dist), NEG_BIG)

    lam = lam_ref[...]
    lam_full = (jnp.exp(jnp.sum(lam[0:1, :] * lam[1:2, :], axis=1, keepdims=True))
                - jnp.exp(jnp.sum(lam[2:3, :] * lam[3:4, :], axis=1, keepdims=True)) + lambda_init)

    def scores(qi, j, slot):
        q = q_ref[0, qi * tq:(qi + 1) * tq, :]
        lane = lax.broadcasted_iota(jnp.int32, q.shape, 1)
        zero = jnp.zeros_like(q)
        q2 = jnp.concatenate([jnp.where(lane < half, q, zero), jnp.where(lane >= half, q, zero)], axis=0)
        kj = k_ref[0, j * tk:(j + 1) * tk, :]
        s = lax.dot_general(q2, kj, (((1,), (1,)), ((), ())), preferred_element_type=F32)
        bias = bias_sc[1 if j == qi else 0]
        s = s + jnp.concatenate([bias, bias], axis=0)
        s_sc[slot] = s
        smax_sc[slot] = jnp.broadcast_to(jnp.max(s, axis=1, keepdims=True), smax_sc.shape[1:])

    def accumulate(qi, j, slot):
        const = 0.0 if j == qi else -slope * float((qi - j) * tq)
        m_prev = m_sc[...]
        m_next = jnp.maximum(m_prev, smax_sc[slot] + const)
        alpha = jnp.exp2(m_prev - m_next)
        p = jnp.exp2(s_sc[slot] - jnp.tile(m_next - const, (1, tk // LANES)))
        pv = jnp.dot(p.astype(BF16), vext_sc[j * tk:(j + 1) * tk, :], preferred_element_type=F32)
        acc_sc[...] = jnp.tile(alpha, (1, 2)) * acc_sc[...] + pv
        m_sc[...] = m_next

    def finalize(qi):
        acc = acc_sc[...]
        o_all = acc[:, :hd] / acc[:, hd:]
        o = o_all[:tq] - lam_full * o_all[tq:]
        o = _rms(o, subln_ref[...]) * (1.0 - lambda_init)
        o_ref[0, qi * tq:(qi + 1) * tq, :] = o.astype(BF16)

    blocks = [(qi, j) for qi in range(n_q) for j in range(qi + 1)]
    scores(*blocks[0], 0)
    for n, (qi, j) in enumerate(blocks):
        if n + 1 < len(blocks):
            scores(*blocks[n + 1], (n + 1) % 2)
        if j == 0:
            m_sc[...] = jnp.full(m_sc.shape, NEG_BIG, F32)
            acc_sc[...] = jnp.zeros(acc_sc.shape, F32)
        accumulate(qi, j, n % 2)
        if j == qi:
            finalize(qi)


def _diff_attention(q, k, v, lam, subln, lambda_init, b, s):
    d = q.shape[1]
    hd = d // ATTN_HEADS
    tq = tk = min(512, s)
    slopes = LOG2E * jnp.exp2(-ALIBI_MAX_BIAS * jnp.arange(1, ATTN_HEADS + 1, dtype=F32) / ATTN_HEADS)
    q3, k3, v3 = (a.reshape(b, s, d) for a in (q, k, v))
    grid_spec = pltpu.PrefetchScalarGridSpec(
        num_scalar_prefetch=1,
        grid=(ATTN_HEADS, b),
        in_specs=[pl.BlockSpec(lam.shape, lambda h, bi, sl: (0, 0)),
                  pl.BlockSpec((1, hd), lambda h, bi, sl: (0, 0)),
                  pl.BlockSpec((1, s, hd), lambda h, bi, sl: (bi, 0, h)),
                  pl.BlockSpec((1, s, hd), lambda h, bi, sl: (bi, 0, h)),
                  pl.BlockSpec((1, s, hd), lambda h, bi, sl: (bi, 0, h))],
        out_specs=pl.BlockSpec((1, s, hd), lambda h, bi, sl: (bi, 0, h)),
        scratch_shapes=[pltpu.VMEM((2 * tq, LANES), F32), pltpu.VMEM((2 * tq, 2 * hd), F32),
                        pltpu.VMEM((2, tq, tk), F32), pltpu.VMEM((s, 2 * hd), BF16),
                        pltpu.VMEM((2, 2 * tq, tk), F32), pltpu.VMEM((2, 2 * tq, LANES), F32)],
    )
    o = pl.pallas_call(
        functools.partial(_attn_kernel, tq=tq, tk=tk, lambda_init=lambda_init),
        grid_spec=grid_spec,
        out_shape=jax.ShapeDtypeStruct((b, s, d), BF16),
        compiler_params=_cparams("arbitrary", "arbitrary"),
        name="diff_attention",
    )(slopes, lam, subln.reshape(1, hd), q3, k3, v3)
    return o.reshape(b * s, d)


def _mixer_out_core(inp_bf16, w_ref, x_ref, mod_ref, g_ref, x_out_ref):
    y = jnp.dot(inp_bf16, w_ref[...], preferred_element_type=F32)
    x1 = x_ref[...] + mod_ref[0, 2:3, :] * _rms(y, g_ref[1:2, :])
    x_out_ref[...] = x1
    return _rms(x1, g_ref[2:3, :]) * (1.0 + mod_ref[0, 4:5, :]) + mod_ref[0, 3:4, :]


def _swiglu_partial(h, width, wg, wu, wd):
    out = None
    for c0 in range(0, width, FFN_SUBCHUNK):
        c = slice(c0, min(c0 + FFN_SUBCHUNK, width))
        gate = jnp.dot(h, wg(c), preferred_element_type=F32)
        up = jnp.dot(h, wu(c), preferred_element_type=F32)
        part = jnp.dot((_silu(gate) * up).astype(BF16), wd(c), preferred_element_type=F32)
        out = part if out is None else out + part
    return out


def _attn_out_ffn_kernel(o_ref, wo_ref, x_ref, mod_ref, g_ref, wg_ref, wu_ref, wd_ref, out_ref, x1_sc):
    h2 = _mixer_out_core(o_ref[...], wo_ref, x_ref, mod_ref, g_ref, x1_sc).astype(BF16)
    y = _swiglu_partial(h2, wg_ref.shape[1], lambda c: wg_ref[:, c], lambda c: wu_ref[:, c],
                        lambda c: wd_ref[c, :])
    out_ref[...] = x1_sc[...] + mod_ref[0, 5:6, :] * _rms(y, g_ref[3:4, :])


def _attn_out_ffn(o, wo, wg, wu, wd, x2d, mod, g, s):
    t, d = x2d.shape
    f = wg.shape[1]
    tm = _token_tile(s)
    tpb = s // tm
    tok = pl.BlockSpec((tm, d), lambda i: (i, 0))
    resident = lambda w: pl.BlockSpec(w.shape, lambda i: (0, 0), pipeline_mode=pl.Buffered(1))
    return pl.pallas_call(
        _attn_out_ffn_kernel,
        grid=(t // tm,),
        in_specs=[tok,
                  resident(wo),
                  tok,
                  pl.BlockSpec((1, 6, d), lambda i: (i // tpb, 0, 0)),
                  pl.BlockSpec((4, d), lambda i: (0, 0)),
                  resident(wg), resident(wu), resident(wd)],
        out_specs=tok,
        out_shape=jax.ShapeDtypeStruct((t, d), F32),
        scratch_shapes=[pltpu.VMEM((tm, d), F32)],
        compiler_params=_cparams("arbitrary"),
        name="attn_out_dense_ffn",
    )(o, wo, x2d, mod, g, wg, wu, wd)


def _mlstm_qkv_kernel(xm_ref, cw_ref, cb_ref, wqk_ref, wv_ref, wgq_ref, wgk_ref, wgv_ref, bg_ref,
                      xc_ref, q_ref, kt_ref, v_ref, gates_ref, pad_sc, *, ts, k_scale):
    c = pl.program_id(1)
    s, cw = xm_ref.shape[1], xm_ref.shape[2]
    front = SUBLANES
    pad_sc[0:front, :] = jnp.zeros((front, cw), F32)
    pad_sc[front:front + s, :] = xm_ref[0].astype(F32)

    @pl.when(c == 0)
    def _():
        gates_ref[0] = jnp.broadcast_to(bg_ref[...], gates_ref.shape[1:])

    for r in range(s // ts):
        r0 = r * ts
        conv = cb_ref[0]
        for j in reversed(range(MLSTM_CONV)):
            start = r0 + front - (MLSTM_CONV - 1) + j
            conv = conv + pad_sc[start:start + ts, :] * cw_ref[0, j:j + 1, :]
        xc = _silu(conv)
        xcb = xc.astype(BF16)
        qk = jnp.dot(xcb, wqk_ref[0], preferred_element_type=F32)
        qb = qk[:, :cw].astype(BF16)
        kb = qk[:, cw:].astype(BF16)
        vb = jnp.dot(xm_ref[0, r0:r0 + ts, :], wv_ref[0], preferred_element_type=F32).astype(BF16)
        gates_ref[0, r0:r0 + ts, :] += (jnp.dot(qb, wgq_ref[...], preferred_element_type=F32)
                                        + jnp.dot(kb, wgk_ref[...], preferred_element_type=F32)
                                        + jnp.dot(vb, wgv_ref[...], preferred_element_type=F32))
        xc_ref[0, r0:r0 + ts, :] = xcb
        q_ref[0, r0:r0 + ts, :] = qb
        kt_ref[0, :, r0:r0 + ts] = jnp.transpose(qk[:, cw:] * k_scale).astype(BF16)
        v_ref[0, r0:r0 + ts, :] = vb


def _block_diag(w, cw):
    g, qb, _ = w.shape
    per = cw // qb
    wr = w.reshape(g // per, per, qb, qb)
    eye = jnp.eye(per, dtype=w.dtype)
    return jnp.einsum("cgio,gh->cgiho", wr, eye).reshape(g // per, cw, cw)


def _mlstm_qkv(xm, conv_w, conv_b, w_q, w_k, w_v, w_gate, b_gate, b, s):
    di = xm.shape[1]
    cw = 256
    nchunk = di // cw
    nh = MLSTM_HEADS
    dh = di // nh
    ts = min(256, s)
    wqk = jnp.concatenate([_block_diag(w_q, cw), _block_diag(w_k, cw)], axis=2).astype(BF16)
    wv = _block_diag(w_v, cw).astype(BF16)
    wg = jnp.zeros((3 * di, LANES), F32).at[:, :2 * nh].set(w_gate).astype(BF16)
    bg = jnp.zeros((1, LANES), F32).at[0, :2 * nh].set(b_gate)
    xm3 = xm.reshape(b, s, di)
    blk = pl.BlockSpec((1, s, cw), lambda bi, c: (bi, 0, c))
    outs = pl.pallas_call(
        functools.partial(_mlstm_qkv_kernel, ts=ts, k_scale=dh ** -0.5),
        grid=(b, nchunk),
        in_specs=[blk,
                  pl.BlockSpec((1, MLSTM_CONV, cw), lambda bi, c: (c, 0, 0)),
                  pl.BlockSpec((1, 1, cw), lambda bi, c: (c, 0, 0)),
                  pl.BlockSpec((1, cw, 2 * cw), lambda bi, c: (c, 0, 0)),
                  pl.BlockSpec((1, cw, cw), lambda bi, c: (c, 0, 0)),
                  pl.BlockSpec((cw, LANES), lambda bi, c: (c, 0)),
                  pl.BlockSpec((cw, LANES), lambda bi, c: (nchunk + c, 0)),
                  pl.BlockSpec((cw, LANES), lambda bi, c: (2 * nchunk + c, 0)),
                  pl.BlockSpec((1, LANES), lambda bi, c: (0, 0))],
        out_specs=[blk, blk, pl.BlockSpec((1, cw, s), lambda bi, c: (bi, c, 0)), blk,
                   pl.BlockSpec((1, s, LANES), lambda bi, c: (bi, 0, 0))],
        out_shape=[jax.ShapeDtypeStruct((b, s, di), BF16)] * 2 + [jax.ShapeDtypeStruct((b, di, s), BF16)]
        + [jax.ShapeDtypeStruct((b, s, di), BF16), jax.ShapeDtypeStruct((b, s, LANES), F32)],
        scratch_shapes=[pltpu.VMEM((s + SUBLANES, cw), F32)],
        compiler_params=_cparams("arbitrary", "arbitrary"),
        name="mlstm_qkv",
    )(xm3, conv_w.reshape(MLSTM_CONV, nchunk, cw).transpose(1, 0, 2), conv_b.reshape(nchunk, 1, cw),
      wqk, wv, wg, wg, wg, bg)
    return outs


def _log_sigmoid(x):
    return jnp.minimum(x, 0.0) - jnp.log1p(jnp.exp(-jnp.abs(x)))


def _split3(x):
    hi = x.astype(BF16)
    r = x - hi.astype(F32)
    mid = r.astype(BF16)
    lo = (r - mid.astype(F32)).astype(BF16)
    return hi, mid, lo


def _mlstm_cell_kernel(q_ref, kt_ref, v_ref, gc_ref, gr_ref, ng_ref, o_ref, ct_sc, n_sc, m_sc, *, nh):
    c = pl.program_id(1)
    L = q_ref.shape[1]
    dh = q_ref.shape[2] // nh

    @pl.when(c == 0)
    def _():
        ct_sc[...] = jnp.zeros(ct_sc.shape, F32)
        n_sc[...] = jnp.zeros(n_sc.shape, F32)
        m_sc[...] = jnp.zeros(m_sc.shape, F32)

    t_idx = lax.broadcasted_iota(jnp.int32, (L, L), 0)
    s_idx = lax.broadcasted_iota(jnp.int32, (L, L), 1)
    causal = s_idx <= t_idx
    tri = causal.astype(BF16)
    tri_t = (t_idx <= s_idx).astype(F32)
    gc = gc_ref[0]
    gr = gr_ref[0]
    b_cols = sum(jnp.dot(tri, part, preferred_element_type=F32) for part in _split3(_log_sigmoid(gc)))
    b_rows = jnp.dot(_log_sigmoid(gr), tri_t, preferred_element_type=F32, precision=HIGHEST)
    lane = lax.broadcasted_iota(jnp.int32, gc.shape, 1)

    for hl in range(nh):
        h = hl
        cols = slice(hl * dh, (hl + 1) * dh)
        i_col = jnp.sum(jnp.where(lane == h, gc, 0.0), axis=1, keepdims=True)
        b_col = jnp.sum(jnp.where(lane == h + nh, b_cols, 0.0), axis=1, keepdims=True)
        i_row = gr[h:h + 1, :]
        b_row = b_rows[nh + h:nh + h + 1, :]
        u_row = i_row - b_row
        u_col = i_col - b_col

        m_prev = m_sc[hl, 0:1, 0:1]
        dlog = jnp.where(causal, b_col + u_row, NEG_BIG)
        g = b_col + m_prev
        m_t = jnp.maximum(g, jnp.max(dlog, axis=1, keepdims=True))
        qb = q_ref[0, :, cols]
        ktb = kt_ref[0, cols, :]
        vb = v_ref[0, :, cols]
        w = jnp.exp(dlog - m_t) * jnp.dot(qb, ktb, preferred_element_type=F32)
        inter = jnp.exp(g - m_t)
        num = (jnp.dot(w.astype(BF16), vb, preferred_element_type=F32)
               + inter * jnp.dot(qb, ct_sc[hl].astype(BF16), preferred_element_type=F32))
        den = (jnp.sum(w, axis=1, keepdims=True)
               + inter * jnp.dot(qb, n_sc[hl].astype(BF16), preferred_element_type=F32))
        scale = 1.0 / jnp.maximum(jnp.abs(den), jnp.exp(-m_t))
        hh = num * jnp.tile(scale, (1, dh // LANES))

        mu = jnp.mean(hh, axis=1, keepdims=True)
        cen = hh - mu
        var = jnp.mean(cen * cen, axis=1, keepdims=True)
        o_ref[0, :, cols] = (cen * lax.rsqrt(var + NORM_EPS) * ng_ref[h]).astype(BF16)

        b_last = b_row[:, L - 1:L]
        m_new = jnp.maximum(b_last + m_prev, jnp.max(b_last + u_row, axis=1, keepdims=True))
        decay = jnp.exp(b_last + m_prev - m_new)
        ws_col = jnp.exp(b_last - m_new + u_col).astype(BF16)
        ct_sc[hl] = decay * ct_sc[hl] + jnp.dot(ktb, vb * ws_col, preferred_element_type=F32)
        n_sc[hl] = decay * n_sc[hl] + jnp.dot(ktb, jnp.broadcast_to(ws_col, (L, LANES)),
                                              preferred_element_type=F32)
        m_sc[hl] = jnp.broadcast_to(m_new, m_sc.shape[1:])


def _mlstm_cell(q, kt, v, gates, norm_g, b, s):
    di = q.shape[2]
    nh = MLSTM_HEADS
    dh = di // nh
    L = min(256, s)
    gates_t = jnp.transpose(gates[:, :, :SUBLANES], (0, 2, 1))
    blk = pl.BlockSpec((1, L, di), lambda bi, c: (bi, c, 0))
    return pl.pallas_call(
        functools.partial(_mlstm_cell_kernel, nh=nh),
        grid=(b, s // L),
        in_specs=[blk,
                  pl.BlockSpec((1, di, L), lambda bi, c: (bi, 0, c)),
                  blk,
                  pl.BlockSpec((1, L, LANES), lambda bi, c: (bi, c, 0)),
                  pl.BlockSpec((1, SUBLANES, L), lambda bi, c: (bi, 0, c)),
                  pl.BlockSpec((nh, 1, dh), lambda bi, c: (0, 0, 0))],
        out_specs=blk,
        out_shape=jax.ShapeDtypeStruct((b, s, di), BF16),
        scratch_shapes=[pltpu.VMEM((nh, dh, dh), F32), pltpu.VMEM((nh, dh, LANES), F32),
                        pltpu.VMEM((nh, SUBLANES, LANES), F32)],
        compiler_params=_cparams("arbitrary", "arbitrary"),
        name="mlstm_cell",
    )(q, kt, v, gates, gates_t, norm_g.reshape(nh, 1, dh))


def _mlstm_out_kernel(hn_ref, xc_ref, z_ref, skip_ref, w_ref, x_ref, mod_ref, g_ref, wr_ref, br_ref,
                      x_out_ref, h_out_ref, route_ref, meta_ref, *, n_experts):
    inner = (hn_ref[...] + skip_ref[...].astype(BF16) * xc_ref[...]) * _silu(z_ref[...])
    h4 = _mixer_out_core(inner, w_ref, x_ref, mod_ref, g_ref, x_out_ref)
    h_hi = h4.astype(BF16)
    h_out_ref[...] = h_hi

    tm = h4.shape[0]
    lane = lax.broadcasted_iota(jnp.int32, (tm, LANES), 1)
    h_lo = (h4 - h_hi.astype(F32)).astype(BF16)
    w_hi = wr_ref[...].astype(BF16)
    w_lo = (wr_ref[...] - w_hi.astype(F32)).astype(BF16)
    hi_terms = jnp.dot(h_hi, jnp.concatenate([w_hi, w_lo], axis=1), preferred_element_type=F32)
    logits = (hi_terms[:, :LANES] + hi_terms[:, LANES:] + jnp.dot(h_lo, w_hi, preferred_element_type=F32)
              + br_ref[...])
    logits = jnp.where(lane < n_experts, logits, NEG_BIG)
    ex = jnp.exp(logits - jnp.max(logits, axis=1, keepdims=True))
    probs = ex / jnp.sum(ex, axis=1, keepdims=True)
    probs = jnp.where(lane < n_experts, probs, -1.0)
    lane_f = lane.astype(F32)
    p0 = jnp.max(probs, axis=1, keepdims=True)
    e0 = jnp.min(jnp.where(probs == p0, lane_f, float(LANES)), axis=1, keepdims=True)
    rest = jnp.where(lane_f == e0, -1.0, probs)
    p1 = jnp.max(rest, axis=1, keepdims=True)
    e1 = jnp.min(jnp.where(rest == p1, lane_f, float(LANES)), axis=1, keepdims=True)
    tot = p0 + p1
    sel0 = lane_f == e0
    sel1 = lane_f == e1
    sel = jnp.where(sel0 | sel1, 1.0, 0.0)
    r_idx = lax.broadcasted_iota(jnp.int32, (tm, tm), 0)
    c_idx = lax.broadcasted_iota(jnp.int32, (tm, tm), 1)
    before = (c_idx < r_idx).astype(BF16)
    cum = jnp.dot(before, sel.astype(BF16), preferred_element_type=F32)
    run8 = jnp.floor((jnp.sum(sel, axis=0, keepdims=True) + 7.0) * 0.125)
    e_r = lax.broadcasted_iota(jnp.int32, (LANES, LANES), 0)
    e_c = lax.broadcasted_iota(jnp.int32, (LANES, LANES), 1)
    seg8 = jnp.dot(jnp.broadcast_to(run8, (SUBLANES, LANES)).astype(BF16), (e_r < e_c).astype(BF16),
                   preferred_element_type=F32)[0:1, :]
    pos = cum + 8.0 * seg8
    pos0 = jnp.sum(jnp.where(sel0, pos, 0.0), axis=1, keepdims=True)
    pos1 = jnp.sum(jnp.where(sel1, pos, 0.0), axis=1, keepdims=True)
    vals = (p0 / tot, p1 / tot, e0, e1, pos0, pos1)
    route = jnp.zeros((tm, LANES), F32)
    for idx, val in enumerate(vals):
        route = jnp.where(lane == idx, val, route)
    route_ref[...] = route
    meta = jnp.where(lane[0:1, :] < n_experts, run8, 0.0)
    meta = jnp.where((lane[0:1, :] >= n_experts) & (lane[0:1, :] < 2 * n_experts),
                     pltpu.roll(8.0 * seg8, n_experts, 1), meta)
    meta_ref[0] = jnp.broadcast_to(meta, meta_ref.shape[1:]).astype(jnp.int32)


def _mlstm_out(hn, xc, z, skip, w_bf16, x2d, mod, g, w_router, b_router, s):
    t, d = x2d.shape
    di = hn.shape[1]
    e = w_router.shape[1]
    tm = _token_tile(s)
    tpb = s // tm
    n_tok_tiles = t // tm
    wr = jnp.zeros((d, LANES), F32).at[:, :e].set(w_router)
    br = jnp.zeros((1, LANES), F32).at[0, :e].set(b_router)
    tok = lambda n: pl.BlockSpec((tm, n), lambda i: (i, 0))
    return pl.pallas_call(
        functools.partial(_mlstm_out_kernel, n_experts=e),
        grid=(n_tok_tiles,),
        in_specs=[tok(di), tok(di), tok(di),
                  pl.BlockSpec((1, di), lambda i: (0, 0)),
                  pl.BlockSpec(w_bf16.shape, lambda i: (0, 0)),
                  tok(d),
                  pl.BlockSpec((1, 6, d), lambda i: (i // tpb, 0, 0)),
                  pl.BlockSpec((4, d), lambda i: (0, 0)),
                  pl.BlockSpec((d, LANES), lambda i: (0, 0)),
                  pl.BlockSpec((1, LANES), lambda i: (0, 0))],
        out_specs=[tok(d), tok(d), tok(LANES), pl.BlockSpec((1, SUBLANES, LANES), lambda i: (i, 0, 0))],
        out_shape=[jax.ShapeDtypeStruct((t, d), F32), jax.ShapeDtypeStruct((t, d), BF16),
                   jax.ShapeDtypeStruct((t, LANES), F32),
                   jax.ShapeDtypeStruct((n_tok_tiles, SUBLANES, LANES), jnp.int32)],
        compiler_params=_cparams("arbitrary"),
        name="mlstm_out_router",
    )(hn, xc, z, skip.reshape(1, di), w_bf16, x2d, mod, g, wr, br)


def _dispatch_kernel(start_ref, run_ref, seg_ref, fill_ref, h_ref, route_ref, xs_ref, stage_sc, zero_sc,
                     inflight_sm, sems, *, n_experts):
    i = pl.program_id(0)
    n_steps = pl.num_programs(0)
    tm = h_ref.shape[0]
    slot = i % 2
    sizes = (DISPATCH_ROWS, SUBLANES)

    def copy(src, src_row, dst_row, rows, sem_slot):
        return pltpu.make_async_copy(src.at[pl.ds(src_row, rows)], xs_ref.at[pl.ds(dst_row, rows)],
                                     sems.at[sem_slot])

    def wait_copies(n, rows, sem_slot):
        def body(_, carry):
            copy(zero_sc, 0, 0, rows, sem_slot).wait()
            return carry
        lax.fori_loop(0, n, body, 0)

    def wait_slot(sl):
        for which, rows in enumerate(sizes):
            wait_copies(inflight_sm[sl * 2 + which], rows, sl)

    @pl.when(i == 0)
    def _():
        for idx in range(4):
            inflight_sm[idx] = 0
        zero_sc[...] = jnp.zeros(zero_sc.shape, F32)

    route_t = jnp.transpose(route_ref[...])
    r_stage = stage_sc.shape[1]
    slot_row = lax.broadcasted_iota(jnp.int32, (r_stage, tm), 0).astype(F32)
    onehot = jnp.where((route_t[4:5, :] == slot_row) | (route_t[5:6, :] == slot_row), 1.0, 0.0).astype(BF16)
    sorted_rows = jnp.dot(onehot, h_ref[...], preferred_element_type=F32)

    wait_slot(slot)
    stage_sc[slot] = sorted_rows
    n_big = 0
    n_small = 0
    for e in range(n_experts):
        rows = run_ref[i * n_experts + e] * SUBLANES
        start = start_ref[i * n_experts + e]
        seg = seg_ref[i * n_experts + e]
        full = rows // DISPATCH_ROWS
        for k in range(tm // DISPATCH_ROWS):
            @pl.when(k < full)
            def _(seg=seg, start=start, k=k):
                copy(stage_sc.at[slot], pl.multiple_of(seg + k * DISPATCH_ROWS, SUBLANES),
                     pl.multiple_of(start + k * DISPATCH_ROWS, SUBLANES), DISPATCH_ROWS, slot).start()
        rest = (rows - full * DISPATCH_ROWS) // SUBLANES

        def small(r, carry, seg=seg, start=start, full=full):
            off = full * DISPATCH_ROWS + r * SUBLANES
            copy(stage_sc.at[slot], pl.multiple_of(seg + off, SUBLANES), pl.multiple_of(start + off, SUBLANES),
                 SUBLANES, slot).start()
            return carry

        lax.fori_loop(0, rest, small, 0)
        n_big = n_big + full
        n_small = n_small + rest
    inflight_sm[slot * 2] = n_big
    inflight_sm[slot * 2 + 1] = n_small

    @pl.when(i == n_steps - 1)
    def _():
        def fill(first_row, n, rows):
            def body(j, carry):
                copy(zero_sc, 0, pl.multiple_of(first_row + j * rows, SUBLANES), rows, 2).start()
                return carry
            lax.fori_loop(0, n, body, 0)

        for e in range(n_experts):
            fill(fill_ref[e], fill_ref[n_experts + e], SUBLANES)
        fill(fill_ref[2 * n_experts], fill_ref[2 * n_experts + 1], DISPATCH_ROWS)
        wait_slot(0)
        wait_slot(1)
        for e in range(n_experts):
            wait_copies(fill_ref[n_experts + e], SUBLANES, 2)
        wait_copies(fill_ref[2 * n_experts + 1], DISPATCH_ROWS, 2)


def _dispatch(h4, route, starts, runs, segs, fill, n_rows, s, n_experts):
    t, d = h4.shape
    tm = _token_tile(s)
    r_stage = TOP_K * tm + SUBLANES * n_experts
    grid_spec = pltpu.PrefetchScalarGridSpec(
        num_scalar_prefetch=4,
        grid=(t // tm,),
        in_specs=[pl.BlockSpec((tm, d), lambda i, *_: (i, 0)),
                  pl.BlockSpec((tm, LANES), lambda i, *_: (i, 0))],
        out_specs=pl.BlockSpec(memory_space=pl.ANY),
        scratch_shapes=[pltpu.VMEM((2, r_stage, d), F32), pltpu.VMEM((DISPATCH_ROWS, d), F32),
                        pltpu.SMEM((4,), jnp.int32), pltpu.SemaphoreType.DMA((3,))],
    )
    return pl.pallas_call(
        functools.partial(_dispatch_kernel, n_experts=n_experts),
        grid_spec=grid_spec,
        out_shape=jax.ShapeDtypeStruct((n_rows, d), F32),
        compiler_params=_cparams("arbitrary"),
        name="moe_dispatch",
    )(starts, runs, segs, fill, h4, route)


def _expert_ffn_kernel(te_ref, nreal_ref, xs_ref, wg_ref, wu_ref, wd_ref, ys_ref):
    real = pl.program_id(0) < nreal_ref[0]

    @pl.when(real)
    def _():
        ys_ref[...] = _swiglu_partial(xs_ref[...].astype(BF16), wg_ref.shape[2], lambda c: wg_ref[0, :, c],
                                      lambda c: wu_ref[0, :, c], lambda c: wd_ref[0, c, :])

    @pl.when(jnp.logical_not(real))
    def _():
        ys_ref[...] = jnp.zeros(ys_ref.shape, F32)


def _expert_ffn(xs, tile_expert, n_real, wg, wu, wd):
    p, d = xs.shape
    tm = EXPERT_TILE
    expert_block = lambda w, buffers: pl.BlockSpec((1,) + w.shape[1:], lambda i, te, nr: (te[i], 0, 0),
                                                   pipeline_mode=pl.Buffered(buffers))
    grid_spec = pltpu.PrefetchScalarGridSpec(
        num_scalar_prefetch=2,
        grid=(p // tm,),
        in_specs=[pl.BlockSpec((tm, d), lambda i, te, nr: (jnp.minimum(i, nr[0] - 1), 0)),
                  expert_block(wg, 2), expert_block(wu, 1), expert_block(wd, 2)],
        out_specs=pl.BlockSpec((tm, d), lambda i, te, nr: (i, 0)),
    )
    return pl.pallas_call(
        _expert_ffn_kernel,
        grid_spec=grid_spec,
        out_shape=jax.ShapeDtypeStruct((p, d), F32),
        compiler_params=_cparams("arbitrary"),
        name="moe_expert_ffn",
    )(tile_expert, n_real, xs, wg, wu, wd)


def _combine_kernel(start_ref, run_ref, seg_ref, ys_ref, route_ref, x_ref, mod_ref, g_ref, o_ref, buf_sc, y_sc,
                    col_sc, src_sm, exp_sm, off_sm, count_sm, sems, *, n_experts):
    i = pl.program_id(0)
    tm = x_ref.shape[0]
    d = x_ref.shape[1]
    max_chunks = buf_sc.shape[1]

    def chunk_copy(slot, ci):
        return pltpu.make_async_copy(
            ys_ref.at[pl.ds(pl.multiple_of(src_sm[slot * max_chunks + ci], SUBLANES), COMBINE_ROWS)],
            buf_sc.at[slot, ci], sems.at[slot, ci])

    def fetch(tile, slot):
        n_chunks = 0
        for e in range(n_experts):
            start = start_ref[tile * n_experts + e]
            rows = run_ref[tile * n_experts + e] * SUBLANES
            seg = seg_ref[tile * n_experts + e]
            for k in range(tm // COMBINE_ROWS):
                @pl.when(rows > k * COMBINE_ROWS)
                def _(start=start, seg=seg, k=k, ci=n_chunks + k):
                    src_sm[slot * max_chunks + ci] = start + k * COMBINE_ROWS
                    exp_sm[slot * max_chunks + ci] = e
                    off_sm[slot * max_chunks + ci] = seg + k * COMBINE_ROWS
                    chunk_copy(slot, ci).start()
            n_chunks = n_chunks + (rows + COMBINE_ROWS - 1) // COMBINE_ROWS
        count_sm[slot] = n_chunks

    slot = i % 2

    @pl.when(i == 0)
    def _():
        fetch(i, slot)

    @pl.when(i + 1 < pl.num_programs(0))
    def _():
        fetch(i + 1, 1 - slot)

    route = route_ref[...]
    for idx in range(col_sc.shape[0]):
        col_sc[idx] = jnp.broadcast_to(route[:, idx:idx + 1], col_sc.shape[1:])
    lane = lax.broadcasted_iota(jnp.int32, (tm, COMBINE_ROWS), 1).astype(F32)
    y_sc[...] = jnp.zeros(y_sc.shape, F32)

    def body(ci, carry):
        chunk_copy(slot, ci).wait()
        ef = exp_sm[slot * max_chunks + ci].astype(F32)
        target = lane + off_sm[slot * max_chunks + ci].astype(F32)
        wide = lambda idx: jnp.tile(col_sc[idx], (1, COMBINE_ROWS // LANES))
        first = col_sc[2] == ef
        hit = (wide(2) == ef) & (wide(4) == target) | (wide(3) == ef) & (wide(5) == target)
        spread = jnp.where(hit, 1.0, 0.0).astype(BF16)
        rows_out = jnp.dot(spread, buf_sc[slot, ci].astype(BF16), preferred_element_type=F32)
        weight = jnp.where(first, col_sc[0], col_sc[1])
        y_sc[...] += jnp.tile(weight, (1, d // LANES)) * rows_out
        return carry

    lax.fori_loop(0, count_sm[slot], body, 0)
    o_ref[...] = x_ref[...] + mod_ref[0, 5:6, :] * _rms(y_sc[...], g_ref[3:4, :])


def _combine(ys, starts, runs, segs, route, x2d, mod, g, s, n_experts):
    t, d = x2d.shape
    tm = _token_tile(s)
    tpb = s // tm
    max_chunks = TOP_K * tm // COMBINE_ROWS + n_experts
    grid_spec = pltpu.PrefetchScalarGridSpec(
        num_scalar_prefetch=3,
        grid=(t // tm,),
        in_specs=[pl.BlockSpec(memory_space=pl.ANY),
                  pl.BlockSpec((tm, LANES), lambda i, *_: (i, 0)),
                  pl.BlockSpec((tm, d), lambda i, *_: (i, 0)),
                  pl.BlockSpec((1, 6, d), lambda i, *_: (i // tpb, 0, 0)),
                  pl.BlockSpec((4, d), lambda i, *_: (0, 0))],
        out_specs=pl.BlockSpec((tm, d), lambda i, *_: (i, 0)),
        scratch_shapes=[pltpu.VMEM((2, max_chunks, COMBINE_ROWS, d), F32), pltpu.VMEM((tm, d), F32),
                        pltpu.VMEM((6, tm, LANES), F32),
                        pltpu.SMEM((2 * max_chunks,), jnp.int32), pltpu.SMEM((2 * max_chunks,), jnp.int32),
                        pltpu.SMEM((2 * max_chunks,), jnp.int32), pltpu.SMEM((2,), jnp.int32),
                        pltpu.SemaphoreType.DMA((2, max_chunks))],
    )
    return pl.pallas_call(
        functools.partial(_combine_kernel, n_experts=n_experts),
        grid_spec=grid_spec,
        out_shape=jax.ShapeDtypeStruct((t, d), F32),
        compiler_params=_cparams("arbitrary"),
        name="moe_combine",
    )(starts, runs, segs, ys, route, x2d, mod, g)


def _moe(h4, route, meta, x2d, mod, g, wg, wu, wd, s):
    t, d = x2d.shape
    e = wg.shape[0]
    tm = EXPERT_TILE
    n_tok_tiles = t // _token_tile(s)
    n_tiles = (TOP_K * t + SUBLANES * n_tok_tiles * e) // tm + e + 1
    runs = meta[:, 0, :e]
    segs = meta[:, 0, e:2 * e]
    used = SUBLANES * jnp.sum(runs, axis=0)
    tiles_e = (used + tm - 1) // tm
    ends = jnp.cumsum(tiles_e)
    off = (ends - tiles_e) * tm
    starts = off[None, :] + SUBLANES * (jnp.cumsum(runs, axis=0) - runs)
    n_real = ends[e - 1:e].astype(jnp.int32)
    fill = jnp.concatenate([off + used, (tiles_e * tm - used) // SUBLANES,
                            n_real * tm, (n_tiles - n_real) * (tm // DISPATCH_ROWS)]).astype(jnp.int32)
    tile_ids = jnp.arange(n_tiles, dtype=jnp.int32)
    tile_expert = jnp.minimum(jnp.sum(tile_ids[:, None] >= ends[None, :], axis=1), e - 1)
    last_expert = jnp.minimum(jnp.sum(n_real[0] - 1 >= ends), e - 1)
    tile_expert = jnp.where(tile_ids < n_real[0], tile_expert, last_expert).astype(jnp.int32)
    flat = lambda a: a.reshape(-1).astype(jnp.int32)
    xs = _dispatch(h4, route, flat(starts), flat(runs), flat(segs), fill, n_tiles * tm, s, e)
    ys = _expert_ffn(xs, tile_expert, n_real, wg, wu, wd)
    return _combine(ys, flat(starts), flat(runs), flat(segs), route, x2d, mod, g, s, e)


def kernel(x, c, mod_w, mod_b, norm_g, attn_w_in, attn_w_out, attn_lambda, attn_subln, ffn_w_gate, ffn_w_up, ffn_w_down, mlstm_w_in, mlstm_conv_w, mlstm_conv_b, mlstm_w_q, mlstm_w_k, mlstm_w_v, mlstm_w_gate, mlstm_b_gate, mlstm_skip, mlstm_norm, mlstm_w_out, moe_w_router, moe_b_router, moe_w_gate, moe_w_up, moe_w_down):
    b, s, d = x.shape
    depth = mod_w.shape[0]
    mod_all = _modulation(c, mod_w, mod_b)
    x2d = x.reshape(b * s, d)
    da = d // (2 * ATTN_HEADS)
    for i in range(depth):
        mod = mod_all[i]
        g = norm_g[i]
        j = i // N_MIXERS
        if i % N_MIXERS == 0:
            lambda_init = 0.8 - 0.6 * math.exp(-0.3 * i)
            q, k, v = _in_proj(x2d, mod, g, attn_w_in[j].astype(BF16), 3, (LOG2E * da ** -0.5, 1.0, 1.0), s,
                               "attn_in_proj")
            o = _diff_attention(q, k, v, attn_lambda[j], attn_subln[j], lambda_init, b, s)
            x2d = _attn_out_ffn(o, attn_w_out[j].astype(BF16), ffn_w_gate[j].astype(BF16),
                                ffn_w_up[j].astype(BF16), ffn_w_down[j].astype(BF16), x2d, mod, g, s)
        else:
            xm, z = _in_proj(x2d, mod, g, mlstm_w_in[j].astype(BF16), 2, (1.0, 1.0), s, "mlstm_in_proj")
            xc, q, kt, v, gates = _mlstm_qkv(xm, mlstm_conv_w[j], mlstm_conv_b[j], mlstm_w_q[j], mlstm_w_k[j],
                                             mlstm_w_v[j], mlstm_w_gate[j], mlstm_b_gate[j], b, s)
            hn = _mlstm_cell(q, kt, v, gates, mlstm_norm[j], b, s)
            di = hn.shape[2]
            x2d, h4, route, meta = _mlstm_out(hn.reshape(b * s, di), xc.reshape(b * s, di), z, mlstm_skip[j],
                                              mlstm_w_out[j].astype(BF16), x2d, mod, g,
                                              moe_w_router[j], moe_b_router[j], s)
            x2d = _moe(h4, route, meta, x2d, mod, g, moe_w_gate[j].astype(BF16), moe_w_up[j].astype(BF16),
                       moe_w_down[j].astype(BF16), s)
    return x2d.reshape(b, s, d)
```
